```python
import jax, jax.numpy as jnp
from jax import lax
import numpy as np

D_MODEL = 1024
BATCH = 4
SEQ = 4096
DEPTH = 2

GRID_W = 64
CTX_LEN = 256
D_MIX = D_MODEL
D_ATTN = D_MIX // 2
D_LRU = D_MIX - D_ATTN
HEAD_DIM = 64
N_HEADS = D_ATTN // HEAD_DIM
N_KV_HEADS = 2
KV_GROUP = N_HEADS // N_KV_HEADS
KV_W = N_KV_HEADS * HEAD_DIM
LRU_BLOCKS = 8
LRU_BLOCK_W = D_LRU // LRU_BLOCKS
LRU_CONV_W = 4
LRU_C = 8.0
D_FF = 2816
FFN_CONV_W = 3
Q_BLOCK = 128
ROPE_THETA = 10000.0
ROPE_PAIRS_PER_AXIS = HEAD_DIM // 4
EPS = 1e-6
D_IN = D_ATTN + 2 * KV_W + 2 * D_LRU
IN_SPLITS = (D_ATTN, D_ATTN + KV_W, D_ATTN + 2 * KV_W, D_ATTN + 2 * KV_W + D_LRU)

kernel_name = "hymba_griffin_gqa_convffn_dit"


def rms_norm(x, w=None):
    xf = x.astype(jnp.float32)
    y = xf * lax.rsqrt(jnp.mean(xf * xf, axis=-1, keepdims=True) + EPS)
    if w is not None:
        y = y * w.astype(jnp.float32)
    return y.astype(x.dtype)


def modulate(h, shift, scale):
    return h * (1 + scale) + shift


def dwconv_centred(x, w, b):
    width = w.shape[0]
    left = width // 2
    right = width - 1 - left
    T = x.shape[1]
    xp = jnp.pad(x, ((0, 0), (left, right), (0, 0)))
    y = xp[:, 0:T] * w[0]
    for k in range(1, width):
        y = y + xp[:, k:k + T] * w[k]
    return y + b


def grid_rope_tables(S):
    rows = S // GRID_W
    row = jnp.repeat(jnp.arange(rows, dtype=jnp.float32), GRID_W)
    col = jnp.tile(jnp.arange(GRID_W, dtype=jnp.float32), rows)
    inv = ROPE_THETA ** (-jnp.arange(ROPE_PAIRS_PER_AXIS, dtype=jnp.float32) / ROPE_PAIRS_PER_AXIS)
    ang = jnp.concatenate([row[:, None] * inv, col[:, None] * inv], axis=-1)
    return jnp.cos(ang)[:, None, :], jnp.sin(ang)[:, None, :]


def rope_2d(t, cos, sin):
    half = HEAD_DIM // 2
    t1 = t[..., :half].astype(jnp.float32)
    t2 = t[..., half:].astype(jnp.float32)
    return jnp.concatenate([t1 * cos - t2 * sin, t2 * cos + t1 * sin], axis=-1).astype(t.dtype)


def gqa_softmax(q, k, v):
    s = jnp.einsum('bqkgd,blkd->bkgql', q, k).astype(jnp.float32) * (HEAD_DIM ** -0.5)
    p = jax.nn.softmax(s, axis=-1).astype(v.dtype)
    return jnp.einsum('bkgql,blkd->bqkgd', p, v)


def attention_group(q_c, k_c, v_c, q_l, k_l, v_l, cos, sin, q_norm_w, k_norm_w, ctx_out):
    B_, S, _ = q_l.shape
    C = q_c.shape[1]
    heads = lambda t, n: t.reshape(t.shape[0], t.shape[1], n, HEAD_DIM)
    qc = rms_norm(heads(q_c, N_HEADS), q_norm_w)
    kc = rms_norm(heads(k_c, N_KV_HEADS), k_norm_w)
    vc = heads(v_c, N_KV_HEADS)
    ql = rope_2d(rms_norm(heads(q_l, N_HEADS), q_norm_w), cos, sin)
    kl = rope_2d(rms_norm(heads(k_l, N_KV_HEADS), k_norm_w), cos, sin)
    vl = heads(v_l, N_KV_HEADS)
    k_all = jnp.concatenate([kc, kl], axis=1)
    v_all = jnp.concatenate([vc, vl], axis=1)
    nblk = S // Q_BLOCK
    q_blocks = ql.reshape(B_, nblk, Q_BLOCK, N_KV_HEADS, KV_GROUP, HEAD_DIM).transpose(1, 0, 2, 3, 4, 5)
    o_l = lax.map(lambda qb: gqa_softmax(qb, k_all, v_all), q_blocks)
    o_l = o_l.transpose(1, 0, 2, 3, 4, 5).reshape(B_, S, D_ATTN)
    o_c = None
    if ctx_out:
        o_c = gqa_softmax(qc.reshape(B_, C, N_KV_HEADS, KV_GROUP, HEAD_DIM), kc, vc).reshape(B_, C, D_ATTN)
    return o_c, o_l


def rglru_coeffs(x, wa, ba, wx, bx, lam):
    B_, T, _ = x.shape
    xb = x.reshape(B_, T, LRU_BLOCKS, LRU_BLOCK_W)
    r = jax.nn.sigmoid(jnp.einsum('btnc,ncd->btnd', xb, wa).reshape(B_, T, D_LRU) + ba)
    i = jax.nn.sigmoid(jnp.einsum('btnc,ncd->btnd', xb, wx).reshape(B_, T, D_LRU) + bx)
    log_a = -LRU_C * r * jax.nn.softplus(-lam)
    a = jnp.exp(log_a)
    b = jnp.sqrt(-jnp.expm1(2.0 * log_a)) * (i * x)
    return a, b


def _scan_combine(e1, e2):
    a1, b1 = e1
    a2, b2 = e2
    return a1 * a2, a2 * b1 + b2


def linear_scan(a, b, h0, reverse):
    if reverse:
        a, b = jnp.flip(a, 1), jnp.flip(b, 1)
    b = b.at[:, 0].add(a[:, 0] * h0)
    _, h = lax.associative_scan(_scan_combine, (a, b), axis=1)
    final = h[:, -1]
    if reverse:
        h = jnp.flip(h, 1)
    return h, final


def lru_group(x_c, g_c, x_l, g_l, conv_w, conv_b, wa, ba, wx, bx, lam, ctx_out):
    f32 = jnp.float32
    xc = dwconv_centred(x_c, conv_w, conv_b).astype(f32)
    xl = dwconv_centred(x_l, conv_w, conv_b).astype(f32)
    h_c_sum, h_l_sum = 0.0, 0.0
    for d, rev in ((0, False), (1, True)):
        wa_d, ba_d = wa[d].astype(f32), ba[d].astype(f32)
        wx_d, bx_d = wx[d].astype(f32), bx[d].astype(f32)
        lam_d = lam[d].astype(f32)
        a_c, b_c = rglru_coeffs(xc, wa_d, ba_d, wx_d, bx_d, lam_d)
        a_l, b_l = rglru_coeffs(xl, wa_d, ba_d, wx_d, bx_d, lam_d)
        h_c, fin_c = linear_scan(a_c, b_c, jnp.zeros_like(b_c[:, 0]), rev)
        h_l, _ = linear_scan(a_l, b_l, fin_c, rev)
        h_c_sum = h_c_sum + h_c
        h_l_sum = h_l_sum + h_l
    o_l = h_l_sum.astype(x_l.dtype) * jax.nn.gelu(g_l)
    o_c = h_c_sum.astype(x_c.dtype) * jax.nn.gelu(g_c) if ctx_out else None
    return o_c, o_l


def token_mixer(h_ctx, h_lat, cos, sin, w_in, q_norm_w, k_norm_w, lru_conv_w, lru_conv_b,
                lru_wa, lru_ba, lru_wx, lru_bx, lru_lambda, attn_out_norm_w, lru_out_norm_w,
                w_out, ctx_out):
    q_c, k_c, v_c, x_c, g_c = jnp.split(h_ctx @ w_in, IN_SPLITS, axis=-1)
    q_l, k_l, v_l, x_l, g_l = jnp.split(h_lat @ w_in, IN_SPLITS, axis=-1)
    a_c, a_l = attention_group(q_c, k_c, v_c, q_l, k_l, v_l, cos, sin, q_norm_w, k_norm_w, ctx_out)
    r_c, r_l = lru_group(x_c, g_c, x_l, g_l, lru_conv_w, lru_conv_b, lru_wa, lru_ba, lru_wx, lru_bx,
                         lru_lambda, ctx_out)
    y_l = jnp.concatenate([rms_norm(a_l, attn_out_norm_w), rms_norm(r_l, lru_out_norm_w)], axis=-1) @ w_out
    y_c = None
    if ctx_out:
        y_c = jnp.concatenate([rms_norm(a_c, attn_out_norm_w), rms_norm(r_c, lru_out_norm_w)], axis=-1) @ w_out
    return y_c, y_l


def conv_ffn(h, w_up, conv_w, conv_b, w_down):
    u = dwconv_centred(h @ w_up, conv_w, conv_b)
    g, v = jnp.split(u, 2, axis=-1)
    return (jax.nn.silu(g) * v) @ w_down


def setup_inputs(seed: int = 0) -> dict:
    key = jax.random.key(seed)
    ks = jax.random.split(key, 26)
    f32 = jnp.float32

    def nrm(k, shape, scale):
        return jax.random.normal(k, shape, f32) * scale

    a_init = jax.random.uniform(ks[15], (DEPTH, 2, D_LRU), f32, 0.9, 0.999) ** (1.0 / LRU_C)
    return {
        "x": nrm(ks[0], (BATCH, SEQ, D_MODEL), 1.0),
        "c": nrm(ks[1], (BATCH, D_MODEL), 1.0),
        "ctx": nrm(ks[2], (BATCH, CTX_LEN, D_MODEL), 1.0),
        "c_ctx": nrm(ks[3], (D_MODEL,), 1.0),
        "w_ada": nrm(ks[4], (DEPTH, D_MODEL, 6 * D_MODEL), 0.5 * D_MODEL ** -0.5),
        "b_ada": nrm(ks[5], (DEPTH, 6 * D_MODEL), 0.01),
        "w_in": nrm(ks[6], (DEPTH, D_MODEL, D_IN), D_MODEL ** -0.5),
        "q_norm_w": 1.0 + nrm(ks[7], (DEPTH, HEAD_DIM), 0.05),
        "k_norm_w": 1.0 + nrm(ks[8], (DEPTH, HEAD_DIM), 0.05),
        "lru_conv_w": nrm(ks[9], (DEPTH, LRU_CONV_W, D_LRU), LRU_CONV_W ** -0.5),
        "lru_conv_b": nrm(ks[10], (DEPTH, D_LRU), 0.01),
        "lru_wa": nrm(ks[11], (DEPTH, 2, LRU_BLOCKS, LRU_BLOCK_W, LRU_BLOCK_W), LRU_BLOCK_W ** -0.5),
        "lru_ba": nrm(ks[12], (DEPTH, 2, D_LRU), 0.01),
        "lru_wx": nrm(ks[13], (DEPTH, 2, LRU_BLOCKS, LRU_BLOCK_W, LRU_BLOCK_W), LRU_BLOCK_W ** -0.5),
        "lru_bx": nrm(ks[14], (DEPTH, 2, D_LRU), 0.01),
        "lru_lambda": jnp.log(a_init) - jnp.log1p(-a_init),
        "attn_out_norm_w": 1.0 + nrm(ks[16], (DEPTH, D_ATTN), 0.05),
        "lru_out_norm_w": 1.0 + nrm(ks[17], (DEPTH, D_LRU), 0.05),
        "w_out": nrm(ks[18], (DEPTH, D_MIX, D_MODEL), D_MIX ** -0.5),
        "ffn_w_up": nrm(ks[19], (DEPTH, D_MODEL, 2 * D_FF), D_MODEL ** -0.5),
        "ffn_conv_w": nrm(ks[20], (DEPTH, FFN_CONV_W, 2 * D_FF), FFN_CONV_W ** -0.5),
        "ffn_conv_b": nrm(ks[21], (DEPTH, 2 * D_FF), 0.01),
        "ffn_w_down": nrm(ks[22], (DEPTH, D_FF, D_MODEL), D_FF ** -0.5),
        "final_norm_w": 1.0 + nrm(ks[23], (D_MODEL,), 0.05),
    }


def reference(x, c, ctx, c_ctx, w_ada, b_ada, w_in, q_norm_w, k_norm_w, lru_conv_w, lru_conv_b,
              lru_wa, lru_ba, lru_wx, lru_bx, lru_lambda, attn_out_norm_w, lru_out_norm_w, w_out,
              ffn_w_up, ffn_conv_w, ffn_conv_b, ffn_w_down, final_norm_w):
    S = x.shape[1]
    cos, sin = grid_rope_tables(S)
    x_lat, x_ctx = x, ctx
    for l in range(DEPTH):
        ctx_out = l < DEPTH - 1
        mod_lat = (jax.nn.silu(c) @ w_ada[l] + b_ada[l])[:, None, :]
        mod_ctx = (jax.nn.silu(c_ctx) @ w_ada[l] + b_ada[l])[None, None, :]
        shm_l, scm_l, gm_l, shf_l, scf_l, gf_l = jnp.split(mod_lat, 6, axis=-1)
        shm_c, scm_c, gm_c, shf_c, scf_c, gf_c = jnp.split(mod_ctx, 6, axis=-1)
        h_c = modulate(rms_norm(x_ctx), shm_c, scm_c)
        h_l = modulate(rms_norm(x_lat), shm_l, scm_l)
        y_c, y_l = token_mixer(h_c, h_l, cos, sin, w_in[l], q_norm_w[l], k_norm_w[l], lru_conv_w[l],
                               lru_conv_b[l], lru_wa[l], lru_ba[l], lru_wx[l], lru_bx[l], lru_lambda[l],
                               attn_out_norm_w[l], lru_out_norm_w[l], w_out[l], ctx_out)
        x_lat = x_lat + gm_l * y_l
        h_l = modulate(rms_norm(x_lat), shf_l, scf_l)
        x_lat = x_lat + gf_l * conv_ffn(h_l, ffn_w_up[l], ffn_conv_w[l], ffn_conv_b[l], ffn_w_down[l])
        if ctx_out:
            x_ctx = x_ctx + gm_c * y_c
            h_c = modulate(rms_norm(x_ctx), shf_c, scf_c)
            x_ctx = x_ctx + gf_c * conv_ffn(h_c, ffn_w_up[l], ffn_conv_w[l], ffn_conv_b[l], ffn_w_down[l])
    return rms_norm(x_lat, final_norm_w)
```

```python
import functools

import jax
import jax.numpy as jnp
from jax import lax
from jax.experimental import pallas as pl
from jax.experimental.pallas import tpu as pltpu

F32 = jnp.float32
BF16 = jnp.bfloat16

D_MODEL = 1024
D_ATTN = 512
D_LRU = 512
KV_W = 128
HEAD_DIM = 64
N_KV = 2
KV_GROUP = 4
QK_W = D_ATTN + KV_W
D_IN = D_ATTN + 2 * KV_W + 2 * D_LRU
D_FF = 2816
LRU_C = 8.0
GRID_W = 64
ROPE_THETA = 10000.0
EPS = 1e-6
SUBLANES = 8
LANES = 128
BF16_ROWS = 16
FF_CHUNK = 256
VMEM_LIMIT = 56 * 1024 * 1024


def _dot(a, b):
    return jnp.dot(a, b, preferred_element_type=F32)


def _sigmoid(z):
    return 0.5 * jnp.tanh(0.5 * z) + 0.5


def _rms(x):
    return x * lax.rsqrt(jnp.mean(x * x, axis=-1, keepdims=True) + EPS)


def _shift_down(g):
    rows = lax.broadcasted_iota(jnp.int32, g.shape, 0)
    return jnp.where(rows == 0, 0.0, pltpu.roll(g, 1, 0))


def _shift_up(g):
    rows = lax.broadcasted_iota(jnp.int32, g.shape, 0)
    return jnp.where(rows == SUBLANES - 1, 0.0, pltpu.roll(g, SUBLANES - 1, 0))


def _params(*sem):
    return pltpu.CompilerParams(dimension_semantics=sem, vmem_limit_bytes=VMEM_LIMIT)


def _resident(shape):
    nd = len(shape)
    return pl.BlockSpec(shape, lambda *_: (0,) * nd, pipeline_mode=pl.Buffered(1))


def _mod_kernel(c_ref, w_ref, b_ref, o_ref):
    c = c_ref[...]
    s = (c * _sigmoid(c)).astype(BF16)
    o_ref[...] = _dot(s, w_ref[...].astype(BF16)) + b_ref[...]


def _modulation(cvec, w_ada, b_ada, tn=1536):
    depth, d, n = w_ada.shape
    rows = cvec.shape[0]
    return pl.pallas_call(
        _mod_kernel,
        grid=(depth, n // tn),
        in_specs=[pl.BlockSpec((rows, d), lambda l, j: (0, 0)),
                  pl.BlockSpec((None, d, tn), lambda l, j: (l, 0, j)),
                  pl.BlockSpec((None, 1, tn), lambda l, j: (l, 0, j))],
        out_specs=pl.BlockSpec((None, rows, tn), lambda l, j: (l, 0, j)),
        out_shape=jax.ShapeDtypeStruct((depth, rows, n), F32),
        compiler_params=_params("arbitrary", "arbitrary"),
        name="adaln_mod",
    )(cvec, w_ada, b_ada.reshape(depth, 1, n))


def _inproj_kernel(x_ref, mod_ref, w_ref, bd_ref, nw_ref, cos_ref, sin_ref,
                   q_ref, kt_ref, v_ref, xl_ref, g_ref, *, use_rope):
    x = x_ref[...]
    shift = mod_ref[:, 0:D_MODEL]
    scale = mod_ref[:, D_MODEL:2 * D_MODEL]
    h = _rms(x) * (1.0 + scale) + shift
    y = _dot(h.astype(BF16), w_ref[...])
    qk = y[:, 0:QK_W]
    ss = _dot((qk * qk).astype(BF16), bd_ref[...])
    qk = qk * lax.rsqrt(ss * (1.0 / HEAD_DIM) + EPS) * nw_ref[...]
    if use_rope:
        lane = lax.broadcasted_iota(jnp.int32, (x.shape[0], LANES), 1)
        first_half = (lane % HEAD_DIM) < (HEAD_DIM // 2)
        cos = cos_ref[...]
        sin = sin_ref[...]
        cols = []
        for k in range(QK_W // LANES):
            xc = qk[:, LANES * k:LANES * (k + 1)]
            other = jnp.where(first_half, pltpu.roll(xc, LANES - HEAD_DIM // 2, 1),
                              pltpu.roll(xc, HEAD_DIM // 2, 1))
            cols.append(xc * cos + other * sin)
        qk = jnp.concatenate(cols, axis=1)
    q_ref[...] = (qk[:, 0:D_ATTN] * (HEAD_DIM ** -0.5)).astype(BF16)
    kt = qk[:, D_ATTN:QK_W].T
    for g in range(N_KV):
        kt_ref[g] = kt[HEAD_DIM * g:HEAD_DIM * (g + 1)].astype(BF16)
    v_ref[...] = y[:, QK_W:QK_W + KV_W].astype(BF16)
    xl_ref[...] = y[:, QK_W + KV_W:QK_W + KV_W + D_LRU]
    g_ref[...] = y[:, QK_W + KV_W + D_LRU:D_IN]


def _inproj(x, mod, w_in, bd, nw, cos, sin, *, n_seq, seq_len, shared_mod, use_rope):
    tm = min(512, seq_len)
    tps = seq_len // tm
    rows = n_seq * seq_len
    mod_idx = (lambda i: (0, 0, 0)) if shared_mod else (lambda i: (i // tps, 0, 0))
    kern = functools.partial(_inproj_kernel, use_rope=use_rope)
    return pl.pallas_call(
        kern,
        grid=(rows // tm,),
        in_specs=[pl.BlockSpec((tm, D_MODEL), lambda i: (i, 0)),
                  pl.BlockSpec((None, 1, 2 * D_MODEL), mod_idx),
                  _resident((D_MODEL, D_IN)),
                  _resident((QK_W, QK_W)),
                  _resident((1, QK_W)),
                  pl.BlockSpec((tm, LANES), lambda i: (i % tps, 0)),
                  pl.BlockSpec((tm, LANES), lambda i: (i % tps, 0))],
        out_specs=[pl.BlockSpec((tm, D_ATTN), lambda i: (i, 0)),
                   pl.BlockSpec((None, N_KV, HEAD_DIM, tm), lambda i: (i // tps, 0, 0, i % tps)),
                   pl.BlockSpec((tm, KV_W), lambda i: (i, 0)),
                   pl.BlockSpec((tm, D_LRU), lambda i: (i, 0)),
                   pl.BlockSpec((tm, D_LRU), lambda i: (i, 0))],
        out_shape=[jax.ShapeDtypeStruct((rows, D_ATTN), BF16),
                   jax.ShapeDtypeStruct((n_seq, N_KV, HEAD_DIM, seq_len), BF16),
                   jax.ShapeDtypeStruct((rows, KV_W), BF16),
                   jax.ShapeDtypeStruct((rows, D_LRU), F32),
                   jax.ShapeDtypeStruct((rows, D_LRU), F32)],
        compiler_params=_params("arbitrary"),
        name="in_proj_rope" if use_rope else "in_proj",
    )(x, mod, w_in, bd, nw, cos, sin)


def _attn_kernel(*refs, tq, tk, n_lat_chunks):
    if n_lat_chunks:
        q_ref, kc_ref, vc_ref, kl_ref, vl_ref, o_ref = refs
    else:
        q_ref, kc_ref, vc_ref, o_ref = refs
    for g in range(N_KV):
        heads = [KV_GROUP * g + h for h in range(KV_GROUP)]
        qs = jnp.concatenate([q_ref[:, HEAD_DIM * h:HEAD_DIM * (h + 1)] for h in heads], axis=0)
        s = _dot(qs, kc_ref[g])
        m = jnp.max(s, axis=1, keepdims=True)
        p = jnp.exp(s - m)
        l = jnp.sum(p, axis=1, keepdims=True)
        acc = _dot(p.astype(BF16), vc_ref[...])
        if n_lat_chunks:
            def body(i, carry, g=g, qs=qs):
                m, l, acc = carry
                off = pl.multiple_of(i * tk, tk)
                s = _dot(qs, kl_ref[g, :, pl.ds(off, tk)])
                m_new = jnp.maximum(m, jnp.max(s, axis=1, keepdims=True))
                alpha = jnp.exp(m - m_new)
                p = jnp.exp(s - m_new)
                l = alpha * l + jnp.sum(p, axis=1, keepdims=True)
                acc = alpha * acc + _dot(p.astype(BF16), vl_ref[pl.ds(off, tk), :])
                return m_new, l, acc
            m, l, acc = lax.fori_loop(0, n_lat_chunks, body, (m, l, acc))
        o = acc * (1.0 / l)
        for k, h in enumerate(heads):
            o_ref[:, HEAD_DIM * h:HEAD_DIM * (h + 1)] = (
                o[k * tq:(k + 1) * tq, HEAD_DIM * g:HEAD_DIM * (g + 1)])


def _attention(q, kt_ctx, v_ctx, kt_lat, v_lat, *, n_seq, q_len, ctx_len, lat_len):
    tq = min(256, q_len)
    tk = min(512, lat_len) if lat_len else 0
    n_lat_chunks = lat_len // tk if lat_len else 0
    nq = q_len // tq
    kern = functools.partial(_attn_kernel, tq=tq, tk=tk, n_lat_chunks=n_lat_chunks)
    in_specs = [pl.BlockSpec((tq, D_ATTN), lambda b, i: (b * nq + i, 0)),
                pl.BlockSpec((None, N_KV, HEAD_DIM, ctx_len), lambda b, i: (b, 0, 0, 0)),
                pl.BlockSpec((ctx_len, KV_W), lambda b, i: (b, 0))]
    args = [q, kt_ctx, v_ctx]
    if lat_len:
        in_specs += [pl.BlockSpec((None, N_KV, HEAD_DIM, lat_len), lambda b, i: (b, 0, 0, 0)),
                     pl.BlockSpec((lat_len, KV_W), lambda b, i: (b, 0))]
        args += [kt_lat, v_lat]
    return pl.pallas_call(
        kern,
        grid=(n_seq, nq),
        in_specs=in_specs,
        out_specs=pl.BlockSpec((tq, D_ATTN), lambda b, i: (b * nq + i, 0)),
        out_shape=jax.ShapeDtypeStruct((n_seq * q_len, D_ATTN), F32),
        compiler_params=_params("arbitrary", "arbitrary"),
        name="attn_latent" if lat_len else "attn_context",
    )(*args)


def _lru_sequence(x_ref, g_ref, o_ref, h0, consts, scr, *, n_groups):
    cw_ref, cb_ref, wg_ref, bg_ref, nsp = consts
    xe_ref, af_ref, bf_ref, ar_ref, br_ref = scr
    rows = SUBLANES * n_groups
    width = x_ref.shape[1]
    chunk = min(256, rows)
    n_chunks = rows // chunk

    xe_ref[pl.ds(2 * SUBLANES, rows), :] = x_ref[...]
    xe_ref[pl.ds(0, SUBLANES), :] = _shift_down(x_ref[pl.ds(rows - 2 * SUBLANES, SUBLANES), :])
    xe_ref[pl.ds(SUBLANES, SUBLANES), :] = _shift_down(x_ref[pl.ds(rows - SUBLANES, SUBLANES), :])
    xe_ref[pl.ds(rows + 2 * SUBLANES, SUBLANES), :] = _shift_up(x_ref[pl.ds(0, SUBLANES), :])

    def gates(c, _):
        r0 = pl.multiple_of(c * chunk, chunk)
        xc = cb_ref[...]
        for k in range(4):
            xc = xc + cw_ref[k:k + 1, :] * xe_ref[pl.ds(r0 + SUBLANES * k, chunk), :]
        pre = _dot(xc.astype(BF16), wg_ref[...]) + bg_ref[...]
        for d, (a_ref, b_ref) in enumerate(((af_ref, bf_ref), (ar_ref, br_ref))):
            r_gate = _sigmoid(pre[:, (2 * d) * width:(2 * d + 1) * width])
            i_gate = _sigmoid(pre[:, (2 * d + 1) * width:(2 * d + 2) * width])
            log_a = nsp[d] * r_gate
            a = jnp.exp(log_a)
            b = jnp.sqrt(jnp.tanh(-log_a) * (1.0 + a * a)) * (i_gate * xc)
            a_ref[pl.ds(r0, chunk), :] = a
            b_ref[pl.ds(r0, chunk), :] = b
        return 0

    lax.fori_loop(0, n_chunks, gates, 0)

    def scan(j, carry):
        hf, pf, hr, pr = carry
        rf = pl.multiple_of(j * SUBLANES, SUBLANES)
        rr = pl.multiple_of((n_groups - 1 - j) * SUBLANES, SUBLANES)
        a = af_ref[pl.ds(rf, SUBLANES), :]
        hf = a * hf + bf_ref[pl.ds(rf, SUBLANES), :]
        pf = a * pf
        bf_ref[pl.ds(rf, SUBLANES), :] = hf
        af_ref[pl.ds(rf, SUBLANES), :] = pf
        a = ar_ref[pl.ds(rr, SUBLANES), :]
        hr = a * hr + br_ref[pl.ds(rr, SUBLANES), :]
        pr = a * pr
        br_ref[pl.ds(rr, SUBLANES), :] = hr
        ar_ref[pl.ds(rr, SUBLANES), :] = pr
        return hf, pf, hr, pr

    zero = jnp.zeros((SUBLANES, width), F32)
    one = jnp.ones((SUBLANES, width), F32)
    hf, pf, hr, pr = lax.fori_loop(0, n_groups, scan, (zero, one, zero, one), unroll=4)

    e = h0[0]
    init_f = []
    for s in range(SUBLANES):
        init_f.append(e)
        e = hf[s:s + 1, :] + pf[s:s + 1, :] * e
    fin_f = e
    e = h0[1]
    init_r = [None] * SUBLANES
    for s in range(SUBLANES - 1, -1, -1):
        init_r[s] = e
        e = hr[s:s + 1, :] + pr[s:s + 1, :] * e
    fin_r = e

    if o_ref is not None:
        init_f = jnp.concatenate(init_f, axis=0)[None]
        init_r = jnp.concatenate(init_r, axis=0)[None]

        def combine(c, _):
            r0 = pl.multiple_of(c * chunk, chunk)
            sl = pl.ds(r0, chunk)
            shp = (chunk // SUBLANES, SUBLANES, width)
            h = (bf_ref[sl, :].reshape(shp) + af_ref[sl, :].reshape(shp) * init_f
                 + br_ref[sl, :].reshape(shp) + ar_ref[sl, :].reshape(shp) * init_r)
            gt = g_ref[sl, :]
            gelu = 0.5 * gt * (1.0 + jnp.tanh(0.7978845608028654 * (gt + 0.044715 * gt * gt * gt)))
            o_ref[sl, :] = h.reshape(chunk, width) * gelu
            return 0

        lax.fori_loop(0, n_chunks, combine, 0)
    return fin_f, fin_r


def _lru_kernel(*refs, ctx_groups, lat_groups, ctx_out):
    xc_ref, gc_ref, xl_ref, gl_ref, cw_ref, cb_ref, wg_ref, bg_ref, lam_ref = refs[:9]
    n_out = 2 if ctx_out else 1
    outs = refs[9:9 + n_out]
    scr = refs[9 + n_out:]
    z = -lam_ref[...]
    softplus = jnp.maximum(z, 0.0) + jnp.log1p(jnp.exp(-jnp.abs(z)))
    nsp = [-LRU_C * softplus[d:d + 1, :] for d in range(2)]
    consts = (cw_ref, cb_ref, wg_ref, bg_ref, nsp)
    zero = jnp.zeros((1, xc_ref.shape[1]), F32)
    fin = _lru_sequence(xc_ref, gc_ref, outs[1] if ctx_out else None, (zero, zero), consts, scr,
                        n_groups=ctx_groups)
    _lru_sequence(xl_ref, gl_ref, outs[0], fin, consts, scr, n_groups=lat_groups)


def _lru(x_ctx, g_ctx, x_lat, g_lat, conv_w, conv_b, wg, bg, lam, *, n_seq, ctx_len, lat_len, ctx_out):
    w = LANES
    n_slab = D_LRU // w
    kern = functools.partial(_lru_kernel, ctx_groups=ctx_len // SUBLANES,
                             lat_groups=lat_len // SUBLANES, ctx_out=ctx_out)
    ctx_spec = pl.BlockSpec((ctx_len, w), lambda b, c: (b, c))
    lat_spec = pl.BlockSpec((lat_len, w), lambda b, c: (b, c))
    out_specs = [lat_spec]
    out_shape = [jax.ShapeDtypeStruct((n_seq * lat_len, D_LRU), F32)]
    if ctx_out:
        out_specs.append(ctx_spec)
        out_shape.append(jax.ShapeDtypeStruct((n_seq * ctx_len, D_LRU), F32))
    max_rows = max(ctx_len, lat_len)
    return pl.pallas_call(
        kern,
        grid=(n_seq, n_slab),
        in_specs=[ctx_spec, ctx_spec, lat_spec, lat_spec,
                  pl.BlockSpec((4, w), lambda b, c: (0, c)),
                  pl.BlockSpec((1, w), lambda b, c: (0, c)),
                  pl.BlockSpec((None, w, 4 * w), lambda b, c: (c, 0, 0)),
                  pl.BlockSpec((None, 1, 4 * w), lambda b, c: (c, 0, 0)),
                  pl.BlockSpec((2, w), lambda b, c: (0, c))],
        out_specs=out_specs,
        out_shape=out_shape,
        scratch_shapes=[pltpu.VMEM((max_rows + 3 * SUBLANES, w), F32)]
                       + [pltpu.VMEM((max_rows, w), F32)] * 4,
        compiler_params=_params("arbitrary", "arbitrary"),
        name="rglru_bidir",
    )(x_ctx, g_ctx, x_lat, g_lat, conv_w, conv_b, wg, bg, lam)


def _outproj_kernel(a_ref, r_ref, x_ref, gate_ref, nw_ref, w_ref, o_ref):
    cat = jnp.concatenate([_rms(a_ref[...]), _rms(r_ref[...])], axis=1) * nw_ref[...]
    y = _dot(cat.astype(BF16), w_ref[...])
    o_ref[...] = x_ref[...] + gate_ref[...] * y


def _outproj(a, r, x, mod, nw, w_out, *, n_seq, seq_len, shared_mod):
    tm = min(512, seq_len)
    tps = seq_len // tm
    rows = n_seq * seq_len
    gate_idx = (lambda i: (0, 0, 2)) if shared_mod else (lambda i: (i // tps, 0, 2))
    return pl.pallas_call(
        _outproj_kernel,
        grid=(rows // tm,),
        in_specs=[pl.BlockSpec((tm, D_ATTN), lambda i: (i, 0)),
                  pl.BlockSpec((tm, D_LRU), lambda i: (i, 0)),
                  pl.BlockSpec((tm, D_MODEL), lambda i: (i, 0)),
                  pl.BlockSpec((None, 1, D_MODEL), gate_idx),
                  _resident((1, D_MODEL)),
                  _resident((D_MODEL, D_MODEL))],
        out_specs=pl.BlockSpec((tm, D_MODEL), lambda i: (i, 0)),
        out_shape=jax.ShapeDtypeStruct((rows, D_MODEL), F32),
        compiler_params=_params("arbitrary"),
        name="out_proj",
    )(a, r, x, mod, nw, w_out)


def _ffn_kernel(*refs, tm, tps, final_norm):
    x_ref, xp_ref, xn_ref, mod_ref, wup_ref, cw_ref, cb_ref, wdn_ref = refs[:8]
    if final_norm:
        fnw_ref, o_ref, hext_ref, act_ref = refs[8:]
    else:
        o_ref, hext_ref, act_ref = refs[8:]
    jt = pl.program_id(0) % tps
    shift = mod_ref[:, 0:D_MODEL]
    scale = mod_ref[:, D_MODEL:2 * D_MODEL]
    gate = mod_ref[:, 2 * D_MODEL:3 * D_MODEL]

    def norm_mod(x):
        return _rms(x) * (1.0 + scale) + shift

    x = x_ref[...]
    hext_ref[pl.ds(BF16_ROWS, tm), :] = norm_mod(x).astype(BF16)
    hp = norm_mod(xp_ref[...])
    hp = jnp.where(jt == 0, _shift_down(hp), hp)
    hn = norm_mod(xn_ref[...])
    hn = jnp.where(jt == tps - 1, _shift_up(hn), hn)
    zeros = jnp.zeros_like(hp)
    hext_ref[pl.ds(0, BF16_ROWS), :] = jnp.concatenate([zeros, hp], axis=0).astype(BF16)
    hext_ref[pl.ds(BF16_ROWS + tm, BF16_ROWS), :] = jnp.concatenate([hn, zeros], axis=0).astype(BF16)

    def conv(u, off):
        sl = pl.ds(off, FF_CHUNK)
        return (cw_ref[0:1, sl] * u[SUBLANES:SUBLANES + tm]
                + cw_ref[1:2, sl] * u[2 * SUBLANES:2 * SUBLANES + tm]
                + cw_ref[2:3, sl] * u[3 * SUBLANES:3 * SUBLANES + tm]
                + cb_ref[:, sl])

    def chunk(c, _):
        og = pl.multiple_of(c * FF_CHUNK, FF_CHUNK)
        ov = pl.multiple_of(D_FF + c * FF_CHUNK, FF_CHUNK)
        h = hext_ref[...]
        yg = conv(_dot(h, wup_ref[:, pl.ds(og, FF_CHUNK)]), og)
        yv = conv(_dot(h, wup_ref[:, pl.ds(ov, FF_CHUNK)]), ov)
        act_ref[:, pl.ds(og, FF_CHUNK)] = (yg * _sigmoid(yg) * yv).astype(BF16)
        return 0

    lax.fori_loop(0, D_FF // FF_CHUNK, chunk, 0)
    out = x + gate * _dot(act_ref[...], wdn_ref[...])
    if final_norm:
        out = _rms(out) * fnw_ref[...]
    o_ref[...] = out


def _ffn(x, mod, w_up, conv_w, conv_b, w_down, final_w, *, n_seq, seq_len, shared_mod):
    tm = min(512, seq_len)
    tps = seq_len // tm
    rows = n_seq * seq_len
    gps = seq_len // SUBLANES
    gpt = tm // SUBLANES
    mod_idx = (lambda i: (0, 0, 1)) if shared_mod else (lambda i: (i // tps, 0, 1))

    def prev_idx(i):
        jt = i % tps
        return ((i // tps) * gps + jnp.where(jt == 0, gps - 1, jt * gpt - 1), 0)

    def next_idx(i):
        jt = i % tps
        return ((i // tps) * gps + jnp.where(jt == tps - 1, 0, (jt + 1) * gpt), 0)

    final_norm = final_w is not None
    in_specs = [pl.BlockSpec((tm, D_MODEL), lambda i: (i, 0)),
                pl.BlockSpec((SUBLANES, D_MODEL), prev_idx),
                pl.BlockSpec((SUBLANES, D_MODEL), next_idx),
                pl.BlockSpec((None, 1, 3 * D_MODEL), mod_idx),
                _resident((D_MODEL, 2 * D_FF)),
                _resident((3, 2 * D_FF)),
                _resident((1, 2 * D_FF)),
                _resident((D_FF, D_MODEL))]
    args = [x, x, x, mod, w_up, conv_w, conv_b, w_down]
    if final_norm:
        in_specs.append(_resident((1, D_MODEL)))
        args.append(final_w)
    kern = functools.partial(_ffn_kernel, tm=tm, tps=tps, final_norm=final_norm)
    return pl.pallas_call(
        kern,
        grid=(rows // tm,),
        in_specs=in_specs,
        out_specs=pl.BlockSpec((tm, D_MODEL), lambda i: (i, 0)),
        out_shape=jax.ShapeDtypeStruct((rows, D_MODEL), F32),
        scratch_shapes=[pltpu.VMEM((tm + 2 * BF16_ROWS, D_MODEL), BF16),
                        pltpu.VMEM((tm, D_FF), BF16)],
        compiler_params=_params("arbitrary"),
        name="conv_ffn_final" if final_norm else "conv_ffn",
    )(*args)


def _to_segments(x):
    b, t, c = x.shape
    return x.reshape(b, SUBLANES, t // SUBLANES, c).transpose(0, 2, 1, 3).reshape(b * t, c)


def _from_segments(y, b, t):
    return y.reshape(b, t // SUBLANES, SUBLANES, -1).transpose(0, 2, 1, 3).reshape(b, t, -1)


def _rope_tables(seq_len):
    r = jnp.arange(seq_len)
    t = (r % SUBLANES) * (seq_len // SUBLANES) + r // SUBLANES
    row = (t // GRID_W).astype(F32)
    col = (t % GRID_W).astype(F32)
    pairs = HEAD_DIM // 4
    inv = ROPE_THETA ** (-jnp.arange(pairs, dtype=F32) / pairs)
    ang = jnp.concatenate([row[:, None] * inv, col[:, None] * inv], axis=-1)
    cos, sin = jnp.cos(ang), jnp.sin(ang)
    return (jnp.concatenate([cos, cos, cos, cos], axis=-1),
            jnp.concatenate([-sin, sin, -sin, sin], axis=-1))


def _gate_weights(wa, wx, ba, bx):
    n_slab = D_LRU // LANES
    per = LANES // wa.shape[-1]

    def slab_diag(w):
        w = w.reshape(n_slab, per, w.shape[-2], w.shape[-1])
        eye = jnp.eye(per, dtype=w.dtype)
        return jnp.einsum('spcd,pq->spcqd', w, eye).reshape(n_slab, LANES, LANES)

    wg = jnp.concatenate([slab_diag(wa[0]), slab_diag(wx[0]), slab_diag(wa[1]), slab_diag(wx[1])],
                         axis=-1).astype(BF16)
    bg = jnp.stack([ba[0], bx[0], ba[1], bx[1]], axis=0).reshape(4, n_slab, LANES)
    bg = bg.transpose(1, 0, 2).reshape(n_slab, 1, 4 * LANES)
    return wg, bg


def kernel(x, c, ctx, c_ctx, w_ada, b_ada, w_in, q_norm_w, k_norm_w, lru_conv_w, lru_conv_b, lru_wa,
           lru_ba, lru_wx, lru_bx, lru_lambda, attn_out_norm_w, lru_out_norm_w, w_out, ffn_w_up,
           ffn_conv_w, ffn_conv_b, ffn_w_down, final_norm_w):
    batch, seq, _ = x.shape
    ctx_len = ctx.shape[1]
    depth = w_in.shape[0]

    mod_rows = -(-(batch + 1) // SUBLANES) * SUBLANES
    cvec = jnp.zeros((mod_rows, D_MODEL), F32).at[:batch].set(c).at[batch].set(c_ctx)
    mods = _modulation(cvec, w_ada, b_ada)

    cos, sin = _rope_tables(seq)
    ones = jnp.ones((ctx_len, LANES), F32)
    bd = jnp.kron(jnp.eye(QK_W // HEAD_DIM, dtype=F32), jnp.ones((HEAD_DIM, HEAD_DIM), F32)).astype(BF16)

    x_lat = _to_segments(x)
    x_ctx = _to_segments(ctx)
    lat = dict(n_seq=batch, seq_len=seq, shared_mod=False)
    cx = dict(n_seq=batch, seq_len=ctx_len, shared_mod=True)

    for l in range(depth):
        ctx_out = l < depth - 1
        last = l == depth - 1
        mod_lat = mods[l, :batch].reshape(batch, 1, 6 * D_MODEL)
        mod_ctx = mods[l, batch:batch + 1].reshape(1, 1, 6 * D_MODEL)
        w_in_l = w_in[l].astype(BF16)
        nw_qk = jnp.concatenate([jnp.tile(q_norm_w[l], D_ATTN // HEAD_DIM),
                                 jnp.tile(k_norm_w[l], KV_W // HEAD_DIM)]).reshape(1, QK_W)
        q_l, kt_l, v_l, xr_l, g_l = _inproj(x_lat, mod_lat, w_in_l, bd, nw_qk, cos, sin,
                                            use_rope=True, **lat)
        q_c, kt_c, v_c, xr_c, g_c = _inproj(x_ctx, mod_ctx, w_in_l, bd, nw_qk, ones, ones,
                                            use_rope=False, **cx)

        a_l = _attention(q_l, kt_c, v_c, kt_l, v_l, n_seq=batch, q_len=seq, ctx_len=ctx_len,
                         lat_len=seq)
        wg, bg = _gate_weights(lru_wa[l], lru_wx[l], lru_ba[l], lru_bx[l])
        r = _lru(xr_c, g_c, xr_l, g_l, lru_conv_w[l], lru_conv_b[l].reshape(1, D_LRU), wg, bg,
                 lru_lambda[l], n_seq=batch, ctx_len=ctx_len, lat_len=seq, ctx_out=ctx_out)

        nw_out = jnp.concatenate([attn_out_norm_w[l], lru_out_norm_w[l]]).reshape(1, D_MODEL)
        w_out_l = w_out[l].astype(BF16)
        w_up_l = ffn_w_up[l].astype(BF16)
        w_dn_l = ffn_w_down[l].astype(BF16)
        cb_l = ffn_conv_b[l].reshape(1, 2 * D_FF)
        x_lat = _outproj(a_l, r[0], x_lat, mod_lat, nw_out, w_out_l, **lat)
        x_lat = _ffn(x_lat, mod_lat, w_up_l, ffn_conv_w[l], cb_l, w_dn_l,
                     final_norm_w.reshape(1, D_MODEL) if last else None, **lat)
        if ctx_out:
            a_c = _attention(q_c, kt_c, v_c, None, None, n_seq=batch, q_len=ctx_len,
                             ctx_len=ctx_len, lat_len=0)
            x_ctx = _outproj(a_c, r[1], x_ctx, mod_ctx, nw_out, w_out_l, **cx)
            x_ctx = _ffn(x_ctx, mod_ctx, w_up_l, ffn_conv_w[l], cb_l, w_dn_l, None, **cx)
    return _from_segments(x_lat, batch, seq)
```

```python
import functools

import jax
import jax.numpy as jnp
from jax import lax
from jax.experimental import pallas as pl
from jax.experimental.pallas import tpu as pltpu

F32 = jnp.float32
BF16 = jnp.bfloat16

D_MODEL = 1024
D_ATTN = 512
D_LRU = 512
KV_W = 128
HEAD_DIM = 64
N_HEADS = 8
N_KV = 2
KV_GROUP = 4
LOG2E = 1.4426950408889634
QK_W = D_ATTN + KV_W
D_IN = D_ATTN + 2 * KV_W + 2 * D_LRU
D_FF = 2816
LRU_C = 8.0
GRID_W = 64
ROPE_THETA = 10000.0
EPS = 1e-6
SUBLANES = 8
LANES = 128
BF16_ROWS = 16
V_ROWS = HEAD_DIM + BF16_ROWS
FF_CHUNK = 256
VMEM_LIMIT = 56 * 1024 * 1024


def _dot(a, b):
    return jnp.dot(a, b, preferred_element_type=F32)


def _sigmoid(z):
    return 0.5 * jnp.tanh(0.5 * z) + 0.5


def _rms(x):
    return x * lax.rsqrt(jnp.mean(x * x, axis=-1, keepdims=True) + EPS)


def _shift_down(g):
    rows = lax.broadcasted_iota(jnp.int32, g.shape, 0)
    return jnp.where(rows == 0, 0.0, pltpu.roll(g, 1, 0))


def _shift_up(g):
    rows = lax.broadcasted_iota(jnp.int32, g.shape, 0)
    return jnp.where(rows == SUBLANES - 1, 0.0, pltpu.roll(g, SUBLANES - 1, 0))


def _params(*sem):
    return pltpu.CompilerParams(dimension_semantics=sem, vmem_limit_bytes=VMEM_LIMIT)


def _resident(shape):
    nd = len(shape)
    return pl.BlockSpec(shape, lambda *_: (0,) * nd, pipeline_mode=pl.Buffered(1))


def _mod_kernel(c_ref, w_ref, b_ref, o_ref):
    c = c_ref[...]
    s = (c * _sigmoid(c)).astype(BF16)
    o_ref[...] = _dot(s, w_ref[...].astype(BF16)) + b_ref[...]


def _modulation(cvec, w_ada, b_ada, tn=1536):
    depth, d, n = w_ada.shape
    rows = cvec.shape[0]
    return pl.pallas_call(
        _mod_kernel,
        grid=(depth, n // tn),
        in_specs=[pl.BlockSpec((rows, d), lambda l, j: (0, 0)),
                  pl.BlockSpec((None, d, tn), lambda l, j: (l, 0, j)),
                  pl.BlockSpec((None, 1, tn), lambda l, j: (l, 0, j))],
        out_specs=pl.BlockSpec((None, rows, tn), lambda l, j: (l, 0, j)),
        out_shape=jax.ShapeDtypeStruct((depth, rows, n), F32),
        compiler_params=_params("arbitrary", "arbitrary"),
        name="adaln_mod",
    )(cvec, w_ada, b_ada.reshape(depth, 1, n))


def _inproj_kernel(x_ref, mod_ref, w_ref, bd_ref, nw_ref, cos_ref, sin_ref,
                   qt_ref, k_ref, vt_ref, xl_ref, g_ref, *, use_rope):
    x = x_ref[...]
    shift = mod_ref[:, 0:D_MODEL]
    scale = mod_ref[:, D_MODEL:2 * D_MODEL]
    h = _rms(x) * (1.0 + scale) + shift
    y = _dot(h.astype(BF16), w_ref[...])
    qk = y[:, 0:QK_W]
    ss = _dot((qk * qk).astype(BF16), bd_ref[...])
    qk = qk * lax.rsqrt(ss * (1.0 / HEAD_DIM) + EPS) * nw_ref[...]
    if use_rope:
        lane = lax.broadcasted_iota(jnp.int32, (x.shape[0], LANES), 1)
        first_half = (lane % HEAD_DIM) < (HEAD_DIM // 2)
        cos = cos_ref[...]
        sin = sin_ref[...]
        cols = []
        for k in range(QK_W // LANES):
            xc = qk[:, LANES * k:LANES * (k + 1)]
            other = jnp.where(first_half, pltpu.roll(xc, LANES - HEAD_DIM // 2, 1),
                              pltpu.roll(xc, HEAD_DIM // 2, 1))
            cols.append(xc * cos + other * sin)
        qk = jnp.concatenate(cols, axis=1)
    qt = (qk[:, 0:D_ATTN] * (LOG2E * HEAD_DIM ** -0.5)).T
    for h in range(N_HEADS):
        qt_ref[h] = qt[HEAD_DIM * h:HEAD_DIM * (h + 1)].astype(BF16)
    for g in range(N_KV):
        k_ref[g] = qk[:, D_ATTN + HEAD_DIM * g:D_ATTN + HEAD_DIM * (g + 1)].astype(BF16)
    vt = y[:, QK_W:QK_W + KV_W].T.astype(BF16)
    ones = jnp.ones((V_ROWS - HEAD_DIM, x.shape[0]), BF16)
    for g in range(N_KV):
        vt_ref[g] = jnp.concatenate([vt[HEAD_DIM * g:HEAD_DIM * (g + 1)], ones], axis=0)
    xl_ref[...] = y[:, QK_W + KV_W:QK_W + KV_W + D_LRU]
    g_ref[...] = y[:, QK_W + KV_W + D_LRU:D_IN]


def _inproj(x, mod, w_in, bd, nw, cos, sin, *, n_seq, seq_len, shared_mod, use_rope):
    tm = min(512, seq_len)
    tps = seq_len // tm
    rows = n_seq * seq_len
    mod_idx = (lambda i: (0, 0, 0)) if shared_mod else (lambda i: (i // tps, 0, 0))
    kern = functools.partial(_inproj_kernel, use_rope=use_rope)
    return pl.pallas_call(
        kern,
        grid=(rows // tm,),
        in_specs=[pl.BlockSpec((tm, D_MODEL), lambda i: (i, 0)),
                  pl.BlockSpec((None, 1, 2 * D_MODEL), mod_idx),
                  _resident((D_MODEL, D_IN)),
                  _resident((QK_W, QK_W)),
                  _resident((1, QK_W)),
                  pl.BlockSpec((tm, LANES), lambda i: (i % tps, 0)),
                  pl.BlockSpec((tm, LANES), lambda i: (i % tps, 0))],
        out_specs=[pl.BlockSpec((None, N_HEADS, HEAD_DIM, tm), lambda i: (i // tps, 0, 0, i % tps)),
                   pl.BlockSpec((None, N_KV, tm, HEAD_DIM), lambda i: (i // tps, 0, i % tps, 0)),
                   pl.BlockSpec((None, N_KV, V_ROWS, tm), lambda i: (i // tps, 0, 0, i % tps)),
                   pl.BlockSpec((tm, D_LRU), lambda i: (i, 0)),
                   pl.BlockSpec((tm, D_LRU), lambda i: (i, 0))],
        out_shape=[jax.ShapeDtypeStruct((n_seq, N_HEADS, HEAD_DIM, seq_len), BF16),
                   jax.ShapeDtypeStruct((n_seq, N_KV, seq_len, HEAD_DIM), BF16),
                   jax.ShapeDtypeStruct((n_seq, N_KV, V_ROWS, seq_len), BF16),
                   jax.ShapeDtypeStruct((rows, D_LRU), F32),
                   jax.ShapeDtypeStruct((rows, D_LRU), F32)],
        compiler_params=_params("arbitrary"),
        name="in_proj_rope" if use_rope else "in_proj",
    )(x, mod, w_in, bd, nw, cos, sin)


def _attn_kernel(*refs, tq, tk, n_lat_chunks):
    if n_lat_chunks:
        qt_ref, kc_ref, vtc_ref, kl_ref, vtl_ref, o_ref, sa_ref, sb_ref = refs
    else:
        qt_ref, kc_ref, vtc_ref, o_ref = refs
    for g in range(N_KV):
        heads = [KV_GROUP * g + h for h in range(KV_GROUP)]
        qt = jnp.concatenate([qt_ref[h] for h in heads], axis=1)

        def scores(c, g=g, qt=qt):
            off = pl.multiple_of(c * tk, tk)
            return _dot(kl_ref[g, pl.ds(off, tk), :], qt)

        def update(c, s_ref, carry, g=g):
            m, acc = carry
            off = pl.multiple_of(c * tk, tk)
            m_new = jnp.maximum(m, jnp.max(s_ref[...], axis=0, keepdims=True))
            p = jnp.exp2(s_ref[...] - m_new).astype(BF16)
            return m_new, jnp.exp2(m - m_new) * acc + _dot(vtl_ref[g, :, pl.ds(off, tk)], p)

        s = _dot(kc_ref[g], qt)
        m = jnp.max(s, axis=0, keepdims=True)
        carry = (m, _dot(vtc_ref[g], jnp.exp2(s - m).astype(BF16)))
        if n_lat_chunks:
            sa_ref[...] = scores(0)

            def pair(j, carry):
                sb_ref[...] = scores(2 * j + 1)
                carry = update(2 * j, sa_ref, carry)
                sa_ref[...] = scores(2 * j + 2)
                return update(2 * j + 1, sb_ref, carry)

            carry = lax.fori_loop(0, n_lat_chunks // 2 - 1, pair, carry)
            sb_ref[...] = scores(n_lat_chunks - 1)
            carry = update(n_lat_chunks - 2, sa_ref, carry)
            carry = update(n_lat_chunks - 1, sb_ref, carry)
        acc = carry[1]
        ot = acc[0:HEAD_DIM] * (1.0 / acc[HEAD_DIM:HEAD_DIM + 1])
        for k, h in enumerate(heads):
            o_ref[:, HEAD_DIM * h:HEAD_DIM * (h + 1)] = ot[:, k * tq:(k + 1) * tq].T


def _attention(qt, k_ctx, vt_ctx, k_lat, vt_lat, *, n_seq, q_len, ctx_len, lat_len):
    tq = min(256, q_len)
    tk = min(512, lat_len // 2) if lat_len else 0
    n_lat_chunks = lat_len // tk if lat_len else 0
    assert n_lat_chunks % 2 == 0
    nq = q_len // tq
    kern = functools.partial(_attn_kernel, tq=tq, tk=tk, n_lat_chunks=n_lat_chunks)
    in_specs = [pl.BlockSpec((None, N_HEADS, HEAD_DIM, tq), lambda b, i: (b, 0, 0, i)),
                pl.BlockSpec((None, N_KV, ctx_len, HEAD_DIM), lambda b, i: (b, 0, 0, 0)),
                pl.BlockSpec((None, N_KV, V_ROWS, ctx_len), lambda b, i: (b, 0, 0, 0))]
    args = [qt, k_ctx, vt_ctx]
    scratch = []
    if lat_len:
        in_specs += [pl.BlockSpec((None, N_KV, lat_len, HEAD_DIM), lambda b, i: (b, 0, 0, 0)),
                     pl.BlockSpec((None, N_KV, V_ROWS, lat_len), lambda b, i: (b, 0, 0, 0))]
        args += [k_lat, vt_lat]
        scratch = [pltpu.VMEM((tk, KV_GROUP * tq), F32)] * 2
    return pl.pallas_call(
        kern,
        grid=(n_seq, nq),
        in_specs=in_specs,
        out_specs=pl.BlockSpec((tq, D_ATTN), lambda b, i: (b * nq + i, 0)),
        out_shape=jax.ShapeDtypeStruct((n_seq * q_len, D_ATTN), F32),
        scratch_shapes=scratch,
        compiler_params=_params("arbitrary", "arbitrary"),
        name="attn_latent" if lat_len else "attn_context",
    )(*args)


def _lru_sequence(x_ref, g_ref, o_ref, h0, consts, scr, *, n_groups):
    cw_ref, cb_ref, wg_ref, bg_ref, nsp = consts
    xe_ref, af_ref, bf_ref, ar_ref, br_ref = scr
    rows = SUBLANES * n_groups
    width = x_ref.shape[1]
    chunk = min(256, rows)
    n_chunks = rows // chunk

    xe_ref[pl.ds(2 * SUBLANES, rows), :] = x_ref[...]
    xe_ref[pl.ds(0, SUBLANES), :] = _shift_down(x_ref[pl.ds(rows - 2 * SUBLANES, SUBLANES), :])
    xe_ref[pl.ds(SUBLANES, SUBLANES), :] = _shift_down(x_ref[pl.ds(rows - SUBLANES, SUBLANES), :])
    xe_ref[pl.ds(rows + 2 * SUBLANES, SUBLANES), :] = _shift_up(x_ref[pl.ds(0, SUBLANES), :])

    def gates(c, _):
        r0 = pl.multiple_of(c * chunk, chunk)
        xc = cb_ref[...]
        for k in range(4):
            xc = xc + cw_ref[k:k + 1, :] * xe_ref[pl.ds(r0 + SUBLANES * k, chunk), :]
        pre = _dot(xc.astype(BF16), wg_ref[...]) + bg_ref[...]
        for d, (a_ref, b_ref) in enumerate(((af_ref, bf_ref), (ar_ref, br_ref))):
            r_gate = _sigmoid(pre[:, (2 * d) * width:(2 * d + 1) * width])
            i_gate = _sigmoid(pre[:, (2 * d + 1) * width:(2 * d + 2) * width])
            log_a = nsp[d] * r_gate
            a = jnp.exp(log_a)
            b = jnp.sqrt(jnp.tanh(-log_a) * (1.0 + a * a)) * (i_gate * xc)
            a_ref[pl.ds(r0, chunk), :] = a
            b_ref[pl.ds(r0, chunk), :] = b
        return 0

    lax.fori_loop(0, n_chunks, gates, 0)

    def scan(j, carry):
        hf, pf, hr, pr = carry
        rf = pl.multiple_of(j * SUBLANES, SUBLANES)
        rr = pl.multiple_of((n_groups - 1 - j) * SUBLANES, SUBLANES)
        a = af_ref[pl.ds(rf, SUBLANES), :]
        hf = a * hf + bf_ref[pl.ds(rf, SUBLANES), :]
        pf = a * pf
        bf_ref[pl.ds(rf, SUBLANES), :] = hf
        af_ref[pl.ds(rf, SUBLANES), :] = pf
        a = ar_ref[pl.ds(rr, SUBLANES), :]
        hr = a * hr + br_ref[pl.ds(rr, SUBLANES), :]
        pr = a * pr
        br_ref[pl.ds(rr, SUBLANES), :] = hr
        ar_ref[pl.ds(rr, SUBLANES), :] = pr
        return hf, pf, hr, pr

    zero = jnp.zeros((SUBLANES, width), F32)
    one = jnp.ones((SUBLANES, width), F32)
    hf, pf, hr, pr = lax.fori_loop(0, n_groups, scan, (zero, one, zero, one), unroll=4)

    e = h0[0]
    init_f = []
    for s in range(SUBLANES):
        init_f.append(e)
        e = hf[s:s + 1, :] + pf[s:s + 1, :] * e
    fin_f = e
    e = h0[1]
    init_r = [None] * SUBLANES
    for s in range(SUBLANES - 1, -1, -1):
        init_r[s] = e
        e = hr[s:s + 1, :] + pr[s:s + 1, :] * e
    fin_r = e

    if o_ref is not None:
        init_f = jnp.concatenate(init_f, axis=0)[None]
        init_r = jnp.concatenate(init_r, axis=0)[None]

        def combine(c, _):
            r0 = pl.multiple_of(c * chunk, chunk)
            sl = pl.ds(r0, chunk)
            shp = (chunk // SUBLANES, SUBLANES, width)
            h = (bf_ref[sl, :].reshape(shp) + af_ref[sl, :].reshape(shp) * init_f
                 + br_ref[sl, :].reshape(shp) + ar_ref[sl, :].reshape(shp) * init_r)
            gt = g_ref[sl, :]
            gelu = 0.5 * gt * (1.0 + jnp.tanh(0.7978845608028654 * (gt + 0.044715 * gt * gt * gt)))
            o_ref[sl, :] = h.reshape(chunk, width) * gelu
            return 0

        lax.fori_loop(0, n_chunks, combine, 0)
    return fin_f, fin_r


def _lru_kernel(*refs, ctx_groups, lat_groups, ctx_out):
    xc_ref, gc_ref, xl_ref, gl_ref, cw_ref, cb_ref, wg_ref, bg_ref, lam_ref = refs[:9]
    n_out = 2 if ctx_out else 1
    outs = refs[9:9 + n_out]
    scr = refs[9 + n_out:]
    z = -lam_ref[...]
    softplus = jnp.maximum(z, 0.0) + jnp.log1p(jnp.exp(-jnp.abs(z)))
    nsp = [-LRU_C * softplus[d:d + 1, :] for d in range(2)]
    consts = (cw_ref, cb_ref, wg_ref, bg_ref, nsp)
    zero = jnp.zeros((1, xc_ref.shape[1]), F32)
    fin = _lru_sequence(xc_ref, gc_ref, outs[1] if ctx_out else None, (zero, zero), consts, scr,
                        n_groups=ctx_groups)
    _lru_sequence(xl_ref, gl_ref, outs[0], fin, consts, scr, n_groups=lat_groups)


def _lru(x_ctx, g_ctx, x_lat, g_lat, conv_w, conv_b, wg, bg, lam, *, n_seq, ctx_len, lat_len, ctx_out):
    w = LANES
    n_slab = D_LRU // w
    kern = functools.partial(_lru_kernel, ctx_groups=ctx_len // SUBLANES,
                             lat_groups=lat_len // SUBLANES, ctx_out=ctx_out)
    ctx_spec = pl.BlockSpec((ctx_len, w), lambda b, c: (b, c))
    lat_spec = pl.BlockSpec((lat_len, w), lambda b, c: (b, c))
    out_specs = [lat_spec]
    out_shape = [jax.ShapeDtypeStruct((n_seq * lat_len, D_LRU), F32)]
    if ctx_out:
        out_specs.append(ctx_spec)
        out_shape.append(jax.ShapeDtypeStruct((n_seq * ctx_len, D_LRU), F32))
    max_rows = max(ctx_len, lat_len)
    return pl.pallas_call(
        kern,
        grid=(n_seq, n_slab),
        in_specs=[ctx_spec, ctx_spec, lat_spec, lat_spec,
                  pl.BlockSpec((4, w), lambda b, c: (0, c)),
                  pl.BlockSpec((1, w), lambda b, c: (0, c)),
                  pl.BlockSpec((None, w, 4 * w), lambda b, c: (c, 0, 0)),
                  pl.BlockSpec((None, 1, 4 * w), lambda b, c: (c, 0, 0)),
                  pl.BlockSpec((2, w), lambda b, c: (0, c))],
        out_specs=out_specs,
        out_shape=out_shape,
        scratch_shapes=[pltpu.VMEM((max_rows + 3 * SUBLANES, w), F32)]
                       + [pltpu.VMEM((max_rows, w), F32)] * 4,
        compiler_params=_params("arbitrary", "arbitrary"),
        name="rglru_bidir",
    )(x_ctx, g_ctx, x_lat, g_lat, conv_w, conv_b, wg, bg, lam)


def _outproj_kernel(a_ref, r_ref, x_ref, gate_ref, nw_ref, w_ref, o_ref):
    cat = jnp.concatenate([_rms(a_ref[...]), _rms(r_ref[...])], axis=1) * nw_ref[...]
    y = _dot(cat.astype(BF16), w_ref[...])
    o_ref[...] = x_ref[...] + gate_ref[...] * y


def _outproj(a, r, x, mod, nw, w_out, *, n_seq, seq_len, shared_mod):
    tm = min(512, seq_len)
    tps = seq_len // tm
    rows = n_seq * seq_len
    gate_idx = (lambda i: (0, 0, 2)) if shared_mod else (lambda i: (i // tps, 0, 2))
    return pl.pallas_call(
        _outproj_kernel,
        grid=(rows // tm,),
        in_specs=[pl.BlockSpec((tm, D_ATTN), lambda i: (i, 0)),
                  pl.BlockSpec((tm, D_LRU), lambda i: (i, 0)),
                  pl.BlockSpec((tm, D_MODEL), lambda i: (i, 0)),
                  pl.BlockSpec((None, 1, D_MODEL), gate_idx),
                  _resident((1, D_MODEL)),
                  _resident((D_MODEL, D_MODEL))],
        out_specs=pl.BlockSpec((tm, D_MODEL), lambda i: (i, 0)),
        out_shape=jax.ShapeDtypeStruct((rows, D_MODEL), F32),
        compiler_params=_params("arbitrary"),
        name="out_proj",
    )(a, r, x, mod, nw, w_out)


def _ffn_kernel(*refs, tm, tps, final_norm):
    x_ref, xp_ref, xn_ref, mod_ref, wup_ref, cw_ref, cb_ref, wdn_ref = refs[:8]
    if final_norm:
        fnw_ref, o_ref, hext_ref, act_ref = refs[8:]
    else:
        o_ref, hext_ref, act_ref = refs[8:]
    jt = pl.program_id(0) % tps
    shift = mod_ref[:, 0:D_MODEL]
    scale = mod_ref[:, D_MODEL:2 * D_MODEL]
    gate = mod_ref[:, 2 * D_MODEL:3 * D_MODEL]

    def norm_mod(x):
        return _rms(x) * (1.0 + scale) + shift

    x = x_ref[...]
    hext_ref[pl.ds(BF16_ROWS, tm), :] = norm_mod(x).astype(BF16)
    hp = norm_mod(xp_ref[...])
    hp = jnp.where(jt == 0, _shift_down(hp), hp)
    hn = norm_mod(xn_ref[...])
    hn = jnp.where(jt == tps - 1, _shift_up(hn), hn)
    zeros = jnp.zeros_like(hp)
    hext_ref[pl.ds(0, BF16_ROWS), :] = jnp.concatenate([zeros, hp], axis=0).astype(BF16)
    hext_ref[pl.ds(BF16_ROWS + tm, BF16_ROWS), :] = jnp.concatenate([hn, zeros], axis=0).astype(BF16)

    def conv(u, off):
        sl = pl.ds(off, FF_CHUNK)
        return (cw_ref[0:1, sl] * u[SUBLANES:SUBLANES + tm]
                + cw_ref[1:2, sl] * u[2 * SUBLANES:2 * SUBLANES + tm]
                + cw_ref[2:3, sl] * u[3 * SUBLANES:3 * SUBLANES + tm]
                + cb_ref[:, sl])

    def chunk(c, _):
        og = pl.multiple_of(c * FF_CHUNK, FF_CHUNK)
        ov = pl.multiple_of(D_FF + c * FF_CHUNK, FF_CHUNK)
        h = hext_ref[...]
        yg = conv(_dot(h, wup_ref[:, pl.ds(og, FF_CHUNK)]), og)
        yv = conv(_dot(h, wup_ref[:, pl.ds(ov, FF_CHUNK)]), ov)
        act_ref[:, pl.ds(og, FF_CHUNK)] = (yg * _sigmoid(yg) * yv).astype(BF16)
        return 0

    lax.fori_loop(0, D_FF // FF_CHUNK, chunk, 0)
    out = x + gate * _dot(act_ref[...], wdn_ref[...])
    if final_norm:
        out = _rms(out) * fnw_ref[...]
    o_ref[...] = out


def _ffn(x, mod, w_up, conv_w, conv_b, w_down, final_w, *, n_seq, seq_len, shared_mod):
    tm = min(512, seq_len)
    tps = seq_len // tm
    rows = n_seq * seq_len
    gps = seq_len // SUBLANES
    gpt = tm // SUBLANES
    mod_idx = (lambda i: (0, 0, 1)) if shared_mod else (lambda i: (i // tps, 0, 1))

    def prev_idx(i):
        jt = i % tps
        return ((i // tps) * gps + jnp.where(jt == 0, gps - 1, jt * gpt - 1), 0)

    def next_idx(i):
        jt = i % tps
        return ((i // tps) * gps + jnp.where(jt == tps - 1, 0, (jt + 1) * gpt), 0)

    final_norm = final_w is not None
    in_specs = [pl.BlockSpec((tm, D_MODEL), lambda i: (i, 0)),
                pl.BlockSpec((SUBLANES, D_MODEL), prev_idx),
                pl.BlockSpec((SUBLANES, D_MODEL), next_idx),
                pl.BlockSpec((None, 1, 3 * D_MODEL), mod_idx),
                _resident((D_MODEL, 2 * D_FF)),
                _resident((3, 2 * D_FF)),
                _resident((1, 2 * D_FF)),
                _resident((D_FF, D_MODEL))]
    args = [x, x, x, mod, w_up, conv_w, conv_b, w_down]
    if final_norm:
        in_specs.append(_resident((1, D_MODEL)))
        args.append(final_w)
    kern = functools.partial(_ffn_kernel, tm=tm, tps=tps, final_norm=final_norm)
    return pl.pallas_call(
        kern,
        grid=(rows // tm,),
        in_specs=in_specs,
        out_specs=pl.BlockSpec((tm, D_MODEL), lambda i: (i, 0)),
        out_shape=jax.ShapeDtypeStruct((rows, D_MODEL), F32),
        scratch_shapes=[pltpu.VMEM((tm + 2 * BF16_ROWS, D_MODEL), BF16),
                        pltpu.VMEM((tm, D_FF), BF16)],
        compiler_params=_params("arbitrary"),
        name="conv_ffn_final" if final_norm else "conv_ffn",
    )(*args)


def _to_segments(x):
    b, t, c = x.shape
    return x.reshape(b, SUBLANES, t // SUBLANES, c).transpose(0, 2, 1, 3).reshape(b * t, c)


def _from_segments(y, b, t):
    return y.reshape(b, t // SUBLANES, SUBLANES, -1).transpose(0, 2, 1, 3).reshape(b, t, -1)


def _rope_tables(seq_len):
    r = jnp.arange(seq_len)
    t = (r % SUBLANES) * (seq_len // SUBLANES) + r // SUBLANES
    row = (t // GRID_W).astype(F32)
    col = (t % GRID_W).astype(F32)
    pairs = HEAD_DIM // 4
    inv = ROPE_THETA ** (-jnp.arange(pairs, dtype=F32) / pairs)
    ang = jnp.concatenate([row[:, None] * inv, col[:, None] * inv], axis=-1)
    cos, sin = jnp.cos(ang), jnp.sin(ang)
    return (jnp.concatenate([cos, cos, cos, cos], axis=-1),
            jnp.concatenate([-sin, sin, -sin, sin], axis=-1))


def _gate_weights(wa, wx, ba, bx):
    n_slab = D_LRU // LANES
    per = LANES // wa.shape[-1]

    def slab_diag(w):
        w = w.reshape(n_slab, per, w.shape[-2], w.shape[-1])
        eye = jnp.eye(per, dtype=w.dtype)
        return jnp.einsum('spcd,pq->spcqd', w, eye).reshape(n_slab, LANES, LANES)

    wg = jnp.concatenate([slab_diag(wa[0]), slab_diag(wx[0]), slab_diag(wa[1]), slab_diag(wx[1])],
                         axis=-1).astype(BF16)
    bg = jnp.stack([ba[0], bx[0], ba[1], bx[1]], axis=0).reshape(4, n_slab, LANES)
    bg = bg.transpose(1, 0, 2).reshape(n_slab, 1, 4 * LANES)
    return wg, bg


def kernel(x, c, ctx, c_ctx, w_ada, b_ada, w_in, q_norm_w, k_norm_w, lru_conv_w, lru_conv_b, lru_wa,
           lru_ba, lru_wx, lru_bx, lru_lambda, attn_out_norm_w, lru_out_norm_w, w_out, ffn_w_up,
           ffn_conv_w, ffn_conv_b, ffn_w_down, final_norm_w):
    batch, seq, _ = x.shape
    ctx_len = ctx.shape[1]
    depth = w_in.shape[0]

    mod_rows = -(-(batch + 1) // SUBLANES) * SUBLANES
    cvec = jnp.zeros((mod_rows, D_MODEL), F32).at[:batch].set(c).at[batch].set(c_ctx)
    mods = _modulation(cvec, w_ada, b_ada)

    cos, sin = _rope_tables(seq)
    ones = jnp.ones((ctx_len, LANES), F32)
    bd = jnp.kron(jnp.eye(QK_W // HEAD_DIM, dtype=F32), jnp.ones((HEAD_DIM, HEAD_DIM), F32)).astype(BF16)

    x_lat = _to_segments(x)
    x_ctx = _to_segments(ctx)
    lat = dict(n_seq=batch, seq_len=seq, shared_mod=False)
    cx = dict(n_seq=batch, seq_len=ctx_len, shared_mod=True)

    for l in range(depth):
        ctx_out = l < depth - 1
        last = l == depth - 1
        mod_lat = mods[l, :batch].reshape(batch, 1, 6 * D_MODEL)
        mod_ctx = mods[l, batch:batch + 1].reshape(1, 1, 6 * D_MODEL)
        w_in_l = w_in[l].astype(BF16)
        nw_qk = jnp.concatenate([jnp.tile(q_norm_w[l], D_ATTN // HEAD_DIM),
                                 jnp.tile(k_norm_w[l], KV_W // HEAD_DIM)]).reshape(1, QK_W)
        q_l, kt_l, v_l, xr_l, g_l = _inproj(x_lat, mod_lat, w_in_l, bd, nw_qk, cos, sin,
                                            use_rope=True, **lat)
        q_c, kt_c, v_c, xr_c, g_c = _inproj(x_ctx, mod_ctx, w_in_l, bd, nw_qk, ones, ones,
                                            use_rope=False, **cx)

        a_l = _attention(q_l, kt_c, v_c, kt_l, v_l, n_seq=batch, q_len=seq, ctx_len=ctx_len,
                         lat_len=seq)
        wg, bg = _gate_weights(lru_wa[l], lru_wx[l], lru_ba[l], lru_bx[l])
        r = _lru(xr_c, g_c, xr_l, g_l, lru_conv_w[l], lru_conv_b[l].reshape(1, D_LRU), wg, bg,
                 lru_lambda[l], n_seq=batch, ctx_len=ctx_len, lat_len=seq, ctx_out=ctx_out)

        nw_out = jnp.concatenate([attn_out_norm_w[l], lru_out_norm_w[l]]).reshape(1, D_MODEL)
        w_out_l = w_out[l].astype(BF16)
        w_up_l = ffn_w_up[l].astype(BF16)
        w_dn_l = ffn_w_down[l].astype(BF16)
        cb_l = ffn_conv_b[l].reshape(1, 2 * D_FF)
        x_lat = _outproj(a_l, r[0], x_lat, mod_lat, nw_out, w_out_l, **lat)
        x_lat = _ffn(x_lat, mod_lat, w_up_l, ffn_conv_w[l], cb_l, w_dn_l,
                     final_norm_w.reshape(1, D_MODEL) if last else None, **lat)
        if ctx_out:
            a_c = _attention(q_c, kt_c, v_c, None, None, n_seq=batch, q_len=ctx_len,
                             ctx_len=ctx_len, lat_len=0)
            x_ctx = _outproj(a_c, r[1], x_ctx, mod_ctx, nw_out, w_out_l, **cx)
            x_ctx = _ffn(x_ctx, mod_ctx, w_up_l, ffn_conv_w[l], cb_l, w_dn_l, None, **cx)
    return _from_segments(x_lat, batch, seq)
```

```python
import functools

import jax
import jax.numpy as jnp
from jax import lax
from jax.experimental import pallas as pl
from jax.experimental.pallas import tpu as pltpu

F32 = jnp.float32
BF16 = jnp.bfloat16

D_MODEL = 1024
D_ATTN = 512
D_LRU = 512
KV_W = 128
HEAD_DIM = 64
N_HEADS = 8
N_KV = 2
KV_GROUP = 4
LOG2E = 1.4426950408889634
QK_W = D_ATTN + KV_W
D_IN = D_ATTN + 2 * KV_W + 2 * D_LRU
D_FF = 2816
LRU_C = 8.0
GRID_W = 64
ROPE_THETA = 10000.0
EPS = 1e-6
SUBLANES = 8
LANES = 128
BF16_ROWS = 16
V_ROWS = HEAD_DIM + BF16_ROWS
FF_CHUNK = 256
N_SUB = 2
VMEM_LIMIT = 56 * 1024 * 1024


def _dot(a, b):
    return jnp.dot(a, b, preferred_element_type=F32)


def _sigmoid(z):
    return 0.5 * jnp.tanh(0.5 * z) + 0.5


def _rms(x):
    return x * lax.rsqrt(jnp.mean(x * x, axis=-1, keepdims=True) + EPS)


def _shift_down(g):
    rows = lax.broadcasted_iota(jnp.int32, g.shape, 0)
    return jnp.where(rows == 0, 0.0, pltpu.roll(g, 1, 0))


def _shift_up(g):
    rows = lax.broadcasted_iota(jnp.int32, g.shape, 0)
    return jnp.where(rows == SUBLANES - 1, 0.0, pltpu.roll(g, SUBLANES - 1, 0))


def _params(*sem):
    return pltpu.CompilerParams(dimension_semantics=sem, vmem_limit_bytes=VMEM_LIMIT)


def _resident(shape):
    nd = len(shape)
    return pl.BlockSpec(shape, lambda *_: (0,) * nd, pipeline_mode=pl.Buffered(1))


def _mod_kernel(c_ref, w_ref, b_ref, o_ref):
    c = c_ref[...]
    s = (c * _sigmoid(c)).astype(BF16)
    o_ref[...] = _dot(s, w_ref[...].astype(BF16)) + b_ref[...]


def _modulation(cvec, w_ada, b_ada, tn=1536):
    depth, d, n = w_ada.shape
    rows = cvec.shape[0]
    return pl.pallas_call(
        _mod_kernel,
        grid=(depth, n // tn),
        in_specs=[pl.BlockSpec((rows, d), lambda l, j: (0, 0)),
                  pl.BlockSpec((None, d, tn), lambda l, j: (l, 0, j)),
                  pl.BlockSpec((None, 1, tn), lambda l, j: (l, 0, j))],
        out_specs=pl.BlockSpec((None, rows, tn), lambda l, j: (l, 0, j)),
        out_shape=jax.ShapeDtypeStruct((depth, rows, n), F32),
        compiler_params=_params("arbitrary", "arbitrary"),
        name="adaln_mod",
    )(cvec, w_ada, b_ada.reshape(depth, 1, n))


def _inproj_kernel(x_ref, mod_ref, w_ref, bd_ref, nw_ref, cos_ref, sin_ref,
                   qt_ref, k_ref, vt_ref, xl_ref, g_ref, *, use_rope):
    shift = mod_ref[:, 0:D_MODEL]
    scale = mod_ref[:, D_MODEL:2 * D_MODEL]
    tm = x_ref.shape[0]
    sub = tm // N_SUB

    def project(r0):
        h = _rms(x_ref[pl.ds(r0, sub), :]) * (1.0 + scale) + shift
        y = _dot(h.astype(BF16), w_ref[...])
        qk = y[:, 0:QK_W]
        return y, _dot((qk * qk).astype(BF16), bd_ref[...])

    def finish(r0, y_ss):
        y, ss = y_ss
        rows = pl.ds(r0, sub)
        qk = y[:, 0:QK_W]
        qk = qk * lax.rsqrt(ss * (1.0 / HEAD_DIM) + EPS) * nw_ref[...]
        if use_rope:
            lane = lax.broadcasted_iota(jnp.int32, (sub, LANES), 1)
            first_half = (lane % HEAD_DIM) < (HEAD_DIM // 2)
            cos = cos_ref[rows, :]
            sin = sin_ref[rows, :]
            cols = []
            for k in range(QK_W // LANES):
                xc = qk[:, LANES * k:LANES * (k + 1)]
                other = jnp.where(first_half, pltpu.roll(xc, LANES - HEAD_DIM // 2, 1),
                                  pltpu.roll(xc, HEAD_DIM // 2, 1))
                cols.append(xc * cos + other * sin)
            qk = jnp.concatenate(cols, axis=1)
        qt = (qk[:, 0:D_ATTN] * (LOG2E * HEAD_DIM ** -0.5)).T
        for h in range(N_HEADS):
            qt_ref[h, :, rows] = qt[HEAD_DIM * h:HEAD_DIM * (h + 1)].astype(BF16)
        for g in range(N_KV):
            k_ref[g, rows, :] = qk[:, D_ATTN + HEAD_DIM * g:D_ATTN + HEAD_DIM * (g + 1)].astype(BF16)
        vt = y[:, QK_W:QK_W + KV_W].T.astype(BF16)
        ones = jnp.ones((V_ROWS - HEAD_DIM, sub), BF16)
        for g in range(N_KV):
            vt_ref[g, :, rows] = jnp.concatenate([vt[HEAD_DIM * g:HEAD_DIM * (g + 1)], ones], axis=0)
        xl_ref[rows, :] = y[:, QK_W + KV_W:QK_W + KV_W + D_LRU]
        g_ref[rows, :] = y[:, QK_W + KV_W + D_LRU:D_IN]

    y = project(0)
    for s in range(N_SUB):
        y_next = project((s + 1) * sub) if s + 1 < N_SUB else None
        finish(s * sub, y)
        y = y_next


def _inproj(x, mod, w_in, bd, nw, cos, sin, *, n_seq, seq_len, shared_mod, use_rope):
    tm = min(512, seq_len)
    tps = seq_len // tm
    rows = n_seq * seq_len
    mod_idx = (lambda i: (0, 0, 0)) if shared_mod else (lambda i: (i // tps, 0, 0))
    kern = functools.partial(_inproj_kernel, use_rope=use_rope)
    return pl.pallas_call(
        kern,
        grid=(rows // tm,),
        in_specs=[pl.BlockSpec((tm, D_MODEL), lambda i: (i, 0)),
                  pl.BlockSpec((None, 1, 2 * D_MODEL), mod_idx),
                  _resident((D_MODEL, D_IN)),
                  _resident((QK_W, QK_W)),
                  _resident((1, QK_W)),
                  pl.BlockSpec((tm, LANES), lambda i: (i % tps, 0)),
                  pl.BlockSpec((tm, LANES), lambda i: (i % tps, 0))],
        out_specs=[pl.BlockSpec((None, N_HEADS, HEAD_DIM, tm), lambda i: (i // tps, 0, 0, i % tps)),
                   pl.BlockSpec((None, N_KV, tm, HEAD_DIM), lambda i: (i // tps, 0, i % tps, 0)),
                   pl.BlockSpec((None, N_KV, V_ROWS, tm), lambda i: (i // tps, 0, 0, i % tps)),
                   pl.BlockSpec((tm, D_LRU), lambda i: (i, 0)),
                   pl.BlockSpec((tm, D_LRU), lambda i: (i, 0))],
        out_shape=[jax.ShapeDtypeStruct((n_seq, N_HEADS, HEAD_DIM, seq_len), BF16),
                   jax.ShapeDtypeStruct((n_seq, N_KV, seq_len, HEAD_DIM), BF16),
                   jax.ShapeDtypeStruct((n_seq, N_KV, V_ROWS, seq_len), BF16),
                   jax.ShapeDtypeStruct((rows, D_LRU), F32),
                   jax.ShapeDtypeStruct((rows, D_LRU), F32)],
        compiler_params=_params("arbitrary"),
        name="in_proj_rope" if use_rope else "in_proj",
    )(x, mod, w_in, bd, nw, cos, sin)


def _attn_kernel(*refs, tq, tk, n_lat_chunks):
    if n_lat_chunks:
        qt_ref, kc_ref, vtc_ref, kl_ref, vtl_ref, o_ref, sc_ref, sa_ref, sb_ref = refs
    else:
        qt_ref, kc_ref, vtc_ref, o_ref = refs

    def queries(g):
        return jnp.concatenate([qt_ref[KV_GROUP * g + h] for h in range(KV_GROUP)], axis=1)

    def ctx_scores(g, qt):
        return _dot(kc_ref[g], qt)

    def first(g, s_ref):
        m = jnp.max(s_ref[...], axis=0, keepdims=True)
        return m, _dot(vtc_ref[g], jnp.exp2(s_ref[...] - m).astype(BF16))

    def scores(g, qt, c):
        off = pl.multiple_of(c * tk, tk)
        return _dot(kl_ref[g, pl.ds(off, tk), :], qt)

    def update(g, c, s_ref, carry):
        m, acc = carry
        off = pl.multiple_of(c * tk, tk)
        m_new = jnp.maximum(m, jnp.max(s_ref[...], axis=0, keepdims=True))
        p = jnp.exp2(s_ref[...] - m_new).astype(BF16)
        return m_new, jnp.exp2(m - m_new) * acc + _dot(vtl_ref[g, :, pl.ds(off, tk)], p)

    def finish(g, carry):
        acc = carry[1]
        ot = acc[0:HEAD_DIM] * (1.0 / acc[HEAD_DIM:HEAD_DIM + 1])
        for k in range(KV_GROUP):
            h = KV_GROUP * g + k
            o_ref[:, HEAD_DIM * h:HEAD_DIM * (h + 1)] = ot[:, k * tq:(k + 1) * tq].T

    if not n_lat_chunks:
        for g in range(N_KV):
            s = ctx_scores(g, queries(g))
            m = jnp.max(s, axis=0, keepdims=True)
            finish(g, (m, _dot(vtc_ref[g], jnp.exp2(s - m).astype(BF16))))
        return

    qt = queries(0)
    sc_ref[...] = ctx_scores(0, qt)
    sa_ref[...] = scores(0, qt, 0)
    for g in range(N_KV):
        carry = first(g, sc_ref)

        def pair(j, carry, g=g, qt=qt):
            sb_ref[...] = scores(g, qt, 2 * j + 1)
            carry = update(g, 2 * j, sa_ref, carry)
            sa_ref[...] = scores(g, qt, 2 * j + 2)
            return update(g, 2 * j + 1, sb_ref, carry)

        for j in range(n_lat_chunks // 2 - 1):
            carry = pair(j, carry)
        sb_ref[...] = scores(g, qt, n_lat_chunks - 1)
        carry = update(g, n_lat_chunks - 2, sa_ref, carry)
        if g + 1 < N_KV:
            qt = queries(g + 1)
            sc_ref[...] = ctx_scores(g + 1, qt)
        carry = update(g, n_lat_chunks - 1, sb_ref, carry)
        if g + 1 < N_KV:
            sa_ref[...] = scores(g + 1, qt, 0)
        finish(g, carry)


def _attention(qt, k_ctx, vt_ctx, k_lat, vt_lat, *, n_seq, q_len, ctx_len, lat_len):
    tq = min(256, q_len)
    tk = min(512, lat_len // 2) if lat_len else 0
    n_lat_chunks = lat_len // tk if lat_len else 0
    assert n_lat_chunks % 2 == 0
    nq = q_len // tq
    kern = functools.partial(_attn_kernel, tq=tq, tk=tk, n_lat_chunks=n_lat_chunks)
    in_specs = [pl.BlockSpec((None, N_HEADS, HEAD_DIM, tq), lambda b, i: (b, 0, 0, i)),
                pl.BlockSpec((None, N_KV, ctx_len, HEAD_DIM), lambda b, i: (b, 0, 0, 0)),
                pl.BlockSpec((None, N_KV, V_ROWS, ctx_len), lambda b, i: (b, 0, 0, 0))]
    args = [qt, k_ctx, vt_ctx]
    scratch = []
    if lat_len:
        in_specs += [pl.BlockSpec((None, N_KV, lat_len, HEAD_DIM), lambda b, i: (b, 0, 0, 0)),
                     pl.BlockSpec((None, N_KV, V_ROWS, lat_len), lambda b, i: (b, 0, 0, 0))]
        args += [k_lat, vt_lat]
        scratch = [pltpu.VMEM((ctx_len, KV_GROUP * tq), F32)] + [pltpu.VMEM((tk, KV_GROUP * tq), F32)] * 2
    return pl.pallas_call(
        kern,
        grid=(n_seq, nq),
        in_specs=in_specs,
        out_specs=pl.BlockSpec((tq, D_ATTN), lambda b, i: (b * nq + i, 0)),
        out_shape=jax.ShapeDtypeStruct((n_seq * q_len, D_ATTN), F32),
        scratch_shapes=scratch,
        compiler_params=_params("arbitrary", "arbitrary"),
        name="attn_latent" if lat_len else "attn_context",
    )(*args)


def _lru_sequence(x_ref, g_ref, o_ref, h0, consts, scr, *, n_groups):
    cw_ref, cb_ref, wg_ref, bg_ref, nsp = consts
    xe_ref, af_ref, bf_ref, ar_ref, br_ref = scr
    rows = SUBLANES * n_groups
    width = x_ref.shape[1]
    chunk = min(256, rows)
    n_chunks = rows // chunk

    xe_ref[pl.ds(2 * SUBLANES, rows), :] = x_ref[...]
    xe_ref[pl.ds(0, SUBLANES), :] = _shift_down(x_ref[pl.ds(rows - 2 * SUBLANES, SUBLANES), :])
    xe_ref[pl.ds(SUBLANES, SUBLANES), :] = _shift_down(x_ref[pl.ds(rows - SUBLANES, SUBLANES), :])
    xe_ref[pl.ds(rows + 2 * SUBLANES, SUBLANES), :] = _shift_up(x_ref[pl.ds(0, SUBLANES), :])

    def gates(c, _):
        r0 = pl.multiple_of(c * chunk, chunk)
        xc = cb_ref[...]
        for k in range(4):
            xc = xc + cw_ref[k:k + 1, :] * xe_ref[pl.ds(r0 + SUBLANES * k, chunk), :]
        pre = _dot(xc.astype(BF16), wg_ref[...]) + bg_ref[...]
        for d, (a_ref, b_ref) in enumerate(((af_ref, bf_ref), (ar_ref, br_ref))):
            r_gate = _sigmoid(pre[:, (2 * d) * width:(2 * d + 1) * width])
            i_gate = _sigmoid(pre[:, (2 * d + 1) * width:(2 * d + 2) * width])
            log_a = nsp[d] * r_gate
            a = jnp.exp(log_a)
            b = jnp.sqrt(jnp.tanh(-log_a) * (1.0 + a * a)) * (i_gate * xc)
            a_ref[pl.ds(r0, chunk), :] = a
            b_ref[pl.ds(r0, chunk), :] = b
        return 0

    lax.fori_loop(0, n_chunks, gates, 0)

    def scan(j, carry):
        hf, pf, hr, pr = carry
        rf = pl.multiple_of(j * SUBLANES, SUBLANES)
        rr = pl.multiple_of((n_groups - 1 - j) * SUBLANES, SUBLANES)
        a = af_ref[pl.ds(rf, SUBLANES), :]
        hf = a * hf + bf_ref[pl.ds(rf, SUBLANES), :]
        pf = a * pf
        bf_ref[pl.ds(rf, SUBLANES), :] = hf
        af_ref[pl.ds(rf, SUBLANES), :] = pf
        a = ar_ref[pl.ds(rr, SUBLANES), :]
        hr = a * hr + br_ref[pl.ds(rr, SUBLANES), :]
        pr = a * pr
        br_ref[pl.ds(rr, SUBLANES), :] = hr
        ar_ref[pl.ds(rr, SUBLANES), :] = pr
        return hf, pf, hr, pr

    zero = jnp.zeros((SUBLANES, width), F32)
    one = jnp.ones((SUBLANES, width), F32)
    hf, pf, hr, pr = lax.fori_loop(0, n_groups, scan, (zero, one, zero, one), unroll=4)

    e = h0[0]
    init_f = []
    for s in range(SUBLANES):
        init_f.append(e)
        e = hf[s:s + 1, :] + pf[s:s + 1, :] * e
    fin_f = e
    e = h0[1]
    init_r = [None] * SUBLANES
    for s in range(SUBLANES - 1, -1, -1):
        init_r[s] = e
        e = hr[s:s + 1, :] + pr[s:s + 1, :] * e
    fin_r = e

    if o_ref is not None:
        init_f = jnp.concatenate(init_f, axis=0)[None]
        init_r = jnp.concatenate(init_r, axis=0)[None]

        def combine(c, _):
            r0 = pl.multiple_of(c * chunk, chunk)
            sl = pl.ds(r0, chunk)
            shp = (chunk // SUBLANES, SUBLANES, width)
            h = (bf_ref[sl, :].reshape(shp) + af_ref[sl, :].reshape(shp) * init_f
                 + br_ref[sl, :].reshape(shp) + ar_ref[sl, :].reshape(shp) * init_r)
            gt = g_ref[sl, :]
            gelu = 0.5 * gt * (1.0 + jnp.tanh(0.7978845608028654 * (gt + 0.044715 * gt * gt * gt)))
            o_ref[sl, :] = h.reshape(chunk, width) * gelu
            return 0

        lax.fori_loop(0, n_chunks, combine, 0)
    return fin_f, fin_r


def _lru_kernel(*refs, ctx_groups, lat_groups, ctx_out):
    xc_ref, gc_ref, xl_ref, gl_ref, cw_ref, cb_ref, wg_ref, bg_ref, lam_ref = refs[:9]
    n_out = 2 if ctx_out else 1
    outs = refs[9:9 + n_out]
    scr = refs[9 + n_out:]
    z = -lam_ref[...]
    softplus = jnp.maximum(z, 0.0) + jnp.log1p(jnp.exp(-jnp.abs(z)))
    nsp = [-LRU_C * softplus[d:d + 1, :] for d in range(2)]
    consts = (cw_ref, cb_ref, wg_ref, bg_ref, nsp)
    zero = jnp.zeros((1, xc_ref.shape[1]), F32)
    fin = _lru_sequence(xc_ref, gc_ref, outs[1] if ctx_out else None, (zero, zero), consts, scr,
                        n_groups=ctx_groups)
    _lru_sequence(xl_ref, gl_ref, outs[0], fin, consts, scr, n_groups=lat_groups)


def _lru(x_ctx, g_ctx, x_lat, g_lat, conv_w, conv_b, wg, bg, lam, *, n_seq, ctx_len, lat_len, ctx_out):
    w = LANES
    n_slab = D_LRU // w
    kern = functools.partial(_lru_kernel, ctx_groups=ctx_len // SUBLANES,
                             lat_groups=lat_len // SUBLANES, ctx_out=ctx_out)
    ctx_spec = pl.BlockSpec((ctx_len, w), lambda b, c: (b, c))
    lat_spec = pl.BlockSpec((lat_len, w), lambda b, c: (b, c))
    out_specs = [lat_spec]
    out_shape = [jax.ShapeDtypeStruct((n_seq * lat_len, D_LRU), F32)]
    if ctx_out:
        out_specs.append(ctx_spec)
        out_shape.append(jax.ShapeDtypeStruct((n_seq * ctx_len, D_LRU), F32))
    max_rows = max(ctx_len, lat_len)
    return pl.pallas_call(
        kern,
        grid=(n_seq, n_slab),
        in_specs=[ctx_spec, ctx_spec, lat_spec, lat_spec,
                  pl.BlockSpec((4, w), lambda b, c: (0, c)),
                  pl.BlockSpec((1, w), lambda b, c: (0, c)),
                  pl.BlockSpec((None, w, 4 * w), lambda b, c: (c, 0, 0)),
                  pl.BlockSpec((None, 1, 4 * w), lambda b, c: (c, 0, 0)),
                  pl.BlockSpec((2, w), lambda b, c: (0, c))],
        out_specs=out_specs,
        out_shape=out_shape,
        scratch_shapes=[pltpu.VMEM((max_rows + 3 * SUBLANES, w), F32)]
                       + [pltpu.VMEM((max_rows, w), F32)] * 4,
        compiler_params=_params("arbitrary", "arbitrary"),
        name="rglru_bidir",
    )(x_ctx, g_ctx, x_lat, g_lat, conv_w, conv_b, wg, bg, lam)


def _outproj_kernel(a_ref, r_ref, x_ref, gate_ref, nw_ref, w_ref, o_ref):
    sub = x_ref.shape[0] // N_SUB

    def project(r0):
        rows = pl.ds(r0, sub)
        cat = jnp.concatenate([_rms(a_ref[rows, :]), _rms(r_ref[rows, :])], axis=1) * nw_ref[...]
        return _dot(cat.astype(BF16), w_ref[...])

    y = project(0)
    for s in range(N_SUB):
        y_next = project((s + 1) * sub) if s + 1 < N_SUB else None
        rows = pl.ds(s * sub, sub)
        o_ref[rows, :] = x_ref[rows, :] + gate_ref[...] * y
        y = y_next


def _outproj(a, r, x, mod, nw, w_out, *, n_seq, seq_len, shared_mod):
    tm = min(512, seq_len)
    tps = seq_len // tm
    rows = n_seq * seq_len
    gate_idx = (lambda i: (0, 0, 2)) if shared_mod else (lambda i: (i // tps, 0, 2))
    return pl.pallas_call(
        _outproj_kernel,
        grid=(rows // tm,),
        in_specs=[pl.BlockSpec((tm, D_ATTN), lambda i: (i, 0)),
                  pl.BlockSpec((tm, D_LRU), lambda i: (i, 0)),
                  pl.BlockSpec((tm, D_MODEL), lambda i: (i, 0)),
                  pl.BlockSpec((None, 1, D_MODEL), gate_idx),
                  _resident((1, D_MODEL)),
                  _resident((D_MODEL, D_MODEL))],
        out_specs=pl.BlockSpec((tm, D_MODEL), lambda i: (i, 0)),
        out_shape=jax.ShapeDtypeStruct((rows, D_MODEL), F32),
        compiler_params=_params("arbitrary"),
        name="out_proj",
    )(a, r, x, mod, nw, w_out)


def _ffn_kernel(*refs, tm, tps, final_norm):
    x_ref, xp_ref, xn_ref, mod_ref, wup_ref, cw_ref, cb_ref, wdn_ref = refs[:8]
    if final_norm:
        fnw_ref, o_ref, hext_ref, act_ref, ua_ref, ub_ref = refs[8:]
    else:
        o_ref, hext_ref, act_ref, ua_ref, ub_ref = refs[8:]
    assert (D_FF // FF_CHUNK) % 2 == 1
    jt = pl.program_id(0) % tps
    shift = mod_ref[:, 0:D_MODEL]
    scale = mod_ref[:, D_MODEL:2 * D_MODEL]
    gate = mod_ref[:, 2 * D_MODEL:3 * D_MODEL]

    def norm_mod(x):
        return _rms(x) * (1.0 + scale) + shift

    x = x_ref[...]
    hext_ref[pl.ds(BF16_ROWS, tm), :] = norm_mod(x).astype(BF16)
    hp = norm_mod(xp_ref[...])
    hp = jnp.where(jt == 0, _shift_down(hp), hp)
    hn = norm_mod(xn_ref[...])
    hn = jnp.where(jt == tps - 1, _shift_up(hn), hn)
    zeros = jnp.zeros_like(hp)
    hext_ref[pl.ds(0, BF16_ROWS), :] = jnp.concatenate([zeros, hp], axis=0).astype(BF16)
    hext_ref[pl.ds(BF16_ROWS + tm, BF16_ROWS), :] = jnp.concatenate([hn, zeros], axis=0).astype(BF16)

    def up(c, u_ref):
        for k in range(2):
            off = pl.multiple_of(k * D_FF + c * FF_CHUNK, FF_CHUNK)
            u_ref[k] = _dot(hext_ref[...], wup_ref[:, pl.ds(off, FF_CHUNK)])

    def conv(u_ref, k, off):
        sl = pl.ds(off, FF_CHUNK)
        return (cw_ref[0:1, sl] * u_ref[k, pl.ds(SUBLANES, tm), :]
                + cw_ref[1:2, sl] * u_ref[k, pl.ds(2 * SUBLANES, tm), :]
                + cw_ref[2:3, sl] * u_ref[k, pl.ds(3 * SUBLANES, tm), :]
                + cb_ref[:, sl])

    def activate(c, u_ref):
        og = pl.multiple_of(c * FF_CHUNK, FF_CHUNK)
        ov = pl.multiple_of(D_FF + c * FF_CHUNK, FF_CHUNK)
        yg = conv(u_ref, 0, og)
        yv = conv(u_ref, 1, ov)
        act_ref[:, pl.ds(og, FF_CHUNK)] = (yg * _sigmoid(yg) * yv).astype(BF16)

    n_chunks = D_FF // FF_CHUNK
    up(0, ua_ref)

    def pair(j, _):
        up(2 * j + 1, ub_ref)
        activate(2 * j, ua_ref)
        up(2 * j + 2, ua_ref)
        activate(2 * j + 1, ub_ref)
        return 0

    lax.fori_loop(0, (n_chunks - 1) // 2, pair, 0)
    activate(n_chunks - 1, ua_ref)
    out = x + gate * _dot(act_ref[...], wdn_ref[...])
    if final_norm:
        out = _rms(out) * fnw_ref[...]
    o_ref[...] = out


def _ffn(x, mod, w_up, conv_w, conv_b, w_down, final_w, *, n_seq, seq_len, shared_mod):
    tm = min(512, seq_len)
    tps = seq_len // tm
    rows = n_seq * seq_len
    gps = seq_len // SUBLANES
    gpt = tm // SUBLANES
    mod_idx = (lambda i: (0, 0, 1)) if shared_mod else (lambda i: (i // tps, 0, 1))

    def prev_idx(i):
        jt = i % tps
        return ((i // tps) * gps + jnp.where(jt == 0, gps - 1, jt * gpt - 1), 0)

    def next_idx(i):
        jt = i % tps
        return ((i // tps) * gps + jnp.where(jt == tps - 1, 0, (jt + 1) * gpt), 0)

    final_norm = final_w is not None
    in_specs = [pl.BlockSpec((tm, D_MODEL), lambda i: (i, 0)),
                pl.BlockSpec((SUBLANES, D_MODEL), prev_idx),
                pl.BlockSpec((SUBLANES, D_MODEL), next_idx),
                pl.BlockSpec((None, 1, 3 * D_MODEL), mod_idx),
                _resident((D_MODEL, 2 * D_FF)),
                _resident((3, 2 * D_FF)),
                _resident((1, 2 * D_FF)),
                _resident((D_FF, D_MODEL))]
    args = [x, x, x, mod, w_up, conv_w, conv_b, w_down]
    if final_norm:
        in_specs.append(_resident((1, D_MODEL)))
        args.append(final_w)
    kern = functools.partial(_ffn_kernel, tm=tm, tps=tps, final_norm=final_norm)
    return pl.pallas_call(
        kern,
        grid=(rows // tm,),
        in_specs=in_specs,
        out_specs=pl.BlockSpec((tm, D_MODEL), lambda i: (i, 0)),
        out_shape=jax.ShapeDtypeStruct((rows, D_MODEL), F32),
        scratch_shapes=[pltpu.VMEM((tm + 2 * BF16_ROWS, D_MODEL), BF16),
                        pltpu.VMEM((tm, D_FF), BF16)]
                       + [pltpu.VMEM((2, tm + 2 * BF16_ROWS, FF_CHUNK), F32)] * 2,
        compiler_params=_params("arbitrary"),
        name="conv_ffn_final" if final_norm else "conv_ffn",
    )(*args)


def _to_segments(x):
    b, t, c = x.shape
    return x.reshape(b, SUBLANES, t // SUBLANES, c).transpose(0, 2, 1, 3).reshape(b * t, c)


def _from_segments(y, b, t):
    return y.reshape(b, t // SUBLANES, SUBLANES, -1).transpose(0, 2, 1, 3).reshape(b, t, -1)


def _rope_tables(seq_len):
    r = jnp.arange(seq_len)
    t = (r % SUBLANES) * (seq_len // SUBLANES) + r // SUBLANES
    row = (t // GRID_W).astype(F32)
    col = (t % GRID_W).astype(F32)
    pairs = HEAD_DIM // 4
    inv = ROPE_THETA ** (-jnp.arange(pairs, dtype=F32) / pairs)
    ang = jnp.concatenate([row[:, None] * inv, col[:, None] * inv], axis=-1)
    cos, sin = jnp.cos(ang), jnp.sin(ang)
    return (jnp.concatenate([cos, cos, cos, cos], axis=-1),
            jnp.concatenate([-sin, sin, -sin, sin], axis=-1))


def _gate_weights(wa, wx, ba, bx):
    n_slab = D_LRU // LANES
    per = LANES // wa.shape[-1]

    def slab_diag(w):
        w = w.reshape(n_slab, per, w.shape[-2], w.shape[-1])
        eye = jnp.eye(per, dtype=w.dtype)
        return jnp.einsum('spcd,pq->spcqd', w, eye).reshape(n_slab, LANES, LANES)

    wg = jnp.concatenate([slab_diag(wa[0]), slab_diag(wx[0]), slab_diag(wa[1]), slab_diag(wx[1])],
                         axis=-1).astype(BF16)
    bg = jnp.stack([ba[0], bx[0], ba[1], bx[1]], axis=0).reshape(4, n_slab, LANES)
    bg = bg.transpose(1, 0, 2).reshape(n_slab, 1, 4 * LANES)
    return wg, bg


def kernel(x, c, ctx, c_ctx, w_ada, b_ada, w_in, q_norm_w, k_norm_w, lru_conv_w, lru_conv_b, lru_wa,
           lru_ba, lru_wx, lru_bx, lru_lambda, attn_out_norm_w, lru_out_norm_w, w_out, ffn_w_up,
           ffn_conv_w, ffn_conv_b, ffn_w_down, final_norm_w):
    batch, seq, _ = x.shape
    ctx_len = ctx.shape[1]
    depth = w_in.shape[0]

    mod_rows = -(-(batch + 1) // SUBLANES) * SUBLANES
    cvec = jnp.zeros((mod_rows, D_MODEL), F32).at[:batch].set(c).at[batch].set(c_ctx)
    mods = _modulation(cvec, w_ada, b_ada)

    cos, sin = _rope_tables(seq)
    ones = jnp.ones((ctx_len, LANES), F32)
    bd = jnp.kron(jnp.eye(QK_W // HEAD_DIM, dtype=F32), jnp.ones((HEAD_DIM, HEAD_DIM), F32)).astype(BF16)

    x_lat = _to_segments(x)
    x_ctx = _to_segments(ctx)
    lat = dict(n_seq=batch, seq_len=seq, shared_mod=False)
    cx = dict(n_seq=batch, seq_len=ctx_len, shared_mod=True)

    for l in range(depth):
        ctx_out = l < depth - 1
        last = l == depth - 1
        mod_lat = mods[l, :batch].reshape(batch, 1, 6 * D_MODEL)
        mod_ctx = mods[l, batch:batch + 1].reshape(1, 1, 6 * D_MODEL)
        w_in_l = w_in[l].astype(BF16)
        nw_qk = jnp.concatenate([jnp.tile(q_norm_w[l], D_ATTN // HEAD_DIM),
                                 jnp.tile(k_norm_w[l], KV_W // HEAD_DIM)]).reshape(1, QK_W)
        q_l, kt_l, v_l, xr_l, g_l = _inproj(x_lat, mod_lat, w_in_l, bd, nw_qk, cos, sin,
                                            use_rope=True, **lat)
        q_c, kt_c, v_c, xr_c, g_c = _inproj(x_ctx, mod_ctx, w_in_l, bd, nw_qk, ones, ones,
                                            use_rope=False, **cx)

        a_l = _attention(q_l, kt_c, v_c, kt_l, v_l, n_seq=batch, q_len=seq, ctx_len=ctx_len,
                         lat_len=seq)
        wg, bg = _gate_weights(lru_wa[l], lru_wx[l], lru_ba[l], lru_bx[l])
        r = _lru(xr_c, g_c, xr_l, g_l, lru_conv_w[l], lru_conv_b[l].reshape(1, D_LRU), wg, bg,
                 lru_lambda[l], n_seq=batch, ctx_len=ctx_len, lat_len=seq, ctx_out=ctx_out)

        nw_out = jnp.concatenate([attn_out_norm_w[l], lru_out_norm_w[l]]).reshape(1, D_MODEL)
        w_out_l = w_out[l].astype(BF16)
        w_up_l = ffn_w_up[l].astype(BF16)
        w_dn_l = ffn_w_down[l].astype(BF16)
        cb_l = ffn_conv_b[l].reshape(1, 2 * D_FF)
        x_lat = _outproj(a_l, r[0], x_lat, mod_lat, nw_out, w_out_l, **lat)
        x_lat = _ffn(x_lat, mod_lat, w_up_l, ffn_conv_w[l], cb_l, w_dn_l,
                     final_norm_w.reshape(1, D_MODEL) if last else None, **lat)
        if ctx_out:
            a_c = _attention(q_c, kt_c, v_c, None, None, n_seq=batch, q_len=ctx_len,
                             ctx_len=ctx_len, lat_len=0)
            x_ctx = _outproj(a_c, r[1], x_ctx, mod_ctx, nw_out, w_out_l, **cx)
            x_ctx = _ffn(x_ctx, mod_ctx, w_up_l, ffn_conv_w[l], cb_l, w_dn_l, None, **cx)
    return _from_segments(x_lat, batch, seq)
```

```python
import functools

import jax
import jax.numpy as jnp
from jax import lax
from jax.experimental import pallas as pl
from jax.experimental.pallas import tpu as pltpu

F32 = jnp.float32
BF16 = jnp.bfloat16

D_MODEL = 1024
D_ATTN = 512
D_LRU = 512
KV_W = 128
HEAD_DIM = 64
N_HEADS = 8
N_KV = 2
KV_GROUP = 4
LOG2E = 1.4426950408889634
QK_W = D_ATTN + KV_W
D_IN = D_ATTN + 2 * KV_W + 2 * D_LRU
D_FF = 2816
LRU_C = 8.0
GRID_W = 64
ROPE_THETA = 10000.0
EPS = 1e-6
SUBLANES = 8
LANES = 128
BF16_ROWS = 16
V_ROWS = HEAD_DIM + BF16_ROWS
K_COLS = LANES
MAX_SHIFTED_WEIGHT = 2.0 ** 60
FF_CHUNK = 256
N_SUB = 2
LRU_SLAB = 256
VMEM_LIMIT = 56 * 1024 * 1024


def _dot(a, b):
    return jnp.dot(a, b, preferred_element_type=F32)


def _sigmoid(z):
    return 0.5 * jnp.tanh(0.5 * z) + 0.5


def _rms(x):
    return x * lax.rsqrt(jnp.mean(x * x, axis=-1, keepdims=True) + EPS)


def _shift_down(g):
    rows = lax.broadcasted_iota(jnp.int32, g.shape, 0)
    return jnp.where(rows == 0, 0.0, pltpu.roll(g, 1, 0))


def _shift_up(g):
    rows = lax.broadcasted_iota(jnp.int32, g.shape, 0)
    return jnp.where(rows == SUBLANES - 1, 0.0, pltpu.roll(g, SUBLANES - 1, 0))


def _params(*sem):
    return pltpu.CompilerParams(dimension_semantics=sem, vmem_limit_bytes=VMEM_LIMIT)


def _resident(shape):
    nd = len(shape)
    return pl.BlockSpec(shape, lambda *_: (0,) * nd, pipeline_mode=pl.Buffered(1))


def _mod_kernel(c_ref, w_ref, b_ref, o_ref):
    c = c_ref[...]
    s = (c * _sigmoid(c)).astype(BF16)
    o_ref[...] = _dot(s, w_ref[...].astype(BF16)) + b_ref[...]


def _modulation(cvec, w_ada, b_ada, tn=1536):
    depth, d, n = w_ada.shape
    rows = cvec.shape[0]
    return pl.pallas_call(
        _mod_kernel,
        grid=(depth, n // tn),
        in_specs=[pl.BlockSpec((rows, d), lambda l, j: (0, 0)),
                  pl.BlockSpec((None, d, tn), lambda l, j: (l, 0, j)),
                  pl.BlockSpec((None, 1, tn), lambda l, j: (l, 0, j))],
        out_specs=pl.BlockSpec((None, rows, tn), lambda l, j: (l, 0, j)),
        out_shape=jax.ShapeDtypeStruct((depth, rows, n), F32),
        compiler_params=_params("arbitrary", "arbitrary"),
        name="adaln_mod",
    )(cvec, w_ada, b_ada.reshape(depth, 1, n))


def _inproj_kernel(x_ref, mod_ref, w_ref, bd_ref, nw_ref, cos_ref, sin_ref,
                   qt_ref, k_ref, vt_ref, xl_ref, g_ref, *, use_rope):
    shift = mod_ref[:, 0:D_MODEL]
    scale = mod_ref[:, D_MODEL:2 * D_MODEL]
    tm = x_ref.shape[0]
    sub = tm // N_SUB

    def project(r0):
        h = _rms(x_ref[pl.ds(r0, sub), :]) * (1.0 + scale) + shift
        y = _dot(h.astype(BF16), w_ref[...])
        qk = y[:, 0:QK_W]
        return y, _dot((qk * qk).astype(BF16), bd_ref[...])

    def finish(r0, y_ss):
        y, ss = y_ss
        rows = pl.ds(r0, sub)
        qk = y[:, 0:QK_W]
        qk = qk * lax.rsqrt(ss * (1.0 / HEAD_DIM) + EPS) * nw_ref[...]
        if use_rope:
            lane = lax.broadcasted_iota(jnp.int32, (sub, LANES), 1)
            first_half = (lane % HEAD_DIM) < (HEAD_DIM // 2)
            cos = cos_ref[rows, :]
            sin = sin_ref[rows, :]
            cols = []
            for k in range(QK_W // LANES):
                xc = qk[:, LANES * k:LANES * (k + 1)]
                other = jnp.where(first_half, pltpu.roll(xc, LANES - HEAD_DIM // 2, 1),
                                  pltpu.roll(xc, HEAD_DIM // 2, 1))
                cols.append(xc * cos + other * sin)
            qk = jnp.concatenate(cols, axis=1)
        qt = (qk[:, 0:D_ATTN] * (LOG2E * HEAD_DIM ** -0.5)).T
        for h in range(N_HEADS):
            qt_ref[h, :, rows] = qt[HEAD_DIM * h:HEAD_DIM * (h + 1)].astype(BF16)
        kk = qk[:, D_ATTN:QK_W]
        lane = lax.broadcasted_iota(jnp.int32, kk.shape, 1)
        one_hot = jnp.where(lane == HEAD_DIM, 1.0, 0.0)
        for g in range(N_KV):
            kg = kk if g == 0 else pltpu.roll(kk, HEAD_DIM, 1)
            k_ref[g, rows, :] = jnp.where(lane < HEAD_DIM, kg, one_hot).astype(BF16)
        vt = y[:, QK_W:QK_W + KV_W].T.astype(BF16)
        ones = jnp.ones((V_ROWS - HEAD_DIM, sub), BF16)
        for g in range(N_KV):
            vt_ref[g, :, rows] = jnp.concatenate([vt[HEAD_DIM * g:HEAD_DIM * (g + 1)], ones], axis=0)
        xl_ref[rows, :] = y[:, QK_W + KV_W:QK_W + KV_W + D_LRU]
        g_ref[rows, :] = y[:, QK_W + KV_W + D_LRU:D_IN]

    y = project(0)
    for s in range(N_SUB):
        y_next = project((s + 1) * sub) if s + 1 < N_SUB else None
        finish(s * sub, y)
        y = y_next


def _inproj(x, mod, w_in, bd, nw, cos, sin, *, n_seq, seq_len, shared_mod, use_rope):
    tm = min(512, seq_len)
    tps = seq_len // tm
    rows = n_seq * seq_len
    mod_idx = (lambda i: (0, 0, 0)) if shared_mod else (lambda i: (i // tps, 0, 0))
    kern = functools.partial(_inproj_kernel, use_rope=use_rope)
    return pl.pallas_call(
        kern,
        grid=(rows // tm,),
        in_specs=[pl.BlockSpec((tm, D_MODEL), lambda i: (i, 0)),
                  pl.BlockSpec((None, 1, 2 * D_MODEL), mod_idx),
                  _resident((D_MODEL, D_IN)),
                  _resident((QK_W, QK_W)),
                  _resident((1, QK_W)),
                  pl.BlockSpec((tm, LANES), lambda i: (i % tps, 0)),
                  pl.BlockSpec((tm, LANES), lambda i: (i % tps, 0))],
        out_specs=[pl.BlockSpec((None, N_HEADS, HEAD_DIM, tm), lambda i: (i // tps, 0, 0, i % tps)),
                   pl.BlockSpec((None, N_KV, tm, K_COLS), lambda i: (i // tps, 0, i % tps, 0)),
                   pl.BlockSpec((None, N_KV, V_ROWS, tm), lambda i: (i // tps, 0, 0, i % tps)),
                   pl.BlockSpec((tm, D_LRU), lambda i: (i, 0)),
                   pl.BlockSpec((tm, D_LRU), lambda i: (i, 0))],
        out_shape=[jax.ShapeDtypeStruct((n_seq, N_HEADS, HEAD_DIM, seq_len), BF16),
                   jax.ShapeDtypeStruct((n_seq, N_KV, seq_len, K_COLS), BF16),
                   jax.ShapeDtypeStruct((n_seq, N_KV, V_ROWS, seq_len), BF16),
                   jax.ShapeDtypeStruct((rows, D_LRU), F32),
                   jax.ShapeDtypeStruct((rows, D_LRU), F32)],
        compiler_params=_params("arbitrary"),
        name="in_proj_rope" if use_rope else "in_proj",
    )(x, mod, w_in, bd, nw, cos, sin)


def _attn_kernel(*refs, tq, tk, n_lat_chunks):
    if n_lat_chunks:
        qt_ref, kc_ref, vtc_ref, kl_ref, vtl_ref, o_ref = refs
    else:
        qt_ref, kc_ref, vtc_ref, o_ref = refs
    n = KV_GROUP * tq

    def queries(g):
        return jnp.concatenate([qt_ref[KV_GROUP * g + h] for h in range(KV_GROUP)], axis=1)

    def ctx_scores(g, qt):
        return _dot(kc_ref[g, :, 0:HEAD_DIM], qt)

    def finish(g, acc):
        ot = acc[0:HEAD_DIM] * (1.0 / acc[HEAD_DIM:HEAD_DIM + 1])
        for k in range(KV_GROUP):
            h = KV_GROUP * g + k
            o_ref[:, HEAD_DIM * h:HEAD_DIM * (h + 1)] = ot[:, k * tq:(k + 1) * tq].T

    def exact(g):
        qt = queries(g)
        s = ctx_scores(g, qt)
        m = jnp.max(s, axis=0, keepdims=True)
        acc = _dot(vtc_ref[g], jnp.exp2(s - m).astype(BF16))

        def body(c, carry):
            m, acc = carry
            off = pl.multiple_of(c * tk, tk)
            s = _dot(kl_ref[g, pl.ds(off, tk), 0:HEAD_DIM], qt)
            m_new = jnp.maximum(m, jnp.max(s, axis=0, keepdims=True))
            p = jnp.exp2(s - m_new).astype(BF16)
            return m_new, jnp.exp2(m - m_new) * acc + _dot(vtl_ref[g, :, pl.ds(off, tk)], p)

        if n_lat_chunks:
            m, acc = lax.fori_loop(0, n_lat_chunks, body, (m, acc))
        return acc

    if not n_lat_chunks:
        for g in range(N_KV):
            finish(g, exact(g))
        return

    def shifted(g):
        qt = queries(g)
        s = ctx_scores(g, qt)
        running = jnp.max(s, axis=0, keepdims=True)
        shift = running.astype(BF16).astype(F32)
        shifts = [shift, shift]
        acc = _dot(vtc_ref[g], jnp.exp2(s - shift).astype(BF16))
        row = lax.broadcasted_iota(jnp.int32, (BF16_ROWS, n), 0)
        pad = jnp.zeros((K_COLS - HEAD_DIM - BF16_ROWS, n), BF16)
        largest = jnp.ones((1, n), F32)
        for c in range(n_lat_chunks):
            new_shift = shifts.pop(0)
            shift_rows = jnp.where(row == 0, -new_shift, 0.0).astype(BF16)
            qa = jnp.concatenate([qt, shift_rows, pad], axis=0)
            p = jnp.exp2(_dot(kl_ref[g, pl.ds(c * tk, tk), :], qa))
            pv = _dot(vtl_ref[g, :, pl.ds(c * tk, tk)], p.astype(BF16))
            acc = acc * jnp.exp2(shift - new_shift) + pv
            shift = new_shift
            pmax = jnp.max(p, axis=0, keepdims=True)
            largest = jnp.maximum(largest, pmax)
            running = jnp.maximum(running, shift + jnp.log2(pmax))
            shifts.append(running.astype(BF16).astype(F32))
        return acc, largest

    acc0, big0 = shifted(0)
    acc1, big1 = shifted(1)
    overflow = jnp.max(jnp.maximum(big0, big1)) > MAX_SHIFTED_WEIGHT

    @pl.when(overflow)
    def _():
        for g in range(N_KV):
            finish(g, exact(g))

    @pl.when(jnp.logical_not(overflow))
    def _():
        finish(0, acc0)
        finish(1, acc1)


def _attention(qt, k_ctx, vt_ctx, k_lat, vt_lat, *, n_seq, q_len, ctx_len, lat_len):
    tq = min(256, q_len)
    tk = min(512, lat_len // 2) if lat_len else 0
    n_lat_chunks = lat_len // tk if lat_len else 0
    assert n_lat_chunks % 2 == 0
    nq = q_len // tq
    kern = functools.partial(_attn_kernel, tq=tq, tk=tk, n_lat_chunks=n_lat_chunks)
    in_specs = [pl.BlockSpec((None, N_HEADS, HEAD_DIM, tq), lambda b, i: (b, 0, 0, i)),
                pl.BlockSpec((None, N_KV, ctx_len, K_COLS), lambda b, i: (b, 0, 0, 0)),
                pl.BlockSpec((None, N_KV, V_ROWS, ctx_len), lambda b, i: (b, 0, 0, 0))]
    args = [qt, k_ctx, vt_ctx]
    scratch = []
    if lat_len:
        in_specs += [pl.BlockSpec((None, N_KV, lat_len, K_COLS), lambda b, i: (b, 0, 0, 0)),
                     pl.BlockSpec((None, N_KV, V_ROWS, lat_len), lambda b, i: (b, 0, 0, 0))]
        args += [k_lat, vt_lat]
    return pl.pallas_call(
        kern,
        grid=(n_seq, nq),
        in_specs=in_specs,
        out_specs=pl.BlockSpec((tq, D_ATTN), lambda b, i: (b * nq + i, 0)),
        out_shape=jax.ShapeDtypeStruct((n_seq * q_len, D_ATTN), F32),
        scratch_shapes=scratch,
        compiler_params=_params("arbitrary", "arbitrary"),
        name="attn_latent" if lat_len else "attn_context",
    )(*args)


def _lru_sequence(x_ref, g_ref, o_ref, h0, consts, scr, *, n_groups):
    cw_ref, cb_ref, wg_ref, bg_ref, hnsp = consts
    xe_ref, af_ref, bf_ref, ar_ref, br_ref = scr
    rows = SUBLANES * n_groups
    width = x_ref.shape[1]
    chunk = min(256, rows)
    n_chunks = rows // chunk

    xe_ref[pl.ds(2 * SUBLANES, rows), :] = x_ref[...]
    xe_ref[pl.ds(0, SUBLANES), :] = _shift_down(x_ref[pl.ds(rows - 2 * SUBLANES, SUBLANES), :])
    xe_ref[pl.ds(SUBLANES, SUBLANES), :] = _shift_down(x_ref[pl.ds(rows - SUBLANES, SUBLANES), :])
    xe_ref[pl.ds(rows + 2 * SUBLANES, SUBLANES), :] = _shift_up(x_ref[pl.ds(0, SUBLANES), :])

    def gates(c, _):
        r0 = pl.multiple_of(c * chunk, chunk)
        xc = cb_ref[...]
        for k in range(4):
            xc = xc + cw_ref[k:k + 1, :] * xe_ref[pl.ds(r0 + SUBLANES * k, chunk), :]
        t = jnp.tanh(_dot(xc.astype(BF16), wg_ref[...]) + bg_ref[...])
        hx = 0.5 * xc
        for d, (a_ref, b_ref) in enumerate(((af_ref, bf_ref), (ar_ref, br_ref))):
            t_r = t[:, (2 * d) * width:(2 * d + 1) * width]
            t_i = t[:, (2 * d + 1) * width:(2 * d + 2) * width]
            log_a = hnsp[d] * t_r + hnsp[d]
            a = jnp.exp(log_a)
            b = jnp.sqrt(jnp.tanh(-log_a) * (1.0 + a * a)) * (hx * t_i + hx)
            a_ref[pl.ds(r0, chunk), :] = a
            b_ref[pl.ds(r0, chunk), :] = b
        return 0

    lax.fori_loop(0, n_chunks, gates, 0)

    def scan(j, carry):
        hf, pf, hr, pr = carry
        rf = pl.multiple_of(j * SUBLANES, SUBLANES)
        rr = pl.multiple_of((n_groups - 1 - j) * SUBLANES, SUBLANES)
        a = af_ref[pl.ds(rf, SUBLANES), :]
        hf = a * hf + bf_ref[pl.ds(rf, SUBLANES), :]
        pf = a * pf
        bf_ref[pl.ds(rf, SUBLANES), :] = hf
        af_ref[pl.ds(rf, SUBLANES), :] = pf
        a = ar_ref[pl.ds(rr, SUBLANES), :]
        hr = a * hr + br_ref[pl.ds(rr, SUBLANES), :]
        pr = a * pr
        br_ref[pl.ds(rr, SUBLANES), :] = hr
        ar_ref[pl.ds(rr, SUBLANES), :] = pr
        return hf, pf, hr, pr

    zero = jnp.zeros((SUBLANES, width), F32)
    one = jnp.ones((SUBLANES, width), F32)
    hf, pf, hr, pr = lax.fori_loop(0, n_groups, scan, (zero, one, zero, one), unroll=4)

    e = h0[0]
    init_f = []
    for s in range(SUBLANES):
        init_f.append(e)
        e = hf[s:s + 1, :] + pf[s:s + 1, :] * e
    fin_f = e
    e = h0[1]
    init_r = [None] * SUBLANES
    for s in range(SUBLANES - 1, -1, -1):
        init_r[s] = e
        e = hr[s:s + 1, :] + pr[s:s + 1, :] * e
    fin_r = e

    if o_ref is not None:
        init_f = jnp.concatenate(init_f, axis=0)[None]
        init_r = jnp.concatenate(init_r, axis=0)[None]

        def combine(c, _):
            r0 = pl.multiple_of(c * chunk, chunk)
            sl = pl.ds(r0, chunk)
            shp = (chunk // SUBLANES, SUBLANES, width)
            h = (bf_ref[sl, :].reshape(shp) + af_ref[sl, :].reshape(shp) * init_f
                 + br_ref[sl, :].reshape(shp) + ar_ref[sl, :].reshape(shp) * init_r)
            gt = g_ref[sl, :]
            gelu = 0.5 * gt * (1.0 + jnp.tanh(0.7978845608028654 * (gt + 0.044715 * gt * gt * gt)))
            o_ref[sl, :] = h.reshape(chunk, width) * gelu
            return 0

        lax.fori_loop(0, n_chunks, combine, 0)
    return fin_f, fin_r


def _lru_kernel(*refs, ctx_groups, lat_groups, ctx_out):
    xc_ref, gc_ref, xl_ref, gl_ref, cw_ref, cb_ref, wg_ref, bg_ref, lam_ref = refs[:9]
    n_out = 2 if ctx_out else 1
    outs = refs[9:9 + n_out]
    scr = refs[9 + n_out:]
    z = -lam_ref[...]
    softplus = jnp.maximum(z, 0.0) + jnp.log1p(jnp.exp(-jnp.abs(z)))
    hnsp = [(-0.5 * LRU_C) * softplus[d:d + 1, :] for d in range(2)]
    consts = (cw_ref, cb_ref, wg_ref, bg_ref, hnsp)
    zero = jnp.zeros((1, xc_ref.shape[1]), F32)
    fin = _lru_sequence(xc_ref, gc_ref, outs[1] if ctx_out else None, (zero, zero), consts, scr,
                        n_groups=ctx_groups)
    _lru_sequence(xl_ref, gl_ref, outs[0], fin, consts, scr, n_groups=lat_groups)


def _lru(x_ctx, g_ctx, x_lat, g_lat, conv_w, conv_b, wg, bg, lam, *, n_seq, ctx_len, lat_len, ctx_out):
    w = LRU_SLAB
    n_slab = D_LRU // w
    kern = functools.partial(_lru_kernel, ctx_groups=ctx_len // SUBLANES,
                             lat_groups=lat_len // SUBLANES, ctx_out=ctx_out)
    ctx_spec = pl.BlockSpec((ctx_len, w), lambda b, c: (b, c))
    lat_spec = pl.BlockSpec((lat_len, w), lambda b, c: (b, c))
    out_specs = [lat_spec]
    out_shape = [jax.ShapeDtypeStruct((n_seq * lat_len, D_LRU), F32)]
    if ctx_out:
        out_specs.append(ctx_spec)
        out_shape.append(jax.ShapeDtypeStruct((n_seq * ctx_len, D_LRU), F32))
    max_rows = max(ctx_len, lat_len)
    return pl.pallas_call(
        kern,
        grid=(n_seq, n_slab),
        in_specs=[ctx_spec, ctx_spec, lat_spec, lat_spec,
                  pl.BlockSpec((4, w), lambda b, c: (0, c)),
                  pl.BlockSpec((1, w), lambda b, c: (0, c)),
                  pl.BlockSpec((None, w, 4 * w), lambda b, c: (c, 0, 0)),
                  pl.BlockSpec((None, 1, 4 * w), lambda b, c: (c, 0, 0)),
                  pl.BlockSpec((2, w), lambda b, c: (0, c))],
        out_specs=out_specs,
        out_shape=out_shape,
        scratch_shapes=[pltpu.VMEM((max_rows + 3 * SUBLANES, w), F32)]
                       + [pltpu.VMEM((max_rows, w), F32)] * 4,
        compiler_params=_params("arbitrary", "arbitrary"),
        name="rglru_bidir",
    )(x_ctx, g_ctx, x_lat, g_lat, conv_w, conv_b, wg, bg, lam)


def _outproj_kernel(a_ref, r_ref, x_ref, gate_ref, nw_ref, w_ref, o_ref):
    sub = x_ref.shape[0] // N_SUB

    def project(r0):
        rows = pl.ds(r0, sub)
        cat = jnp.concatenate([_rms(a_ref[rows, :]), _rms(r_ref[rows, :])], axis=1) * nw_ref[...]
        return _dot(cat.astype(BF16), w_ref[...])

    y = project(0)
    for s in range(N_SUB):
        y_next = project((s + 1) * sub) if s + 1 < N_SUB else None
        rows = pl.ds(s * sub, sub)
        o_ref[rows, :] = x_ref[rows, :] + gate_ref[...] * y
        y = y_next


def _outproj(a, r, x, mod, nw, w_out, *, n_seq, seq_len, shared_mod):
    tm = min(512, seq_len)
    tps = seq_len // tm
    rows = n_seq * seq_len
    gate_idx = (lambda i: (0, 0, 2)) if shared_mod else (lambda i: (i // tps, 0, 2))
    return pl.pallas_call(
        _outproj_kernel,
        grid=(rows // tm,),
        in_specs=[pl.BlockSpec((tm, D_ATTN), lambda i: (i, 0)),
                  pl.BlockSpec((tm, D_LRU), lambda i: (i, 0)),
                  pl.BlockSpec((tm, D_MODEL), lambda i: (i, 0)),
                  pl.BlockSpec((None, 1, D_MODEL), gate_idx),
                  _resident((1, D_MODEL)),
                  _resident((D_MODEL, D_MODEL))],
        out_specs=pl.BlockSpec((tm, D_MODEL), lambda i: (i, 0)),
        out_shape=jax.ShapeDtypeStruct((rows, D_MODEL), F32),
        compiler_params=_params("arbitrary"),
        name="out_proj",
    )(a, r, x, mod, nw, w_out)


def _ffn_kernel(*refs, tm, tps, final_norm):
    x_ref, xp_ref, xn_ref, mod_ref, wup_ref, cw_ref, cb_ref, wdn_ref = refs[:8]
    if final_norm:
        fnw_ref, o_ref, hext_ref, act_ref, ua_ref, ub_ref = refs[8:]
    else:
        o_ref, hext_ref, act_ref, ua_ref, ub_ref = refs[8:]
    assert (D_FF // FF_CHUNK) % 2 == 1
    jt = pl.program_id(0) % tps
    shift = mod_ref[:, 0:D_MODEL]
    scale = mod_ref[:, D_MODEL:2 * D_MODEL]
    gate = mod_ref[:, 2 * D_MODEL:3 * D_MODEL]

    def norm_mod(x):
        return _rms(x) * (1.0 + scale) + shift

    x = x_ref[...]
    hext_ref[pl.ds(BF16_ROWS, tm), :] = norm_mod(x).astype(BF16)
    hp = norm_mod(xp_ref[...])
    hp = jnp.where(jt == 0, _shift_down(hp), hp)
    hn = norm_mod(xn_ref[...])
    hn = jnp.where(jt == tps - 1, _shift_up(hn), hn)
    zeros = jnp.zeros_like(hp)
    hext_ref[pl.ds(0, BF16_ROWS), :] = jnp.concatenate([zeros, hp], axis=0).astype(BF16)
    hext_ref[pl.ds(BF16_ROWS + tm, BF16_ROWS), :] = jnp.concatenate([hn, zeros], axis=0).astype(BF16)

    def up(c, u_ref):
        for k in range(2):
            off = pl.multiple_of(k * D_FF + c * FF_CHUNK, FF_CHUNK)
            u_ref[k] = _dot(hext_ref[...], wup_ref[:, pl.ds(off, FF_CHUNK)])

    def conv(u_ref, k, off):
        sl = pl.ds(off, FF_CHUNK)
        return (cw_ref[0:1, sl] * u_ref[k, pl.ds(SUBLANES, tm), :]
                + cw_ref[1:2, sl] * u_ref[k, pl.ds(2 * SUBLANES, tm), :]
                + cw_ref[2:3, sl] * u_ref[k, pl.ds(3 * SUBLANES, tm), :]
                + cb_ref[:, sl])

    def activate(c, u_ref):
        og = pl.multiple_of(c * FF_CHUNK, FF_CHUNK)
        ov = pl.multiple_of(D_FF + c * FF_CHUNK, FF_CHUNK)
        yg = conv(u_ref, 0, og)
        yv = conv(u_ref, 1, ov)
        act_ref[:, pl.ds(og, FF_CHUNK)] = (yg * _sigmoid(yg) * yv).astype(BF16)

    n_chunks = D_FF // FF_CHUNK
    up(0, ua_ref)

    def pair(j, _):
        up(2 * j + 1, ub_ref)
        activate(2 * j, ua_ref)
        up(2 * j + 2, ua_ref)
        activate(2 * j + 1, ub_ref)
        return 0

    lax.fori_loop(0, (n_chunks - 1) // 2, pair, 0)
    activate(n_chunks - 1, ua_ref)
    out = x + gate * _dot(act_ref[...], wdn_ref[...])
    if final_norm:
        out = _rms(out) * fnw_ref[...]
    o_ref[...] = out


def _ffn(x, mod, w_up, conv_w, conv_b, w_down, final_w, *, n_seq, seq_len, shared_mod):
    tm = min(512, seq_len)
    tps = seq_len // tm
    rows = n_seq * seq_len
    gps = seq_len // SUBLANES
    gpt = tm // SUBLANES
    mod_idx = (lambda i: (0, 0, 1)) if shared_mod else (lambda i: (i // tps, 0, 1))

    def prev_idx(i):
        jt = i % tps
        return ((i // tps) * gps + jnp.where(jt == 0, gps - 1, jt * gpt - 1), 0)

    def next_idx(i):
        jt = i % tps
        return ((i // tps) * gps + jnp.where(jt == tps - 1, 0, (jt + 1) * gpt), 0)

    final_norm = final_w is not None
    in_specs = [pl.BlockSpec((tm, D_MODEL), lambda i: (i, 0)),
                pl.BlockSpec((SUBLANES, D_MODEL), prev_idx),
                pl.BlockSpec((SUBLANES, D_MODEL), next_idx),
                pl.BlockSpec((None, 1, 3 * D_MODEL), mod_idx),
                _resident((D_MODEL, 2 * D_FF)),
                _resident((3, 2 * D_FF)),
                _resident((1, 2 * D_FF)),
                _resident((D_FF, D_MODEL))]
    args = [x, x, x, mod, w_up, conv_w, conv_b, w_down]
    if final_norm:
        in_specs.append(_resident((1, D_MODEL)))
        args.append(final_w)
    kern = functools.partial(_ffn_kernel, tm=tm, tps=tps, final_norm=final_norm)
    return pl.pallas_call(
        kern,
        grid=(rows // tm,),
        in_specs=in_specs,
        out_specs=pl.BlockSpec((tm, D_MODEL), lambda i: (i, 0)),
        out_shape=jax.ShapeDtypeStruct((rows, D_MODEL), F32),
        scratch_shapes=[pltpu.VMEM((tm + 2 * BF16_ROWS, D_MODEL), BF16),
                        pltpu.VMEM((tm, D_FF), BF16)]
                       + [pltpu.VMEM((2, tm + 2 * BF16_ROWS, FF_CHUNK), F32)] * 2,
        compiler_params=_params("arbitrary"),
        name="conv_ffn_final" if final_norm else "conv_ffn",
    )(*args)


def _to_segments(x):
    b, t, c = x.shape
    return x.reshape(b, SUBLANES, t // SUBLANES, c).transpose(0, 2, 1, 3).reshape(b * t, c)


def _from_segments(y, b, t):
    return y.reshape(b, t // SUBLANES, SUBLANES, -1).transpose(0, 2, 1, 3).reshape(b, t, -1)


def _rope_tables(seq_len):
    r = jnp.arange(seq_len)
    t = (r % SUBLANES) * (seq_len // SUBLANES) + r // SUBLANES
    row = (t // GRID_W).astype(F32)
    col = (t % GRID_W).astype(F32)
    pairs = HEAD_DIM // 4
    inv = ROPE_THETA ** (-jnp.arange(pairs, dtype=F32) / pairs)
    ang = jnp.concatenate([row[:, None] * inv, col[:, None] * inv], axis=-1)
    cos, sin = jnp.cos(ang), jnp.sin(ang)
    return (jnp.concatenate([cos, cos, cos, cos], axis=-1),
            jnp.concatenate([-sin, sin, -sin, sin], axis=-1))


def _gate_weights(wa, wx, ba, bx):
    n_slab = D_LRU // LRU_SLAB
    per = LRU_SLAB // wa.shape[-1]

    def slab_diag(w):
        w = w.reshape(n_slab, per, w.shape[-2], w.shape[-1])
        eye = jnp.eye(per, dtype=w.dtype)
        return jnp.einsum('spcd,pq->spcqd', w, eye).reshape(n_slab, LRU_SLAB, LRU_SLAB)

    wg = jnp.concatenate([slab_diag(wa[0]), slab_diag(wx[0]), slab_diag(wa[1]), slab_diag(wx[1])],
                         axis=-1)
    bg = jnp.stack([ba[0], bx[0], ba[1], bx[1]], axis=0).reshape(4, n_slab, LRU_SLAB)
    bg = bg.transpose(1, 0, 2).reshape(n_slab, 1, 4 * LRU_SLAB)
    return (0.5 * wg).astype(BF16), 0.5 * bg


def kernel(x, c, ctx, c_ctx, w_ada, b_ada, w_in, q_norm_w, k_norm_w, lru_conv_w, lru_conv_b, lru_wa,
           lru_ba, lru_wx, lru_bx, lru_lambda, attn_out_norm_w, lru_out_norm_w, w_out, ffn_w_up,
           ffn_conv_w, ffn_conv_b, ffn_w_down, final_norm_w):
    batch, seq, _ = x.shape
    ctx_len = ctx.shape[1]
    depth = w_in.shape[0]

    mod_rows = -(-(batch + 1) // SUBLANES) * SUBLANES
    cvec = jnp.zeros((mod_rows, D_MODEL), F32).at[:batch].set(c).at[batch].set(c_ctx)
    mods = _modulation(cvec, w_ada, b_ada)

    cos, sin = _rope_tables(seq)
    ones = jnp.ones((ctx_len, LANES), F32)
    bd = jnp.kron(jnp.eye(QK_W // HEAD_DIM, dtype=F32), jnp.ones((HEAD_DIM, HEAD_DIM), F32)).astype(BF16)

    x_lat = _to_segments(x)
    x_ctx = _to_segments(ctx)
    lat = dict(n_seq=batch, seq_len=seq, shared_mod=False)
    cx = dict(n_seq=batch, seq_len=ctx_len, shared_mod=True)

    for l in range(depth):
        ctx_out = l < depth - 1
        last = l == depth - 1
        mod_lat = mods[l, :batch].reshape(batch, 1, 6 * D_MODEL)
        mod_ctx = mods[l, batch:batch + 1].reshape(1, 1, 6 * D_MODEL)
        w_in_l = w_in[l].astype(BF16)
        nw_qk = jnp.concatenate([jnp.tile(q_norm_w[l], D_ATTN // HEAD_DIM),
                                 jnp.tile(k_norm_w[l], KV_W // HEAD_DIM)]).reshape(1, QK_W)
        q_l, kt_l, v_l, xr_l, g_l = _inproj(x_lat, mod_lat, w_in_l, bd, nw_qk, cos, sin,
                                            use_rope=True, **lat)
        q_c, kt_c, v_c, xr_c, g_c = _inproj(x_ctx, mod_ctx, w_in_l, bd, nw_qk, ones, ones,
                                            use_rope=False, **cx)

        a_l = _attention(q_l, kt_c, v_c, kt_l, v_l, n_seq=batch, q_len=seq, ctx_len=ctx_len,
                         lat_len=seq)
        wg, bg = _gate_weights(lru_wa[l], lru_wx[l], lru_ba[l], lru_bx[l])
        r = _lru(xr_c, g_c, xr_l, g_l, lru_conv_w[l], lru_conv_b[l].reshape(1, D_LRU), wg, bg,
                 lru_lambda[l], n_seq=batch, ctx_len=ctx_len, lat_len=seq, ctx_out=ctx_out)

        nw_out = jnp.concatenate([attn_out_norm_w[l], lru_out_norm_w[l]]).reshape(1, D_MODEL)
        w_out_l = w_out[l].astype(BF16)
        w_up_l = ffn_w_up[l].astype(BF16)
        w_dn_l = ffn_w_down[l].astype(BF16)
        cb_l = ffn_conv_b[l].reshape(1, 2 * D_FF)
        x_lat = _outproj(a_l, r[0], x_lat, mod_lat, nw_out, w_out_l, **lat)
        x_lat = _ffn(x_lat, mod_lat, w_up_l, ffn_conv_w[l], cb_l, w_dn_l,
                     final_norm_w.reshape(1, D_MODEL) if last else None, **lat)
        if ctx_out:
            a_c = _attention(q_c, kt_c, v_c, None, None, n_seq=batch, q_len=ctx_len,
                             ctx_len=ctx_len, lat_len=0)
            x_ctx = _outproj(a_c, r[1], x_ctx, mod_ctx, nw_out, w_out_l, **cx)
            x_ctx = _ffn(x_ctx, mod_ctx, w_up_l, ffn_conv_w[l], cb_l, w_dn_l, None, **cx)
    return _from_segments(x_lat, batch, seq)
```

```python
import functools

import jax
import jax.numpy as jnp
from jax import lax
from jax.experimental import pallas as pl
from jax.experimental.pallas import tpu as pltpu

F32 = jnp.float32
BF16 = jnp.bfloat16

D_MODEL = 1024
D_ATTN = 512
D_LRU = 512
KV_W = 128
HEAD_DIM = 64
N_HEADS = 8
N_KV = 2
KV_GROUP = 4
LOG2E = 1.4426950408889634
QK_W = D_ATTN + KV_W
D_IN = D_ATTN + 2 * KV_W + 2 * D_LRU
D_FF = 2816
LRU_C = 8.0
GRID_W = 64
ROPE_THETA = 10000.0
EPS = 1e-6
SUBLANES = 8
LANES = 128
BF16_ROWS = 16
V_ROWS = HEAD_DIM + BF16_ROWS
K_COLS = LANES
MAX_SHIFTED_WEIGHT = 2.0 ** 60
FF_CHUNK = 256
SUB_ROWS = 256
LRU_SLAB = 256
VMEM_LIMIT = 56 * 1024 * 1024


def _dot(a, b):
    return jnp.dot(a, b, preferred_element_type=F32)


def _sigmoid(z):
    return 0.5 * jnp.tanh(0.5 * z) + 0.5


def _rms(x):
    return x * lax.rsqrt(jnp.mean(x * x, axis=-1, keepdims=True) + EPS)


def _shift_down(g):
    rows = lax.broadcasted_iota(jnp.int32, g.shape, 0)
    return jnp.where(rows == 0, 0.0, pltpu.roll(g, 1, 0))


def _shift_up(g):
    rows = lax.broadcasted_iota(jnp.int32, g.shape, 0)
    return jnp.where(rows == SUBLANES - 1, 0.0, pltpu.roll(g, SUBLANES - 1, 0))


def _n_sub(tile_rows):
    return max(2, tile_rows // SUB_ROWS)


def _params(*sem):
    return pltpu.CompilerParams(dimension_semantics=sem, vmem_limit_bytes=VMEM_LIMIT)


def _resident(shape, layer=None):
    nd = len(shape)
    if layer is None:
        return pl.BlockSpec(shape, lambda *_: (0,) * nd, pipeline_mode=pl.Buffered(1))
    return pl.BlockSpec((None,) + tuple(shape), lambda *_: (layer,) + (0,) * nd,
                        pipeline_mode=pl.Buffered(1))


def _mod_kernel(c_ref, w_ref, b_ref, o_ref):
    c = c_ref[...]
    s = (c * _sigmoid(c)).astype(BF16)
    o_ref[...] = _dot(s, w_ref[...].astype(BF16)) + b_ref[...]


def _modulation(cvec, w_ada, b_ada, tn=1536):
    depth, d, n = w_ada.shape
    rows = cvec.shape[0]
    return pl.pallas_call(
        _mod_kernel,
        grid=(depth, n // tn),
        in_specs=[pl.BlockSpec((rows, d), lambda l, j: (0, 0)),
                  pl.BlockSpec((None, d, tn), lambda l, j: (l, 0, j)),
                  pl.BlockSpec((None, 1, tn), lambda l, j: (l, 0, j))],
        out_specs=pl.BlockSpec((None, rows, tn), lambda l, j: (l, 0, j)),
        out_shape=jax.ShapeDtypeStruct((depth, rows, n), F32),
        compiler_params=_params("arbitrary", "arbitrary"),
        name="adaln_mod",
    )(cvec, w_ada, b_ada.reshape(depth, 1, n))


def _inproj_kernel(x_ref, mod_ref, w_ref, bd_ref, nw_ref, cos_ref, sin_ref,
                   qt_ref, k_ref, vt_ref, xl_ref, g_ref, *, use_rope):
    shift = mod_ref[:, 0:D_MODEL]
    scale = mod_ref[:, D_MODEL:2 * D_MODEL]
    n_sub = _n_sub(x_ref.shape[0])
    sub = x_ref.shape[0] // n_sub

    def project(r0):
        h = _rms(x_ref[pl.ds(r0, sub), :]) * (1.0 + scale) + shift
        y = _dot(h.astype(BF16), w_ref[...])
        qk = y[:, 0:QK_W]
        return y, _dot((qk * qk).astype(BF16), bd_ref[...])

    def finish(r0, y_ss):
        y, ss = y_ss
        rows = pl.ds(r0, sub)
        qk = y[:, 0:QK_W]
        qk = qk * lax.rsqrt(ss * (1.0 / HEAD_DIM) + EPS) * nw_ref[...]
        if use_rope:
            lane = lax.broadcasted_iota(jnp.int32, (sub, LANES), 1)
            first_half = (lane % HEAD_DIM) < (HEAD_DIM // 2)
            cos = cos_ref[rows, :]
            sin = sin_ref[rows, :]
            cols = []
            for k in range(QK_W // LANES):
                xc = qk[:, LANES * k:LANES * (k + 1)]
                other = jnp.where(first_half, pltpu.roll(xc, LANES - HEAD_DIM // 2, 1),
                                  pltpu.roll(xc, HEAD_DIM // 2, 1))
                cols.append(xc * cos + other * sin)
            qk = jnp.concatenate(cols, axis=1)
        qt = (qk[:, 0:D_ATTN] * (LOG2E * HEAD_DIM ** -0.5)).T
        for h in range(N_HEADS):
            qt_ref[h, :, rows] = qt[HEAD_DIM * h:HEAD_DIM * (h + 1)].astype(BF16)
        kk = qk[:, D_ATTN:QK_W]
        lane = lax.broadcasted_iota(jnp.int32, kk.shape, 1)
        one_hot = jnp.where(lane == HEAD_DIM, 1.0, 0.0)
        for g in range(N_KV):
            kg = kk if g == 0 else pltpu.roll(kk, HEAD_DIM, 1)
            k_ref[g, rows, :] = jnp.where(lane < HEAD_DIM, kg, one_hot).astype(BF16)
        vt = y[:, QK_W:QK_W + KV_W].T.astype(BF16)
        ones = jnp.ones((V_ROWS - HEAD_DIM, sub), BF16)
        for g in range(N_KV):
            vt_ref[g, :, rows] = jnp.concatenate([vt[HEAD_DIM * g:HEAD_DIM * (g + 1)], ones], axis=0)
        xl_ref[rows, :] = y[:, QK_W + KV_W:QK_W + KV_W + D_LRU]
        g_ref[rows, :] = y[:, QK_W + KV_W + D_LRU:D_IN]

    y = project(0)
    for s in range(n_sub):
        y_next = project((s + 1) * sub) if s + 1 < n_sub else None
        finish(s * sub, y)
        y = y_next


def _inproj(x, mod, w_in, bd, nw, cos, sin, *, n_seq, seq_len, shared_mod, mod_row, layer,
            use_rope):
    tm = min(1024, seq_len)
    tps = seq_len // tm
    rows = n_seq * seq_len
    mod_idx = (lambda i: (mod_row, 0, 0)) if shared_mod else (lambda i: (mod_row + i // tps, 0, 0))
    kern = functools.partial(_inproj_kernel, use_rope=use_rope)
    return pl.pallas_call(
        kern,
        grid=(rows // tm,),
        in_specs=[pl.BlockSpec((tm, D_MODEL), lambda i: (i, 0)),
                  pl.BlockSpec((None, 1, 2 * D_MODEL), mod_idx),
                  _resident((D_MODEL, D_IN), layer),
                  _resident((QK_W, QK_W)),
                  _resident((1, QK_W)),
                  pl.BlockSpec((tm, LANES), lambda i: (i % tps, 0)),
                  pl.BlockSpec((tm, LANES), lambda i: (i % tps, 0))],
        out_specs=[pl.BlockSpec((None, N_HEADS, HEAD_DIM, tm), lambda i: (i // tps, 0, 0, i % tps)),
                   pl.BlockSpec((None, N_KV, tm, K_COLS), lambda i: (i // tps, 0, i % tps, 0)),
                   pl.BlockSpec((None, N_KV, V_ROWS, tm), lambda i: (i // tps, 0, 0, i % tps)),
                   pl.BlockSpec((tm, D_LRU), lambda i: (i, 0)),
                   pl.BlockSpec((tm, D_LRU), lambda i: (i, 0))],
        out_shape=[jax.ShapeDtypeStruct((n_seq, N_HEADS, HEAD_DIM, seq_len), BF16),
                   jax.ShapeDtypeStruct((n_seq, N_KV, seq_len, K_COLS), BF16),
                   jax.ShapeDtypeStruct((n_seq, N_KV, V_ROWS, seq_len), BF16),
                   jax.ShapeDtypeStruct((rows, D_LRU), F32),
                   jax.ShapeDtypeStruct((rows, D_LRU), F32)],
        compiler_params=_params("arbitrary"),
        name="in_proj_rope" if use_rope else "in_proj",
    )(x, mod, w_in, bd, nw, cos, sin)


def _attn_kernel(*refs, tq, tk, n_lat_chunks):
    if n_lat_chunks:
        qt_ref, kc_ref, vtc_ref, kl_ref, vtl_ref, o_ref = refs
    else:
        qt_ref, kc_ref, vtc_ref, o_ref = refs
    n = KV_GROUP * tq

    def queries(g):
        return jnp.concatenate([qt_ref[KV_GROUP * g + h] for h in range(KV_GROUP)], axis=1)

    def ctx_scores(g, qt):
        return _dot(kc_ref[g, :, 0:HEAD_DIM], qt)

    def finish(g, acc):
        ot = acc[0:HEAD_DIM] * (1.0 / acc[HEAD_DIM:HEAD_DIM + 1])
        for k in range(KV_GROUP):
            h = KV_GROUP * g + k
            o_ref[:, HEAD_DIM * h:HEAD_DIM * (h + 1)] = ot[:, k * tq:(k + 1) * tq].T

    def exact(g):
        qt = queries(g)
        s = ctx_scores(g, qt)
        m = jnp.max(s, axis=0, keepdims=True)
        acc = _dot(vtc_ref[g], jnp.exp2(s - m).astype(BF16))

        def body(c, carry):
            m, acc = carry
            off = pl.multiple_of(c * tk, tk)
            s = _dot(kl_ref[g, pl.ds(off, tk), 0:HEAD_DIM], qt)
            m_new = jnp.maximum(m, jnp.max(s, axis=0, keepdims=True))
            p = jnp.exp2(s - m_new).astype(BF16)
            return m_new, jnp.exp2(m - m_new) * acc + _dot(vtl_ref[g, :, pl.ds(off, tk)], p)

        if n_lat_chunks:
            m, acc = lax.fori_loop(0, n_lat_chunks, body, (m, acc))
        return acc

    if not n_lat_chunks:
        for g in range(N_KV):
            finish(g, exact(g))
        return

    row = lax.broadcasted_iota(jnp.int32, (BF16_ROWS, n), 0)
    pad = jnp.zeros((K_COLS - HEAD_DIM - BF16_ROWS, n), BF16)

    def start(g):
        qt = queries(g)
        s = ctx_scores(g, qt)
        running = jnp.max(s, axis=0, keepdims=True)
        shift = running.astype(BF16).astype(F32)
        acc = _dot(vtc_ref[g], jnp.exp2(s - shift).astype(BF16))
        return dict(qt=qt, running=running, shift=shift, shifts=[shift, shift], acc=acc,
                    largest=jnp.ones((1, n), F32))

    def step(g, c, st):
        new_shift = st["shifts"].pop(0)
        shift_rows = jnp.where(row == 0, -new_shift, 0.0).astype(BF16)
        qa = jnp.concatenate([st["qt"], shift_rows, pad], axis=0)
        p = jnp.exp2(_dot(kl_ref[g, pl.ds(c * tk, tk), :], qa))
        pv = _dot(vtl_ref[g, :, pl.ds(c * tk, tk)], p.astype(BF16))
        st["acc"] = st["acc"] * jnp.exp2(st["shift"] - new_shift) + pv
        st["shift"] = new_shift
        pmax = jnp.max(p, axis=0, keepdims=True)
        st["largest"] = jnp.maximum(st["largest"], pmax)
        st["running"] = jnp.maximum(st["running"], new_shift + jnp.log2(pmax))
        st["shifts"].append(st["running"].astype(BF16).astype(F32))

    states = [start(g) for g in range(N_KV)]
    for c in range(n_lat_chunks):
        for g in range(N_KV):
            step(g, c, states[g])
    overflow = jnp.max(jnp.maximum(states[0]["largest"], states[1]["largest"])) > MAX_SHIFTED_WEIGHT

    @pl.when(overflow)
    def _():
        for g in range(N_KV):
            finish(g, exact(g))

    @pl.when(jnp.logical_not(overflow))
    def _():
        for g in range(N_KV):
            finish(g, states[g]["acc"])


def _attention(qt, k_ctx, vt_ctx, k_lat, vt_lat, *, n_seq, q_len, ctx_len, lat_len):
    tq = min(256, q_len)
    tk = min(512, lat_len // 2) if lat_len else 0
    n_lat_chunks = lat_len // tk if lat_len else 0
    assert n_lat_chunks % 2 == 0
    nq = q_len // tq
    kern = functools.partial(_attn_kernel, tq=tq, tk=tk, n_lat_chunks=n_lat_chunks)
    in_specs = [pl.BlockSpec((None, N_HEADS, HEAD_DIM, tq), lambda b, i: (b, 0, 0, i)),
                pl.BlockSpec((None, N_KV, ctx_len, K_COLS), lambda b, i: (b, 0, 0, 0)),
                pl.BlockSpec((None, N_KV, V_ROWS, ctx_len), lambda b, i: (b, 0, 0, 0))]
    args = [qt, k_ctx, vt_ctx]
    scratch = []
    if lat_len:
        in_specs += [pl.BlockSpec((None, N_KV, lat_len, K_COLS), lambda b, i: (b, 0, 0, 0)),
                     pl.BlockSpec((None, N_KV, V_ROWS, lat_len), lambda b, i: (b, 0, 0, 0))]
        args += [k_lat, vt_lat]
    return pl.pallas_call(
        kern,
        grid=(n_seq, nq),
        in_specs=in_specs,
        out_specs=pl.BlockSpec((tq, D_ATTN), lambda b, i: (b * nq + i, 0)),
        out_shape=jax.ShapeDtypeStruct((n_seq * q_len, D_ATTN), F32),
        scratch_shapes=scratch,
        compiler_params=_params("arbitrary", "arbitrary"),
        name="attn_latent" if lat_len else "attn_context",
    )(*args)


def _lru_sequence(x_ref, g_ref, o_ref, h0, consts, scr, *, n_groups):
    cw_ref, cb_ref, wg_ref, bg_ref, hnsp = consts
    xe_ref, af_ref, bf_ref, ar_ref, br_ref = scr
    rows = SUBLANES * n_groups
    width = x_ref.shape[1]
    chunk = min(256, rows)
    n_chunks = rows // chunk

    xe_ref[pl.ds(2 * SUBLANES, rows), :] = x_ref[...]
    xe_ref[pl.ds(0, SUBLANES), :] = _shift_down(x_ref[pl.ds(rows - 2 * SUBLANES, SUBLANES), :])
    xe_ref[pl.ds(SUBLANES, SUBLANES), :] = _shift_down(x_ref[pl.ds(rows - SUBLANES, SUBLANES), :])
    xe_ref[pl.ds(rows + 2 * SUBLANES, SUBLANES), :] = _shift_up(x_ref[pl.ds(0, SUBLANES), :])

    def gates(c, _):
        r0 = pl.multiple_of(c * chunk, chunk)
        xc = cb_ref[...]
        for k in range(4):
            xc = xc + cw_ref[k:k + 1, :] * xe_ref[pl.ds(r0 + SUBLANES * k, chunk), :]
        t = jnp.tanh(_dot(xc.astype(BF16), wg_ref[...]) + bg_ref[...])
        hx = 0.5 * xc
        for d, (a_ref, b_ref) in enumerate(((af_ref, bf_ref), (ar_ref, br_ref))):
            t_r = t[:, (2 * d) * width:(2 * d + 1) * width]
            t_i = t[:, (2 * d + 1) * width:(2 * d + 2) * width]
            log_a = hnsp[d] * t_r + hnsp[d]
            a = jnp.exp(log_a)
            y = jnp.tanh(-log_a) * (1.0 + a * a)
            b = jnp.where(y > 0.0, y * lax.rsqrt(y), 0.0) * (hx * t_i + hx)
            a_ref[pl.ds(r0, chunk), :] = a
            b_ref[pl.ds(r0, chunk), :] = b
        return 0

    lax.fori_loop(0, n_chunks, gates, 0)

    def scan(j, carry):
        hf, pf, hr, pr = carry
        rf = pl.multiple_of(j * SUBLANES, SUBLANES)
        rr = pl.multiple_of((n_groups - 1 - j) * SUBLANES, SUBLANES)
        a = af_ref[pl.ds(rf, SUBLANES), :]
        hf = a * hf + bf_ref[pl.ds(rf, SUBLANES), :]
        pf = a * pf
        bf_ref[pl.ds(rf, SUBLANES), :] = hf
        af_ref[pl.ds(rf, SUBLANES), :] = pf
        a = ar_ref[pl.ds(rr, SUBLANES), :]
        hr = a * hr + br_ref[pl.ds(rr, SUBLANES), :]
        pr = a * pr
        br_ref[pl.ds(rr, SUBLANES), :] = hr
        ar_ref[pl.ds(rr, SUBLANES), :] = pr
        return hf, pf, hr, pr

    zero = jnp.zeros((SUBLANES, width), F32)
    one = jnp.ones((SUBLANES, width), F32)
    hf, pf, hr, pr = lax.fori_loop(0, n_groups, scan, (zero, one, zero, one), unroll=4)

    e = h0[0]
    init_f = []
    for s in range(SUBLANES):
        init_f.append(e)
        e = hf[s:s + 1, :] + pf[s:s + 1, :] * e
    fin_f = e
    e = h0[1]
    init_r = [None] * SUBLANES
    for s in range(SUBLANES - 1, -1, -1):
        init_r[s] = e
        e = hr[s:s + 1, :] + pr[s:s + 1, :] * e
    fin_r = e

    if o_ref is not None:
        init_f = jnp.concatenate(init_f, axis=0)[None]
        init_r = jnp.concatenate(init_r, axis=0)[None]

        def combine(c, _):
            r0 = pl.multiple_of(c * chunk, chunk)
            sl = pl.ds(r0, chunk)
            shp = (chunk // SUBLANES, SUBLANES, width)
            h = (bf_ref[sl, :].reshape(shp) + af_ref[sl, :].reshape(shp) * init_f
                 + br_ref[sl, :].reshape(shp) + ar_ref[sl, :].reshape(shp) * init_r)
            gt = g_ref[sl, :]
            gelu = 0.5 * gt * (1.0 + jnp.tanh(0.7978845608028654 * (gt + 0.044715 * gt * gt * gt)))
            o_ref[sl, :] = h.reshape(chunk, width) * gelu
            return 0

        lax.fori_loop(0, n_chunks, combine, 0)
    return fin_f, fin_r


def _lru_kernel(*refs, ctx_groups, lat_groups, ctx_out):
    xc_ref, gc_ref, xl_ref, gl_ref, cw_ref, cb_ref, wg_ref, bg_ref, lam_ref = refs[:9]
    n_out = 2 if ctx_out else 1
    outs = refs[9:9 + n_out]
    scr = refs[9 + n_out:]
    z = -lam_ref[...]
    softplus = jnp.maximum(z, 0.0) + jnp.log1p(jnp.exp(-jnp.abs(z)))
    hnsp = [(-0.5 * LRU_C) * softplus[d:d + 1, :] for d in range(2)]
    consts = (cw_ref, cb_ref, wg_ref, bg_ref, hnsp)
    zero = jnp.zeros((1, xc_ref.shape[1]), F32)
    fin = _lru_sequence(xc_ref, gc_ref, outs[1] if ctx_out else None, (zero, zero), consts, scr,
                        n_groups=ctx_groups)
    _lru_sequence(xl_ref, gl_ref, outs[0], fin, consts, scr, n_groups=lat_groups)


def _lru(x_ctx, g_ctx, x_lat, g_lat, conv_w, conv_b, wg, bg, lam, *, n_seq, ctx_len, lat_len, ctx_out):
    w = LRU_SLAB
    n_slab = D_LRU // w
    kern = functools.partial(_lru_kernel, ctx_groups=ctx_len // SUBLANES,
                             lat_groups=lat_len // SUBLANES, ctx_out=ctx_out)
    ctx_spec = pl.BlockSpec((ctx_len, w), lambda b, c: (b, c))
    lat_spec = pl.BlockSpec((lat_len, w), lambda b, c: (b, c))
    out_specs = [lat_spec]
    out_shape = [jax.ShapeDtypeStruct((n_seq * lat_len, D_LRU), F32)]
    if ctx_out:
        out_specs.append(ctx_spec)
        out_shape.append(jax.ShapeDtypeStruct((n_seq * ctx_len, D_LRU), F32))
    max_rows = max(ctx_len, lat_len)
    return pl.pallas_call(
        kern,
        grid=(n_seq, n_slab),
        in_specs=[ctx_spec, ctx_spec, lat_spec, lat_spec,
                  pl.BlockSpec((4, w), lambda b, c: (0, c)),
                  pl.BlockSpec((1, w), lambda b, c: (0, c)),
                  pl.BlockSpec((None, w, 4 * w), lambda b, c: (c, 0, 0)),
                  pl.BlockSpec((None, 1, 4 * w), lambda b, c: (c, 0, 0)),
                  pl.BlockSpec((2, w), lambda b, c: (0, c))],
        out_specs=out_specs,
        out_shape=out_shape,
        scratch_shapes=[pltpu.VMEM((max_rows + 3 * SUBLANES, w), F32)]
                       + [pltpu.VMEM((max_rows, w), F32)] * 4,
        compiler_params=_params("arbitrary", "arbitrary"),
        name="rglru_bidir",
    )(x_ctx, g_ctx, x_lat, g_lat, conv_w, conv_b, wg, bg, lam)


def _outproj_kernel(a_ref, r_ref, x_ref, gate_ref, nw_ref, w_ref, o_ref):
    n_sub = _n_sub(x_ref.shape[0])
    sub = x_ref.shape[0] // n_sub

    def project(r0):
        rows = pl.ds(r0, sub)
        cat = jnp.concatenate([_rms(a_ref[rows, :]), _rms(r_ref[rows, :])], axis=1) * nw_ref[...]
        return _dot(cat.astype(BF16), w_ref[...])

    y = project(0)
    for s in range(n_sub):
        y_next = project((s + 1) * sub) if s + 1 < n_sub else None
        rows = pl.ds(s * sub, sub)
        o_ref[rows, :] = x_ref[rows, :] + gate_ref[...] * y
        y = y_next


def _outproj(a, r, x, mod, nw, w_out, *, n_seq, seq_len, shared_mod, mod_row, layer):
    tm = min(1024, seq_len)
    tps = seq_len // tm
    rows = n_seq * seq_len
    gate_idx = (lambda i: (mod_row, 0, 2)) if shared_mod else (lambda i: (mod_row + i // tps, 0, 2))
    return pl.pallas_call(
        _outproj_kernel,
        grid=(rows // tm,),
        in_specs=[pl.BlockSpec((tm, D_ATTN), lambda i: (i, 0)),
                  pl.BlockSpec((tm, D_LRU), lambda i: (i, 0)),
                  pl.BlockSpec((tm, D_MODEL), lambda i: (i, 0)),
                  pl.BlockSpec((None, 1, D_MODEL), gate_idx),
                  _resident((1, D_MODEL)),
                  _resident((D_MODEL, D_MODEL), layer)],
        out_specs=pl.BlockSpec((tm, D_MODEL), lambda i: (i, 0)),
        out_shape=jax.ShapeDtypeStruct((rows, D_MODEL), F32),
        compiler_params=_params("arbitrary"),
        name="out_proj",
    )(a, r, x, mod, nw, w_out)


def _ffn_kernel(*refs, tm, tps, final_norm):
    x_ref, xp_ref, xn_ref, mod_ref, wup_ref, cw_ref, cb_ref, wdn_ref = refs[:8]
    if final_norm:
        fnw_ref, o_ref, hext_ref, act_ref, ua_ref, ub_ref = refs[8:]
    else:
        o_ref, hext_ref, act_ref, ua_ref, ub_ref = refs[8:]
    assert (D_FF // FF_CHUNK) % 2 == 1
    jt = pl.program_id(0) % tps
    shift = mod_ref[:, 0:D_MODEL]
    scale = mod_ref[:, D_MODEL:2 * D_MODEL]
    gate = mod_ref[:, 2 * D_MODEL:3 * D_MODEL]

    def norm_mod(x):
        return _rms(x) * (1.0 + scale) + shift

    x = x_ref[...]
    hext_ref[pl.ds(BF16_ROWS, tm), :] = norm_mod(x).astype(BF16)
    hp = norm_mod(xp_ref[...])
    hp = jnp.where(jt == 0, _shift_down(hp), hp)
    hn = norm_mod(xn_ref[...])
    hn = jnp.where(jt == tps - 1, _shift_up(hn), hn)
    zeros = jnp.zeros_like(hp)
    hext_ref[pl.ds(0, BF16_ROWS), :] = jnp.concatenate([zeros, hp], axis=0).astype(BF16)
    hext_ref[pl.ds(BF16_ROWS + tm, BF16_ROWS), :] = jnp.concatenate([hn, zeros], axis=0).astype(BF16)

    def up(c, u_ref):
        for k in range(2):
            off = pl.multiple_of(k * D_FF + c * FF_CHUNK, FF_CHUNK)
            u_ref[k] = _dot(hext_ref[...], wup_ref[:, pl.ds(off, FF_CHUNK)])

    def conv(u_ref, k, off):
        sl = pl.ds(off, FF_CHUNK)
        return (cw_ref[0:1, sl] * u_ref[k, pl.ds(SUBLANES, tm), :]
                + cw_ref[1:2, sl] * u_ref[k, pl.ds(2 * SUBLANES, tm), :]
                + cw_ref[2:3, sl] * u_ref[k, pl.ds(3 * SUBLANES, tm), :]
                + cb_ref[:, sl])

    def activate(c, u_ref):
        og = pl.multiple_of(c * FF_CHUNK, FF_CHUNK)
        ov = pl.multiple_of(D_FF + c * FF_CHUNK, FF_CHUNK)
        yg = conv(u_ref, 0, og)
        yv = conv(u_ref, 1, ov)
        hg = 0.5 * yg
        act_ref[:, pl.ds(og, FF_CHUNK)] = ((hg + hg * jnp.tanh(hg)) * yv).astype(BF16)

    n_chunks = D_FF // FF_CHUNK
    up(0, ua_ref)

    def pair(j, _):
        up(2 * j + 1, ub_ref)
        activate(2 * j, ua_ref)
        up(2 * j + 2, ua_ref)
        activate(2 * j + 1, ub_ref)
        return 0

    lax.fori_loop(0, (n_chunks - 1) // 2, pair, 0)
    activate(n_chunks - 1, ua_ref)
    out = x + gate * _dot(act_ref[...], wdn_ref[...])
    if final_norm:
        out = _rms(out) * fnw_ref[...]
    o_ref[...] = out


def _ffn(x, mod, w_up, conv_w, conv_b, w_down, final_w, *, n_seq, seq_len, shared_mod, mod_row, layer):
    tm = min(1024, seq_len)
    tps = seq_len // tm
    rows = n_seq * seq_len
    gps = seq_len // SUBLANES
    gpt = tm // SUBLANES
    mod_idx = (lambda i: (mod_row, 0, 1)) if shared_mod else (lambda i: (mod_row + i // tps, 0, 1))

    def prev_idx(i):
        jt = i % tps
        return ((i // tps) * gps + jnp.where(jt == 0, gps - 1, jt * gpt - 1), 0)

    def next_idx(i):
        jt = i % tps
        return ((i // tps) * gps + jnp.where(jt == tps - 1, 0, (jt + 1) * gpt), 0)

    final_norm = final_w is not None
    in_specs = [pl.BlockSpec((tm, D_MODEL), lambda i: (i, 0)),
                pl.BlockSpec((SUBLANES, D_MODEL), prev_idx),
                pl.BlockSpec((SUBLANES, D_MODEL), next_idx),
                pl.BlockSpec((None, 1, 3 * D_MODEL), mod_idx),
                _resident((D_MODEL, 2 * D_FF), layer),
                _resident((3, 2 * D_FF), layer),
                _resident((1, 2 * D_FF), layer),
                _resident((D_FF, D_MODEL), layer)]
    args = [x, x, x, mod, w_up, conv_w, conv_b, w_down]
    if final_norm:
        in_specs.append(_resident((1, D_MODEL)))
        args.append(final_w)
    kern = functools.partial(_ffn_kernel, tm=tm, tps=tps, final_norm=final_norm)
    return pl.pallas_call(
        kern,
        grid=(rows // tm,),
        in_specs=in_specs,
        out_specs=pl.BlockSpec((tm, D_MODEL), lambda i: (i, 0)),
        out_shape=jax.ShapeDtypeStruct((rows, D_MODEL), F32),
        scratch_shapes=[pltpu.VMEM((tm + 2 * BF16_ROWS, D_MODEL), BF16),
                        pltpu.VMEM((tm, D_FF), BF16)]
                       + [pltpu.VMEM((2, tm + 2 * BF16_ROWS, FF_CHUNK), F32)] * 2,
        compiler_params=_params("arbitrary"),
        name="conv_ffn_final" if final_norm else "conv_ffn",
    )(*args)


def _to_segments(x):
    b, t, c = x.shape
    return x.reshape(b, SUBLANES, t // SUBLANES, c).transpose(0, 2, 1, 3).reshape(b * t, c)


def _from_segments(y, b, t):
    return y.reshape(b, t // SUBLANES, SUBLANES, -1).transpose(0, 2, 1, 3).reshape(b, t, -1)


def _rope_tables(seq_len):
    r = jnp.arange(seq_len)
    t = (r % SUBLANES) * (seq_len // SUBLANES) + r // SUBLANES
    row = (t // GRID_W).astype(F32)
    col = (t % GRID_W).astype(F32)
    pairs = HEAD_DIM // 4
    inv = ROPE_THETA ** (-jnp.arange(pairs, dtype=F32) / pairs)
    ang = jnp.concatenate([row[:, None] * inv, col[:, None] * inv], axis=-1)
    cos, sin = jnp.cos(ang), jnp.sin(ang)
    return (jnp.concatenate([cos, cos, cos, cos], axis=-1),
            jnp.concatenate([-sin, sin, -sin, sin], axis=-1))


def _gate_weights(wa, wx, ba, bx):
    n_slab = D_LRU // LRU_SLAB
    per = LRU_SLAB // wa.shape[-1]

    def slab_diag(w):
        w = w.reshape(n_slab, per, w.shape[-2], w.shape[-1])
        eye = jnp.eye(per, dtype=w.dtype)
        return jnp.einsum('spcd,pq->spcqd', w, eye).reshape(n_slab, LRU_SLAB, LRU_SLAB)

    wg = jnp.concatenate([slab_diag(wa[0]), slab_diag(wx[0]), slab_diag(wa[1]), slab_diag(wx[1])],
                         axis=-1)
    bg = jnp.stack([ba[0], bx[0], ba[1], bx[1]], axis=0).reshape(4, n_slab, LRU_SLAB)
    bg = bg.transpose(1, 0, 2).reshape(n_slab, 1, 4 * LRU_SLAB)
    return (0.5 * wg).astype(BF16), 0.5 * bg


def kernel(x, c, ctx, c_ctx, w_ada, b_ada, w_in, q_norm_w, k_norm_w, lru_conv_w, lru_conv_b, lru_wa,
           lru_ba, lru_wx, lru_bx, lru_lambda, attn_out_norm_w, lru_out_norm_w, w_out, ffn_w_up,
           ffn_conv_w, ffn_conv_b, ffn_w_down, final_norm_w):
    batch, seq, _ = x.shape
    ctx_len = ctx.shape[1]
    depth = w_in.shape[0]

    mod_rows = -(-(batch + 1) // SUBLANES) * SUBLANES
    cvec = jnp.zeros((mod_rows, D_MODEL), F32).at[:batch].set(c).at[batch].set(c_ctx)
    mods = _modulation(cvec, w_ada, b_ada)

    cos, sin = _rope_tables(seq)
    ones = jnp.ones((ctx_len, LANES), F32)
    bd = jnp.kron(jnp.eye(QK_W // HEAD_DIM, dtype=F32), jnp.ones((HEAD_DIM, HEAD_DIM), F32)).astype(BF16)

    x_lat = _to_segments(x)
    x_ctx = _to_segments(ctx)
    mods = mods.reshape(depth * mod_rows, 1, 6 * D_MODEL)
    w_in_b = w_in.astype(BF16)
    w_out_b = w_out.astype(BF16)
    w_up_b = ffn_w_up.astype(BF16)
    w_dn_b = ffn_w_down.astype(BF16)
    conv_b = ffn_conv_b.reshape(depth, 1, 2 * D_FF)

    for l in range(depth):
        ctx_out = l < depth - 1
        last = l == depth - 1
        lat = dict(n_seq=batch, seq_len=seq, shared_mod=False, mod_row=l * mod_rows, layer=l)
        cx = dict(n_seq=batch, seq_len=ctx_len, shared_mod=True, mod_row=l * mod_rows + batch, layer=l)
        nw_qk = jnp.concatenate([jnp.tile(q_norm_w[l], D_ATTN // HEAD_DIM),
                                 jnp.tile(k_norm_w[l], KV_W // HEAD_DIM)]).reshape(1, QK_W)
        qt_l, k_l, vt_l, xr_l, g_l = _inproj(x_lat, mods, w_in_b, bd, nw_qk, cos, sin,
                                             use_rope=True, **lat)
        qt_c, k_c, vt_c, xr_c, g_c = _inproj(x_ctx, mods, w_in_b, bd, nw_qk, ones, ones,
                                             use_rope=False, **cx)

        a_l = _attention(qt_l, k_c, vt_c, k_l, vt_l, n_seq=batch, q_len=seq, ctx_len=ctx_len,
                         lat_len=seq)
        wg, bg = _gate_weights(lru_wa[l], lru_wx[l], lru_ba[l], lru_bx[l])
        r = _lru(xr_c, g_c, xr_l, g_l, lru_conv_w[l], lru_conv_b[l].reshape(1, D_LRU), wg, bg,
                 lru_lambda[l], n_seq=batch, ctx_len=ctx_len, lat_len=seq, ctx_out=ctx_out)

        nw_out = jnp.concatenate([attn_out_norm_w[l], lru_out_norm_w[l]]).reshape(1, D_MODEL)
        x_lat = _outproj(a_l, r[0], x_lat, mods, nw_out, w_out_b, **lat)
        x_lat = _ffn(x_lat, mods, w_up_b, ffn_conv_w, conv_b, w_dn_b,
                     final_norm_w.reshape(1, D_MODEL) if last else None, **lat)
        if ctx_out:
            a_c = _attention(qt_c, k_c, vt_c, None, None, n_seq=batch, q_len=ctx_len,
                             ctx_len=ctx_len, lat_len=0)
            x_ctx = _outproj(a_c, r[1], x_ctx, mods, nw_out, w_out_b, **cx)
            x_ctx = _ffn(x_ctx, mods, w_up_b, ffn_conv_w, conv_b, w_dn_b, None, **cx)
    return _from_segments(x_lat, batch, seq)
```

```python
import functools

import jax
import jax.numpy as jnp
from jax import lax
from jax.experimental import pallas as pl
from jax.experimental.pallas import tpu as pltpu

F32 = jnp.float32
BF16 = jnp.bfloat16

D_MODEL = 1024
D_ATTN = 512
D_LRU = 512
KV_W = 128
HEAD_DIM = 64
N_HEADS = 8
N_KV = 2
KV_GROUP = 4
LOG2E = 1.4426950408889634
QK_W = D_ATTN + KV_W
D_IN = D_ATTN + 2 * KV_W + 2 * D_LRU
D_FF = 2816
LRU_C = 8.0
GRID_W = 64
ROPE_THETA = 10000.0
EPS = 1e-6
SUBLANES = 8
LANES = 128
BF16_ROWS = 16
V_ROWS = HEAD_DIM + BF16_ROWS
K_COLS = LANES
MAX_SHIFTED_WEIGHT = 2.0 ** 60
FF_CHUNK = 256
SUB_ROWS = 256
LRU_SLAB = 256
SCAN_BLOCK = 8
VMEM_LIMIT = 56 * 1024 * 1024


def _dot(a, b):
    return jnp.dot(a, b, preferred_element_type=F32)


def _sigmoid(z):
    return 0.5 * jnp.tanh(0.5 * z) + 0.5


def _rms(x):
    return x * lax.rsqrt(jnp.mean(x * x, axis=-1, keepdims=True) + EPS)


def _shift_down(g):
    rows = lax.broadcasted_iota(jnp.int32, g.shape, 0)
    return jnp.where(rows == 0, 0.0, pltpu.roll(g, 1, 0))


def _shift_up(g):
    rows = lax.broadcasted_iota(jnp.int32, g.shape, 0)
    return jnp.where(rows == SUBLANES - 1, 0.0, pltpu.roll(g, SUBLANES - 1, 0))


def _n_sub(tile_rows):
    return max(2, tile_rows // SUB_ROWS)


def _params(*sem):
    return pltpu.CompilerParams(dimension_semantics=sem, vmem_limit_bytes=VMEM_LIMIT)


def _resident(shape, layer=None):
    nd = len(shape)
    if layer is None:
        return pl.BlockSpec(shape, lambda *_: (0,) * nd, pipeline_mode=pl.Buffered(1))
    return pl.BlockSpec((None,) + tuple(shape), lambda *_: (layer,) + (0,) * nd,
                        pipeline_mode=pl.Buffered(1))


def _mod_kernel(c_ref, w_ref, b_ref, o_ref):
    c = c_ref[...]
    s = (c * _sigmoid(c)).astype(BF16)
    o_ref[...] = _dot(s, w_ref[...].astype(BF16)) + b_ref[...]


def _modulation(cvec, w_ada, b_ada, tn=1536):
    depth, d, n = w_ada.shape
    rows = cvec.shape[0]
    return pl.pallas_call(
        _mod_kernel,
        grid=(depth, n // tn),
        in_specs=[pl.BlockSpec((rows, d), lambda l, j: (0, 0)),
                  pl.BlockSpec((None, d, tn), lambda l, j: (l, 0, j)),
                  pl.BlockSpec((None, 1, tn), lambda l, j: (l, 0, j))],
        out_specs=pl.BlockSpec((None, rows, tn), lambda l, j: (l, 0, j)),
        out_shape=jax.ShapeDtypeStruct((depth, rows, n), F32),
        compiler_params=_params("arbitrary", "arbitrary"),
        name="adaln_mod",
    )(cvec, w_ada, b_ada.reshape(depth, 1, n))


def _inproj_kernel(x_ref, mod_ref, w_ref, bd_ref, nw_ref, cos_ref, sin_ref,
                   qt_ref, k_ref, vt_ref, xl_ref, g_ref, *, use_rope):
    shift = mod_ref[:, 0:D_MODEL]
    scale = mod_ref[:, D_MODEL:2 * D_MODEL]
    n_sub = _n_sub(x_ref.shape[0])
    sub = x_ref.shape[0] // n_sub

    def project(r0):
        h = _rms(x_ref[pl.ds(r0, sub), :]) * (1.0 + scale) + shift
        y = _dot(h.astype(BF16), w_ref[...])
        qk = y[:, 0:QK_W]
        return y, _dot((qk * qk).astype(BF16), bd_ref[...])

    def finish(r0, y_ss):
        y, ss = y_ss
        rows = pl.ds(r0, sub)
        qk = y[:, 0:QK_W]
        qk = qk * lax.rsqrt(ss * (1.0 / HEAD_DIM) + EPS) * nw_ref[...]
        if use_rope:
            lane = lax.broadcasted_iota(jnp.int32, (sub, LANES), 1)
            first_half = (lane % HEAD_DIM) < (HEAD_DIM // 2)
            cos = cos_ref[rows, :]
            sin = sin_ref[rows, :]
            cols = []
            for k in range(QK_W // LANES):
                xc = qk[:, LANES * k:LANES * (k + 1)]
                other = jnp.where(first_half, pltpu.roll(xc, LANES - HEAD_DIM // 2, 1),
                                  pltpu.roll(xc, HEAD_DIM // 2, 1))
                cols.append(xc * cos + other * sin)
            qk = jnp.concatenate(cols, axis=1)
        qt = (qk[:, 0:D_ATTN] * (LOG2E * HEAD_DIM ** -0.5)).T
        for h in range(N_HEADS):
            qt_ref[h, :, rows] = qt[HEAD_DIM * h:HEAD_DIM * (h + 1)].astype(BF16)
        kk = qk[:, D_ATTN:QK_W]
        lane = lax.broadcasted_iota(jnp.int32, kk.shape, 1)
        one_hot = jnp.where(lane == HEAD_DIM, 1.0, 0.0)
        for g in range(N_KV):
            kg = kk if g == 0 else pltpu.roll(kk, HEAD_DIM, 1)
            k_ref[g, rows, :] = jnp.where(lane < HEAD_DIM, kg, one_hot).astype(BF16)
        vt = y[:, QK_W:QK_W + KV_W].T.astype(BF16)
        ones = jnp.ones((V_ROWS - HEAD_DIM, sub), BF16)
        for g in range(N_KV):
            vt_ref[g, :, rows] = jnp.concatenate([vt[HEAD_DIM * g:HEAD_DIM * (g + 1)], ones], axis=0)
        xl_ref[rows, :] = y[:, QK_W + KV_W:QK_W + KV_W + D_LRU]
        g_ref[rows, :] = y[:, QK_W + KV_W + D_LRU:D_IN]

    y = project(0)
    for s in range(n_sub):
        y_next = project((s + 1) * sub) if s + 1 < n_sub else None
        finish(s * sub, y)
        y = y_next


def _inproj(x, mod, w_in, bd, nw, cos, sin, *, n_seq, seq_len, shared_mod, mod_row, layer,
            use_rope):
    tm = min(1024, seq_len)
    tps = seq_len // tm
    rows = n_seq * seq_len
    mod_idx = (lambda i: (mod_row, 0, 0)) if shared_mod else (lambda i: (mod_row + i // tps, 0, 0))
    kern = functools.partial(_inproj_kernel, use_rope=use_rope)
    return pl.pallas_call(
        kern,
        grid=(rows // tm,),
        in_specs=[pl.BlockSpec((tm, D_MODEL), lambda i: (i, 0)),
                  pl.BlockSpec((None, 1, 2 * D_MODEL), mod_idx),
                  _resident((D_MODEL, D_IN), layer),
                  _resident((QK_W, QK_W)),
                  _resident((1, QK_W)),
                  pl.BlockSpec((tm, LANES), lambda i: (i % tps, 0)),
                  pl.BlockSpec((tm, LANES), lambda i: (i % tps, 0))],
        out_specs=[pl.BlockSpec((None, N_HEADS, HEAD_DIM, tm), lambda i: (i // tps, 0, 0, i % tps)),
                   pl.BlockSpec((None, N_KV, tm, K_COLS), lambda i: (i // tps, 0, i % tps, 0)),
                   pl.BlockSpec((None, N_KV, V_ROWS, tm), lambda i: (i // tps, 0, 0, i % tps)),
                   pl.BlockSpec((tm, D_LRU), lambda i: (i, 0)),
                   pl.BlockSpec((tm, D_LRU), lambda i: (i, 0))],
        out_shape=[jax.ShapeDtypeStruct((n_seq, N_HEADS, HEAD_DIM, seq_len), BF16),
                   jax.ShapeDtypeStruct((n_seq, N_KV, seq_len, K_COLS), BF16),
                   jax.ShapeDtypeStruct((n_seq, N_KV, V_ROWS, seq_len), BF16),
                   jax.ShapeDtypeStruct((rows, D_LRU), F32),
                   jax.ShapeDtypeStruct((rows, D_LRU), F32)],
        compiler_params=_params("arbitrary"),
        name="in_proj_rope" if use_rope else "in_proj",
    )(x, mod, w_in, bd, nw, cos, sin)


def _attn_kernel(*refs, tq, tk, n_lat_chunks):
    if n_lat_chunks:
        qt_ref, kc_ref, vtc_ref, kl_ref, vtl_ref, o_ref = refs
    else:
        qt_ref, kc_ref, vtc_ref, o_ref = refs
    n = KV_GROUP * tq

    def queries(g):
        return jnp.concatenate([qt_ref[KV_GROUP * g + h] for h in range(KV_GROUP)], axis=1)

    def ctx_scores(g, qt):
        return _dot(kc_ref[g, :, 0:HEAD_DIM], qt)

    def finish(g, acc):
        ot = acc[0:HEAD_DIM] * (1.0 / acc[HEAD_DIM:HEAD_DIM + 1])
        for k in range(KV_GROUP):
            h = KV_GROUP * g + k
            o_ref[:, HEAD_DIM * h:HEAD_DIM * (h + 1)] = ot[:, k * tq:(k + 1) * tq].T.astype(o_ref.dtype)

    def exact(g):
        qt = queries(g)
        s = ctx_scores(g, qt)
        m = jnp.max(s, axis=0, keepdims=True)
        acc = _dot(vtc_ref[g], jnp.exp2(s - m).astype(BF16))

        def body(c, carry):
            m, acc = carry
            off = pl.multiple_of(c * tk, tk)
            s = _dot(kl_ref[g, pl.ds(off, tk), 0:HEAD_DIM], qt)
            m_new = jnp.maximum(m, jnp.max(s, axis=0, keepdims=True))
            p = jnp.exp2(s - m_new).astype(BF16)
            return m_new, jnp.exp2(m - m_new) * acc + _dot(vtl_ref[g, :, pl.ds(off, tk)], p)

        if n_lat_chunks:
            m, acc = lax.fori_loop(0, n_lat_chunks, body, (m, acc))
        return acc

    if not n_lat_chunks:
        for g in range(N_KV):
            finish(g, exact(g))
        return

    row = lax.broadcasted_iota(jnp.int32, (BF16_ROWS, n), 0)
    pad = jnp.zeros((K_COLS - HEAD_DIM - BF16_ROWS, n), BF16)

    def start(g):
        qt = queries(g)
        s = ctx_scores(g, qt)
        running = jnp.max(s, axis=0, keepdims=True)
        shift = running.astype(BF16).astype(F32)
        acc = _dot(vtc_ref[g], jnp.exp2(s - shift).astype(BF16))
        return dict(qt=qt, running=running, shift=shift, shifts=[shift, shift], acc=acc,
                    largest=jnp.ones((1, n), F32))

    def issue(g, c, st):
        new_shift = st["shifts"].pop(0)
        shift_rows = jnp.where(row == 0, -new_shift, 0.0).astype(BF16)
        qa = jnp.concatenate([st["qt"], shift_rows, pad], axis=0)
        st["issued"] = new_shift, _dot(kl_ref[g, pl.ds(c * tk, tk), :], qa)

    def consume(g, c, st):
        new_shift, s = st.pop("issued")
        p = jnp.exp2(s)
        pv = _dot(vtl_ref[g, :, pl.ds(c * tk, tk)], p.astype(BF16))
        st["acc"] = st["acc"] * jnp.exp2(st["shift"] - new_shift) + pv
        st["shift"] = new_shift
        pmax = jnp.max(p, axis=0, keepdims=True)
        st["largest"] = jnp.maximum(st["largest"], pmax)
        st["running"] = jnp.maximum(st["running"], new_shift + jnp.log2(pmax))
        st["shifts"].append(st["running"].astype(BF16).astype(F32))

    states = [start(g) for g in range(N_KV)]
    items = [(g, c) for c in range(n_lat_chunks) for g in range(N_KV)]
    issue(*items[0], states[items[0][0]])
    for k, (g, c) in enumerate(items):
        if k + 1 < len(items):
            gn, cn = items[k + 1]
            issue(gn, cn, states[gn])
        consume(g, c, states[g])
    overflow = jnp.max(jnp.maximum(states[0]["largest"], states[1]["largest"])) > MAX_SHIFTED_WEIGHT

    @pl.when(overflow)
    def _():
        for g in range(N_KV):
            finish(g, exact(g))

    @pl.when(jnp.logical_not(overflow))
    def _():
        for g in range(N_KV):
            finish(g, states[g]["acc"])


def _attention(qt, k_ctx, vt_ctx, k_lat, vt_lat, *, n_seq, q_len, ctx_len, lat_len):
    tq = min(256, q_len)
    tk = min(512, lat_len // 2) if lat_len else 0
    n_lat_chunks = lat_len // tk if lat_len else 0
    assert n_lat_chunks % 2 == 0
    nq = q_len // tq
    kern = functools.partial(_attn_kernel, tq=tq, tk=tk, n_lat_chunks=n_lat_chunks)
    in_specs = [pl.BlockSpec((None, N_HEADS, HEAD_DIM, tq), lambda b, i: (b, 0, 0, i)),
                pl.BlockSpec((None, N_KV, ctx_len, K_COLS), lambda b, i: (b, 0, 0, 0)),
                pl.BlockSpec((None, N_KV, V_ROWS, ctx_len), lambda b, i: (b, 0, 0, 0))]
    args = [qt, k_ctx, vt_ctx]
    scratch = []
    if lat_len:
        in_specs += [pl.BlockSpec((None, N_KV, lat_len, K_COLS), lambda b, i: (b, 0, 0, 0)),
                     pl.BlockSpec((None, N_KV, V_ROWS, lat_len), lambda b, i: (b, 0, 0, 0))]
        args += [k_lat, vt_lat]
    return pl.pallas_call(
        kern,
        grid=(n_seq, nq),
        in_specs=in_specs,
        out_specs=pl.BlockSpec((tq, D_ATTN), lambda b, i: (b * nq + i, 0)),
        out_shape=jax.ShapeDtypeStruct((n_seq * q_len, D_ATTN), BF16),
        scratch_shapes=scratch,
        compiler_params=_params("arbitrary", "arbitrary"),
        name="attn_latent" if lat_len else "attn_context",
    )(*args)


def _lru_sequence(x_ref, g_ref, o_ref, h0, consts, scr, *, n_groups):
    cw_ref, cb_ref, wg_ref, bg_ref, hnsp = consts
    xe_ref, af_ref, bf_ref, ar_ref, br_ref = scr
    rows = SUBLANES * n_groups
    width = x_ref.shape[1]
    chunk = min(256, rows)
    n_chunks = rows // chunk

    xe_ref[pl.ds(2 * SUBLANES, rows), :] = x_ref[...]
    xe_ref[pl.ds(0, SUBLANES), :] = _shift_down(x_ref[pl.ds(rows - 2 * SUBLANES, SUBLANES), :])
    xe_ref[pl.ds(SUBLANES, SUBLANES), :] = _shift_down(x_ref[pl.ds(rows - SUBLANES, SUBLANES), :])
    xe_ref[pl.ds(rows + 2 * SUBLANES, SUBLANES), :] = _shift_up(x_ref[pl.ds(0, SUBLANES), :])

    def gates(c, _):
        r0 = pl.multiple_of(c * chunk, chunk)
        xc = cb_ref[...]
        for k in range(4):
            xc = xc + cw_ref[k:k + 1, :] * xe_ref[pl.ds(r0 + SUBLANES * k, chunk), :]
        t = jnp.tanh(_dot(xc.astype(BF16), wg_ref[...]) + bg_ref[...])
        hx = 0.5 * xc
        for d, (a_ref, b_ref) in enumerate(((af_ref, bf_ref), (ar_ref, br_ref))):
            t_r = t[:, (2 * d) * width:(2 * d + 1) * width]
            t_i = t[:, (2 * d + 1) * width:(2 * d + 2) * width]
            log_a = hnsp[d] * t_r + hnsp[d]
            a = jnp.exp(log_a)
            y = jnp.tanh(-log_a) * (1.0 + a * a)
            b = jnp.where(y > 0.0, y * lax.rsqrt(y), 0.0) * (hx * t_i + hx)
            a_ref[pl.ds(r0, chunk), :] = a
            b_ref[pl.ds(r0, chunk), :] = b
        return 0

    lax.fori_loop(0, n_chunks, gates, 0)

    blk = min(SCAN_BLOCK, n_groups)
    blk_rows = blk * SUBLANES
    n_blocks = n_groups // blk

    def scan_block(a_ref, b_ref, r0, h, p, order):
        a_blk = a_ref[pl.ds(r0, blk_rows), :]
        b_blk = b_ref[pl.ds(r0, blk_rows), :]
        hs, ps = [None] * blk, [None] * blk
        for k in order:
            a = a_blk[k * SUBLANES:(k + 1) * SUBLANES]
            h = a * h + b_blk[k * SUBLANES:(k + 1) * SUBLANES]
            p = a * p
            hs[k], ps[k] = h, p
        b_ref[pl.ds(r0, blk_rows), :] = jnp.concatenate(hs, axis=0)
        a_ref[pl.ds(r0, blk_rows), :] = jnp.concatenate(ps, axis=0)
        return h, p

    def scan(i, carry):
        hf, pf, hr, pr = carry
        rf = pl.multiple_of(i * blk_rows, blk_rows)
        rr = pl.multiple_of((n_blocks - 1 - i) * blk_rows, blk_rows)
        hf, pf = scan_block(af_ref, bf_ref, rf, hf, pf, range(blk))
        hr, pr = scan_block(ar_ref, br_ref, rr, hr, pr, range(blk - 1, -1, -1))
        return hf, pf, hr, pr

    zero = jnp.zeros((SUBLANES, width), F32)
    one = jnp.ones((SUBLANES, width), F32)
    hf, pf, hr, pr = lax.fori_loop(0, n_blocks, scan, (zero, one, zero, one))

    e = h0[0]
    init_f = []
    for s in range(SUBLANES):
        init_f.append(e)
        e = hf[s:s + 1, :] + pf[s:s + 1, :] * e
    fin_f = e
    e = h0[1]
    init_r = [None] * SUBLANES
    for s in range(SUBLANES - 1, -1, -1):
        init_r[s] = e
        e = hr[s:s + 1, :] + pr[s:s + 1, :] * e
    fin_r = e

    if o_ref is not None:
        init_f = jnp.concatenate(init_f, axis=0)[None]
        init_r = jnp.concatenate(init_r, axis=0)[None]

        def combine(c, _):
            r0 = pl.multiple_of(c * chunk, chunk)
            sl = pl.ds(r0, chunk)
            shp = (chunk // SUBLANES, SUBLANES, width)
            h = (bf_ref[sl, :].reshape(shp) + af_ref[sl, :].reshape(shp) * init_f
                 + br_ref[sl, :].reshape(shp) + ar_ref[sl, :].reshape(shp) * init_r)
            gt = g_ref[sl, :]
            gelu = 0.5 * gt * (1.0 + jnp.tanh(0.7978845608028654 * (gt + 0.044715 * gt * gt * gt)))
            o_ref[sl, :] = (h.reshape(chunk, width) * gelu).astype(o_ref.dtype)
            return 0

        lax.fori_loop(0, n_chunks, combine, 0)
    return fin_f, fin_r


def _lru_kernel(*refs, ctx_groups, lat_groups, ctx_out):
    xc_ref, gc_ref, xl_ref, gl_ref, cw_ref, cb_ref, wg_ref, bg_ref, lam_ref = refs[:9]
    n_out = 2 if ctx_out else 1
    outs = refs[9:9 + n_out]
    scr = refs[9 + n_out:]
    z = -lam_ref[...]
    softplus = jnp.maximum(z, 0.0) + jnp.log1p(jnp.exp(-jnp.abs(z)))
    hnsp = [(-0.5 * LRU_C) * softplus[d:d + 1, :] for d in range(2)]
    consts = (cw_ref, cb_ref, wg_ref, bg_ref, hnsp)
    zero = jnp.zeros((1, xc_ref.shape[1]), F32)
    fin = _lru_sequence(xc_ref, gc_ref, outs[1] if ctx_out else None, (zero, zero), consts, scr,
                        n_groups=ctx_groups)
    _lru_sequence(xl_ref, gl_ref, outs[0], fin, consts, scr, n_groups=lat_groups)


def _lru(x_ctx, g_ctx, x_lat, g_lat, conv_w, conv_b, wg, bg, lam, *, n_seq, ctx_len, lat_len, ctx_out):
    w = LRU_SLAB
    n_slab = D_LRU // w
    kern = functools.partial(_lru_kernel, ctx_groups=ctx_len // SUBLANES,
                             lat_groups=lat_len // SUBLANES, ctx_out=ctx_out)
    ctx_spec = pl.BlockSpec((ctx_len, w), lambda b, c: (b, c))
    lat_spec = pl.BlockSpec((lat_len, w), lambda b, c: (b, c))
    out_specs = [lat_spec]
    out_shape = [jax.ShapeDtypeStruct((n_seq * lat_len, D_LRU), BF16)]
    if ctx_out:
        out_specs.append(ctx_spec)
        out_shape.append(jax.ShapeDtypeStruct((n_seq * ctx_len, D_LRU), BF16))
    max_rows = max(ctx_len, lat_len)
    return pl.pallas_call(
        kern,
        grid=(n_seq, n_slab),
        in_specs=[ctx_spec, ctx_spec, lat_spec, lat_spec,
                  pl.BlockSpec((4, w), lambda b, c: (0, c)),
                  pl.BlockSpec((1, w), lambda b, c: (0, c)),
                  pl.BlockSpec((None, w, 4 * w), lambda b, c: (c, 0, 0)),
                  pl.BlockSpec((None, 1, 4 * w), lambda b, c: (c, 0, 0)),
                  pl.BlockSpec((2, w), lambda b, c: (0, c))],
        out_specs=out_specs,
        out_shape=out_shape,
        scratch_shapes=[pltpu.VMEM((max_rows + 3 * SUBLANES, w), F32)]
                       + [pltpu.VMEM((max_rows, w), F32)] * 4,
        compiler_params=_params("arbitrary", "arbitrary"),
        name="rglru_bidir",
    )(x_ctx, g_ctx, x_lat, g_lat, conv_w, conv_b, wg, bg, lam)


def _outproj_kernel(a_ref, r_ref, x_ref, gate_ref, nw_ref, w_ref, o_ref):
    n_sub = _n_sub(x_ref.shape[0])
    sub = x_ref.shape[0] // n_sub

    def project(r0):
        rows = pl.ds(r0, sub)
        cat = jnp.concatenate([_rms(a_ref[rows, :].astype(F32)), _rms(r_ref[rows, :].astype(F32))],
                              axis=1) * nw_ref[...]
        return _dot(cat.astype(BF16), w_ref[...])

    y = project(0)
    for s in range(n_sub):
        y_next = project((s + 1) * sub) if s + 1 < n_sub else None
        rows = pl.ds(s * sub, sub)
        o_ref[rows, :] = x_ref[rows, :] + gate_ref[...] * y
        y = y_next


def _outproj(a, r, x, mod, nw, w_out, *, n_seq, seq_len, shared_mod, mod_row, layer):
    tm = min(1024, seq_len)
    tps = seq_len // tm
    rows = n_seq * seq_len
    gate_idx = (lambda i: (mod_row, 0, 2)) if shared_mod else (lambda i: (mod_row + i // tps, 0, 2))
    return pl.pallas_call(
        _outproj_kernel,
        grid=(rows // tm,),
        in_specs=[pl.BlockSpec((tm, D_ATTN), lambda i: (i, 0)),
                  pl.BlockSpec((tm, D_LRU), lambda i: (i, 0)),
                  pl.BlockSpec((tm, D_MODEL), lambda i: (i, 0)),
                  pl.BlockSpec((None, 1, D_MODEL), gate_idx),
                  _resident((1, D_MODEL)),
                  _resident((D_MODEL, D_MODEL), layer)],
        out_specs=pl.BlockSpec((tm, D_MODEL), lambda i: (i, 0)),
        out_shape=jax.ShapeDtypeStruct((rows, D_MODEL), F32),
        compiler_params=_params("arbitrary"),
        name="out_proj",
    )(a, r, x, mod, nw, w_out)


def _ffn_kernel(*refs, tm, tps, final_norm):
    x_ref, xp_ref, xn_ref, mod_ref, wup_ref, cw_ref, cb_ref, wdn_ref = refs[:8]
    if final_norm:
        fnw_ref, o_ref, hext_ref, act_ref, ua_ref, ub_ref = refs[8:]
    else:
        o_ref, hext_ref, act_ref, ua_ref, ub_ref = refs[8:]
    assert (D_FF // FF_CHUNK) % 2 == 1
    jt = pl.program_id(0) % tps
    shift = mod_ref[:, 0:D_MODEL]
    scale = mod_ref[:, D_MODEL:2 * D_MODEL]
    gate = mod_ref[:, 2 * D_MODEL:3 * D_MODEL]

    def norm_mod(x):
        return _rms(x) * (1.0 + scale) + shift

    x = x_ref[...]
    hext_ref[pl.ds(BF16_ROWS, tm), :] = norm_mod(x).astype(BF16)
    hp = norm_mod(xp_ref[...])
    hp = jnp.where(jt == 0, _shift_down(hp), hp)
    hn = norm_mod(xn_ref[...])
    hn = jnp.where(jt == tps - 1, _shift_up(hn), hn)
    zeros = jnp.zeros_like(hp)
    hext_ref[pl.ds(0, BF16_ROWS), :] = jnp.concatenate([zeros, hp], axis=0).astype(BF16)
    hext_ref[pl.ds(BF16_ROWS + tm, BF16_ROWS), :] = jnp.concatenate([hn, zeros], axis=0).astype(BF16)

    def up(c, u_ref):
        for k in range(2):
            off = pl.multiple_of(k * D_FF + c * FF_CHUNK, FF_CHUNK)
            u_ref[k] = _dot(hext_ref[...], wup_ref[:, pl.ds(off, FF_CHUNK)])

    def conv(u_ref, k, off):
        sl = pl.ds(off, FF_CHUNK)
        return (cw_ref[0:1, sl] * u_ref[k, pl.ds(SUBLANES, tm), :]
                + cw_ref[1:2, sl] * u_ref[k, pl.ds(2 * SUBLANES, tm), :]
                + cw_ref[2:3, sl] * u_ref[k, pl.ds(3 * SUBLANES, tm), :]
                + cb_ref[:, sl])

    def activate(c, u_ref):
        og = pl.multiple_of(c * FF_CHUNK, FF_CHUNK)
        ov = pl.multiple_of(D_FF + c * FF_CHUNK, FF_CHUNK)
        yg = conv(u_ref, 0, og)
        yv = conv(u_ref, 1, ov)
        hg = 0.5 * yg
        act_ref[:, pl.ds(og, FF_CHUNK)] = ((hg + hg * jnp.tanh(hg)) * yv).astype(BF16)

    n_chunks = D_FF // FF_CHUNK
    up(0, ua_ref)

    def pair(j, _):
        up(2 * j + 1, ub_ref)
        activate(2 * j, ua_ref)
        up(2 * j + 2, ua_ref)
        activate(2 * j + 1, ub_ref)
        return 0

    lax.fori_loop(0, (n_chunks - 1) // 2, pair, 0)
    activate(n_chunks - 1, ua_ref)
    out = x + gate * _dot(act_ref[...], wdn_ref[...])
    if final_norm:
        out = _rms(out) * fnw_ref[...]
    o_ref[...] = out


def _ffn(x, mod, w_up, conv_w, conv_b, w_down, final_w, *, n_seq, seq_len, shared_mod, mod_row, layer):
    tm = min(1024, seq_len)
    tps = seq_len // tm
    rows = n_seq * seq_len
    gps = seq_len // SUBLANES
    gpt = tm // SUBLANES
    mod_idx = (lambda i: (mod_row, 0, 1)) if shared_mod else (lambda i: (mod_row + i // tps, 0, 1))

    def prev_idx(i):
        jt = i % tps
        return ((i // tps) * gps + jnp.where(jt == 0, gps - 1, jt * gpt - 1), 0)

    def next_idx(i):
        jt = i % tps
        return ((i // tps) * gps + jnp.where(jt == tps - 1, 0, (jt + 1) * gpt), 0)

    final_norm = final_w is not None
    in_specs = [pl.BlockSpec((tm, D_MODEL), lambda i: (i, 0)),
                pl.BlockSpec((SUBLANES, D_MODEL), prev_idx),
                pl.BlockSpec((SUBLANES, D_MODEL), next_idx),
                pl.BlockSpec((None, 1, 3 * D_MODEL), mod_idx),
                _resident((D_MODEL, 2 * D_FF), layer),
                _resident((3, 2 * D_FF), layer),
                _resident((1, 2 * D_FF), layer),
                _resident((D_FF, D_MODEL), layer)]
    args = [x, x, x, mod, w_up, conv_w, conv_b, w_down]
    if final_norm:
        in_specs.append(_resident((1, D_MODEL)))
        args.append(final_w)
    kern = functools.partial(_ffn_kernel, tm=tm, tps=tps, final_norm=final_norm)
    return pl.pallas_call(
        kern,
        grid=(rows // tm,),
        in_specs=in_specs,
        out_specs=pl.BlockSpec((tm, D_MODEL), lambda i: (i, 0)),
        out_shape=jax.ShapeDtypeStruct((rows, D_MODEL), F32),
        scratch_shapes=[pltpu.VMEM((tm + 2 * BF16_ROWS, D_MODEL), BF16),
                        pltpu.VMEM((tm, D_FF), BF16)]
                       + [pltpu.VMEM((2, tm + 2 * BF16_ROWS, FF_CHUNK), F32)] * 2,
        compiler_params=_params("arbitrary"),
        name="conv_ffn_final" if final_norm else "conv_ffn",
    )(*args)


def _to_segments(x):
    b, t, c = x.shape
    return x.reshape(b, SUBLANES, t // SUBLANES, c).transpose(0, 2, 1, 3).reshape(b * t, c)


def _from_segments(y, b, t):
    return y.reshape(b, t // SUBLANES, SUBLANES, -1).transpose(0, 2, 1, 3).reshape(b, t, -1)


def _rope_tables(seq_len):
    r = jnp.arange(seq_len)
    t = (r % SUBLANES) * (seq_len // SUBLANES) + r // SUBLANES
    row = (t // GRID_W).astype(F32)
    col = (t % GRID_W).astype(F32)
    pairs = HEAD_DIM // 4
    inv = ROPE_THETA ** (-jnp.arange(pairs, dtype=F32) / pairs)
    ang = jnp.concatenate([row[:, None] * inv, col[:, None] * inv], axis=-1)
    cos, sin = jnp.cos(ang), jnp.sin(ang)
    return (jnp.concatenate([cos, cos, cos, cos], axis=-1),
            jnp.concatenate([-sin, sin, -sin, sin], axis=-1))


def _gate_weights(wa, wx, ba, bx):
    n_slab = D_LRU // LRU_SLAB
    per = LRU_SLAB // wa.shape[-1]

    def slab_diag(w):
        w = w.reshape(n_slab, per, w.shape[-2], w.shape[-1])
        eye = jnp.eye(per, dtype=w.dtype)
        return jnp.einsum('spcd,pq->spcqd', w, eye).reshape(n_slab, LRU_SLAB, LRU_SLAB)

    wg = jnp.concatenate([slab_diag(wa[0]), slab_diag(wx[0]), slab_diag(wa[1]), slab_diag(wx[1])],
                         axis=-1)
    bg = jnp.stack([ba[0], bx[0], ba[1], bx[1]], axis=0).reshape(4, n_slab, LRU_SLAB)
    bg = bg.transpose(1, 0, 2).reshape(n_slab, 1, 4 * LRU_SLAB)
    return (0.5 * wg).astype(BF16), 0.5 * bg


def kernel(x, c, ctx, c_ctx, w_ada, b_ada, w_in, q_norm_w, k_norm_w, lru_conv_w, lru_conv_b, lru_wa,
           lru_ba, lru_wx, lru_bx, lru_lambda, attn_out_norm_w, lru_out_norm_w, w_out, ffn_w_up,
           ffn_conv_w, ffn_conv_b, ffn_w_down, final_norm_w):
    batch, seq, _ = x.shape
    ctx_len = ctx.shape[1]
    depth = w_in.shape[0]

    mod_rows = -(-(batch + 1) // SUBLANES) * SUBLANES
    cvec = jnp.zeros((mod_rows, D_MODEL), F32).at[:batch].set(c).at[batch].set(c_ctx)
    mods = _modulation(cvec, w_ada, b_ada)

    cos, sin = _rope_tables(seq)
    ones = jnp.ones((ctx_len, LANES), F32)
    bd = jnp.kron(jnp.eye(QK_W // HEAD_DIM, dtype=F32), jnp.ones((HEAD_DIM, HEAD_DIM), F32)).astype(BF16)

    x_lat = _to_segments(x)
    x_ctx = _to_segments(ctx)
    mods = mods.reshape(depth * mod_rows, 1, 6 * D_MODEL)
    w_in_b = w_in.astype(BF16)
    w_out_b = w_out.astype(BF16)
    w_up_b = ffn_w_up.astype(BF16)
    w_dn_b = ffn_w_down.astype(BF16)
    conv_b = ffn_conv_b.reshape(depth, 1, 2 * D_FF)

    for l in range(depth):
        ctx_out = l < depth - 1
        last = l == depth - 1
        lat = dict(n_seq=batch, seq_len=seq, shared_mod=False, mod_row=l * mod_rows, layer=l)
        cx = dict(n_seq=batch, seq_len=ctx_len, shared_mod=True, mod_row=l * mod_rows + batch, layer=l)
        nw_qk = jnp.concatenate([jnp.tile(q_norm_w[l], D_ATTN // HEAD_DIM),
                                 jnp.tile(k_norm_w[l], KV_W // HEAD_DIM)]).reshape(1, QK_W)
        qt_l, k_l, vt_l, xr_l, g_l = _inproj(x_lat, mods, w_in_b, bd, nw_qk, cos, sin,
                                             use_rope=True, **lat)
        qt_c, k_c, vt_c, xr_c, g_c = _inproj(x_ctx, mods, w_in_b, bd, nw_qk, ones, ones,
                                             use_rope=False, **cx)

        a_l = _attention(qt_l, k_c, vt_c, k_l, vt_l, n_seq=batch, q_len=seq, ctx_len=ctx_len,
                         lat_len=seq)
        wg, bg = _gate_weights(lru_wa[l], lru_wx[l], lru_ba[l], lru_bx[l])
        r = _lru(xr_c, g_c, xr_l, g_l, lru_conv_w[l], lru_conv_b[l].reshape(1, D_LRU), wg, bg,
                 lru_lambda[l], n_seq=batch, ctx_len=ctx_len, lat_len=seq, ctx_out=ctx_out)

        nw_out = jnp.concatenate([attn_out_norm_w[l], lru_out_norm_w[l]]).reshape(1, D_MODEL)
        x_lat = _outproj(a_l, r[0], x_lat, mods, nw_out, w_out_b, **lat)
        x_lat = _ffn(x_lat, mods, w_up_b, ffn_conv_w, conv_b, w_dn_b,
                     final_norm_w.reshape(1, D_MODEL) if last else None, **lat)
        if ctx_out:
            a_c = _attention(qt_c, k_c, vt_c, None, None, n_seq=batch, q_len=ctx_len,
                             ctx_len=ctx_len, lat_len=0)
            x_ctx = _outproj(a_c, r[1], x_ctx, mods, nw_out, w_out_b, **cx)
            x_ctx = _ffn(x_ctx, mods, w_up_b, ffn_conv_w, conv_b, w_dn_b, None, **cx)
    return _from_segments(x_lat, batch, seq)
```

```python
import functools

import jax
import jax.numpy as jnp
from jax import lax
from jax.experimental import pallas as pl
from jax.experimental.pallas import tpu as pltpu

F32 = jnp.float32
BF16 = jnp.bfloat16

D_MODEL = 1024
D_ATTN = 512
D_LRU = 512
KV_W = 128
HEAD_DIM = 64
N_HEADS = 8
N_KV = 2
KV_GROUP = 4
LOG2E = 1.4426950408889634
QK_W = D_ATTN + KV_W
D_IN = D_ATTN + 2 * KV_W + 2 * D_LRU
D_FF = 2816
LRU_C = 8.0
GRID_W = 64
ROPE_THETA = 10000.0
EPS = 1e-6
SUBLANES = 8
LANES = 128
BF16_ROWS = 16
V_ROWS = HEAD_DIM + BF16_ROWS
K_COLS = LANES
MAX_SHIFTED_WEIGHT = 2.0 ** 60
FF_CHUNK = 256
SUB_ROWS = 256
LRU_SLAB = 256
SCAN_BLOCK = 8
VMEM_LIMIT = 58 * 1024 * 1024


def _dot(a, b):
    return jnp.dot(a, b, preferred_element_type=F32)


def _sigmoid(z):
    return 0.5 * jnp.tanh(0.5 * z) + 0.5


def _rms(x):
    return x * lax.rsqrt(jnp.mean(x * x, axis=-1, keepdims=True) + EPS)


def _shift_down(g):
    rows = lax.broadcasted_iota(jnp.int32, g.shape, 0)
    return jnp.where(rows == 0, 0.0, pltpu.roll(g, 1, 0))


def _shift_up(g):
    rows = lax.broadcasted_iota(jnp.int32, g.shape, 0)
    return jnp.where(rows == SUBLANES - 1, 0.0, pltpu.roll(g, SUBLANES - 1, 0))


def _n_sub(tile_rows):
    return max(2, tile_rows // SUB_ROWS)


def _params(*sem):
    return pltpu.CompilerParams(dimension_semantics=sem, vmem_limit_bytes=VMEM_LIMIT)


def _resident(shape, layer=None):
    nd = len(shape)
    if layer is None:
        return pl.BlockSpec(shape, lambda *_: (0,) * nd, pipeline_mode=pl.Buffered(1))
    return pl.BlockSpec((None,) + tuple(shape), lambda *_: (layer,) + (0,) * nd,
                        pipeline_mode=pl.Buffered(1))


def _mod_kernel(c_ref, w_ref, b_ref, o_ref):
    c = c_ref[...]
    s = (c * _sigmoid(c)).astype(BF16)
    o_ref[...] = _dot(s, w_ref[...].astype(BF16)) + b_ref[...]


def _modulation(cvec, w_ada, b_ada, tn=1536):
    depth, d, n = w_ada.shape
    rows = cvec.shape[0]
    return pl.pallas_call(
        _mod_kernel,
        grid=(depth, n // tn),
        in_specs=[pl.BlockSpec((rows, d), lambda l, j: (0, 0)),
                  pl.BlockSpec((None, d, tn), lambda l, j: (l, 0, j)),
                  pl.BlockSpec((None, 1, tn), lambda l, j: (l, 0, j))],
        out_specs=pl.BlockSpec((None, rows, tn), lambda l, j: (l, 0, j)),
        out_shape=jax.ShapeDtypeStruct((depth, rows, n), F32),
        compiler_params=_params("arbitrary", "arbitrary"),
        name="adaln_mod",
    )(cvec, w_ada, b_ada.reshape(depth, 1, n))


def _inproj_kernel(x_ref, mod_ref, w_ref, bd_ref, nw_ref, cos_ref, sin_ref,
                   qt_ref, k_ref, vt_ref, xl_ref, g_ref, *, use_rope):
    shift = mod_ref[:, 0:D_MODEL]
    scale = mod_ref[:, D_MODEL:2 * D_MODEL]
    n_sub = _n_sub(x_ref.shape[0])
    sub = x_ref.shape[0] // n_sub

    def project(r0):
        h = _rms(x_ref[pl.ds(r0, sub), :]) * (1.0 + scale) + shift
        y = _dot(h.astype(BF16), w_ref[...])
        qk = y[:, 0:QK_W]
        return y, _dot((qk * qk).astype(BF16), bd_ref[...])

    def finish(r0, y_ss):
        y, ss = y_ss
        rows = pl.ds(r0, sub)
        qk = y[:, 0:QK_W]
        qk = qk * lax.rsqrt(ss * (1.0 / HEAD_DIM) + EPS) * nw_ref[...]
        if use_rope:
            lane = lax.broadcasted_iota(jnp.int32, (sub, LANES), 1)
            first_half = (lane % HEAD_DIM) < (HEAD_DIM // 2)
            cos = cos_ref[rows, :]
            sin = sin_ref[rows, :]
            cols = []
            for k in range(QK_W // LANES):
                xc = qk[:, LANES * k:LANES * (k + 1)]
                other = jnp.where(first_half, pltpu.roll(xc, LANES - HEAD_DIM // 2, 1),
                                  pltpu.roll(xc, HEAD_DIM // 2, 1))
                cols.append(xc * cos + other * sin)
            qk = jnp.concatenate(cols, axis=1)
        qt = (qk[:, 0:D_ATTN] * (LOG2E * HEAD_DIM ** -0.5)).T
        for h in range(N_HEADS):
            qt_ref[h, :, rows] = qt[HEAD_DIM * h:HEAD_DIM * (h + 1)].astype(BF16)
        kk = qk[:, D_ATTN:QK_W]
        lane = lax.broadcasted_iota(jnp.int32, kk.shape, 1)
        one_hot = jnp.where(lane == HEAD_DIM, 1.0, 0.0)
        for g in range(N_KV):
            kg = kk if g == 0 else pltpu.roll(kk, HEAD_DIM, 1)
            k_ref[g, rows, :] = jnp.where(lane < HEAD_DIM, kg, one_hot).astype(BF16)
        vt = y[:, QK_W:QK_W + KV_W].T.astype(BF16)
        ones = jnp.ones((V_ROWS - HEAD_DIM, sub), BF16)
        for g in range(N_KV):
            vt_ref[g, :, rows] = jnp.concatenate([vt[HEAD_DIM * g:HEAD_DIM * (g + 1)], ones], axis=0)
        xl_ref[rows, :] = y[:, QK_W + KV_W:QK_W + KV_W + D_LRU]
        g_ref[rows, :] = y[:, QK_W + KV_W + D_LRU:D_IN]

    y = project(0)
    for s in range(n_sub):
        y_next = project((s + 1) * sub) if s + 1 < n_sub else None
        finish(s * sub, y)
        y = y_next


def _inproj(x, mod, w_in, bd, nw, cos, sin, *, n_seq, seq_len, shared_mod, mod_row, layer,
            use_rope):
    tm = min(1024, seq_len)
    tps = seq_len // tm
    rows = n_seq * seq_len
    mod_idx = (lambda i: (mod_row, 0, 0)) if shared_mod else (lambda i: (mod_row + i // tps, 0, 0))
    kern = functools.partial(_inproj_kernel, use_rope=use_rope)
    return pl.pallas_call(
        kern,
        grid=(rows // tm,),
        in_specs=[pl.BlockSpec((tm, D_MODEL), lambda i: (i, 0)),
                  pl.BlockSpec((None, 1, 2 * D_MODEL), mod_idx),
                  _resident((D_MODEL, D_IN), layer),
                  _resident((QK_W, QK_W)),
                  _resident((1, QK_W)),
                  pl.BlockSpec((tm, LANES), lambda i: (i % tps, 0)),
                  pl.BlockSpec((tm, LANES), lambda i: (i % tps, 0))],
        out_specs=[pl.BlockSpec((None, N_HEADS, HEAD_DIM, tm), lambda i: (i // tps, 0, 0, i % tps)),
                   pl.BlockSpec((None, N_KV, tm, K_COLS), lambda i: (i // tps, 0, i % tps, 0)),
                   pl.BlockSpec((None, N_KV, V_ROWS, tm), lambda i: (i // tps, 0, 0, i % tps)),
                   pl.BlockSpec((tm, D_LRU), lambda i: (i, 0)),
                   pl.BlockSpec((tm, D_LRU), lambda i: (i, 0))],
        out_shape=[jax.ShapeDtypeStruct((n_seq, N_HEADS, HEAD_DIM, seq_len), BF16),
                   jax.ShapeDtypeStruct((n_seq, N_KV, seq_len, K_COLS), BF16),
                   jax.ShapeDtypeStruct((n_seq, N_KV, V_ROWS, seq_len), BF16),
                   jax.ShapeDtypeStruct((rows, D_LRU), F32),
                   jax.ShapeDtypeStruct((rows, D_LRU), F32)],
        compiler_params=_params("arbitrary"),
        name="in_proj_rope" if use_rope else "in_proj",
    )(x, mod, w_in, bd, nw, cos, sin)


def _attn_kernel(*refs, tq, tk, n_lat_chunks):
    if n_lat_chunks:
        qt_ref, kc_ref, vtc_ref, kl_ref, vtl_ref, o_ref = refs
    else:
        qt_ref, kc_ref, vtc_ref, o_ref = refs
    n = KV_GROUP * tq

    def queries(g):
        return jnp.concatenate([qt_ref[KV_GROUP * g + h] for h in range(KV_GROUP)], axis=1)

    def ctx_scores(g, qt):
        return _dot(kc_ref[g, :, 0:HEAD_DIM], qt)

    def finish(g, acc):
        ot = acc[0:HEAD_DIM] * (1.0 / acc[HEAD_DIM:HEAD_DIM + 1])
        for k in range(KV_GROUP):
            h = KV_GROUP * g + k
            o_ref[:, HEAD_DIM * h:HEAD_DIM * (h + 1)] = ot[:, k * tq:(k + 1) * tq].T.astype(o_ref.dtype)

    def exact(g):
        qt = queries(g)
        s = ctx_scores(g, qt)
        m = jnp.max(s, axis=0, keepdims=True)
        acc = _dot(vtc_ref[g], jnp.exp2(s - m).astype(BF16))

        def body(c, carry):
            m, acc = carry
            off = pl.multiple_of(c * tk, tk)
            s = _dot(kl_ref[g, pl.ds(off, tk), 0:HEAD_DIM], qt)
            m_new = jnp.maximum(m, jnp.max(s, axis=0, keepdims=True))
            p = jnp.exp2(s - m_new).astype(BF16)
            return m_new, jnp.exp2(m - m_new) * acc + _dot(vtl_ref[g, :, pl.ds(off, tk)], p)

        if n_lat_chunks:
            m, acc = lax.fori_loop(0, n_lat_chunks, body, (m, acc))
        return acc

    if not n_lat_chunks:
        for g in range(N_KV):
            finish(g, exact(g))
        return

    row = lax.broadcasted_iota(jnp.int32, (BF16_ROWS, n), 0)
    pad = jnp.zeros((K_COLS - HEAD_DIM - BF16_ROWS, n), BF16)

    def start(g):
        qt = queries(g)
        s = ctx_scores(g, qt)
        running = jnp.max(s, axis=0, keepdims=True)
        shift = running.astype(BF16).astype(F32)
        acc = _dot(vtc_ref[g], jnp.exp2(s - shift).astype(BF16))
        return dict(qt=qt, running=running, shift=shift, shifts=[shift, shift], acc=acc,
                    largest=jnp.ones((1, n), F32))

    def issue(g, c, st):
        new_shift = st["shifts"].pop(0)
        shift_rows = jnp.where(row == 0, -new_shift, 0.0).astype(BF16)
        qa = jnp.concatenate([st["qt"], shift_rows, pad], axis=0)
        st["issued"] = new_shift, _dot(kl_ref[g, pl.ds(c * tk, tk), :], qa)

    def consume(g, c, st):
        new_shift, s = st.pop("issued")
        p = jnp.exp2(s)
        pv = _dot(vtl_ref[g, :, pl.ds(c * tk, tk)], p.astype(BF16))
        st["acc"] = st["acc"] * jnp.exp2(st["shift"] - new_shift) + pv
        st["shift"] = new_shift
        pmax = jnp.max(p, axis=0, keepdims=True)
        st["largest"] = jnp.maximum(st["largest"], pmax)
        st["running"] = jnp.maximum(st["running"], new_shift + jnp.log2(pmax))
        st["shifts"].append(st["running"].astype(BF16).astype(F32))

    states = [start(g) for g in range(N_KV)]
    items = [(g, c) for c in range(n_lat_chunks) for g in range(N_KV)]
    issue(*items[0], states[items[0][0]])
    for k, (g, c) in enumerate(items):
        if k + 1 < len(items):
            gn, cn = items[k + 1]
            issue(gn, cn, states[gn])
        consume(g, c, states[g])
    overflow = jnp.max(jnp.maximum(states[0]["largest"], states[1]["largest"])) > MAX_SHIFTED_WEIGHT

    @pl.when(overflow)
    def _():
        for g in range(N_KV):
            finish(g, exact(g))

    @pl.when(jnp.logical_not(overflow))
    def _():
        for g in range(N_KV):
            finish(g, states[g]["acc"])


def _attention(qt, k_ctx, vt_ctx, k_lat, vt_lat, *, n_seq, q_len, ctx_len, lat_len):
    tq = min(256, q_len)
    tk = min(512, lat_len // 2) if lat_len else 0
    n_lat_chunks = lat_len // tk if lat_len else 0
    assert n_lat_chunks % 2 == 0
    nq = q_len // tq
    kern = functools.partial(_attn_kernel, tq=tq, tk=tk, n_lat_chunks=n_lat_chunks)
    in_specs = [pl.BlockSpec((None, N_HEADS, HEAD_DIM, tq), lambda b, i: (b, 0, 0, i)),
                pl.BlockSpec((None, N_KV, ctx_len, K_COLS), lambda b, i: (b, 0, 0, 0)),
                pl.BlockSpec((None, N_KV, V_ROWS, ctx_len), lambda b, i: (b, 0, 0, 0))]
    args = [qt, k_ctx, vt_ctx]
    scratch = []
    if lat_len:
        in_specs += [pl.BlockSpec((None, N_KV, lat_len, K_COLS), lambda b, i: (b, 0, 0, 0)),
                     pl.BlockSpec((None, N_KV, V_ROWS, lat_len), lambda b, i: (b, 0, 0, 0))]
        args += [k_lat, vt_lat]
    return pl.pallas_call(
        kern,
        grid=(n_seq, nq),
        in_specs=in_specs,
        out_specs=pl.BlockSpec((tq, D_ATTN), lambda b, i: (b * nq + i, 0)),
        out_shape=jax.ShapeDtypeStruct((n_seq * q_len, D_ATTN), BF16),
        scratch_shapes=scratch,
        compiler_params=_params("arbitrary", "arbitrary"),
        name="attn_latent" if lat_len else "attn_context",
    )(*args)


def _lru_sequence(x_ref, g_ref, o_ref, h0, consts, scr, *, n_groups):
    cw_ref, cb_ref, wg_ref, bg_ref, hnsp = consts
    xe_ref, af_ref, bf_ref, ar_ref, br_ref = scr
    rows = SUBLANES * n_groups
    width = x_ref.shape[1]
    chunk = min(256, rows)
    n_chunks = rows // chunk

    xe_ref[pl.ds(2 * SUBLANES, rows), :] = x_ref[...]
    xe_ref[pl.ds(0, SUBLANES), :] = _shift_down(x_ref[pl.ds(rows - 2 * SUBLANES, SUBLANES), :])
    xe_ref[pl.ds(SUBLANES, SUBLANES), :] = _shift_down(x_ref[pl.ds(rows - SUBLANES, SUBLANES), :])
    xe_ref[pl.ds(rows + 2 * SUBLANES, SUBLANES), :] = _shift_up(x_ref[pl.ds(0, SUBLANES), :])

    def gates(c, _):
        r0 = pl.multiple_of(c * chunk, chunk)
        xc = cb_ref[...]
        for k in range(4):
            xc = xc + cw_ref[k:k + 1, :] * xe_ref[pl.ds(r0 + SUBLANES * k, chunk), :]
        t = jnp.tanh(_dot(xc.astype(BF16), wg_ref[...]) + bg_ref[...])
        hx = 0.5 * xc
        for d, (a_ref, b_ref) in enumerate(((af_ref, bf_ref), (ar_ref, br_ref))):
            t_r = t[:, (2 * d) * width:(2 * d + 1) * width]
            t_i = t[:, (2 * d + 1) * width:(2 * d + 2) * width]
            log_a = hnsp[d] * t_r + hnsp[d]
            a = jnp.exp(log_a)
            y = jnp.tanh(-log_a) * (1.0 + a * a)
            b = jnp.where(y > 0.0, y * lax.rsqrt(y), 0.0) * (hx * t_i + hx)
            a_ref[pl.ds(r0, chunk), :] = a
            b_ref[pl.ds(r0, chunk), :] = b
        return 0

    lax.fori_loop(0, n_chunks, gates, 0)

    blk = min(SCAN_BLOCK, n_groups)
    blk_rows = blk * SUBLANES
    n_blocks = n_groups // blk

    def scan_block(a_ref, b_ref, r0, h, p, order):
        a_blk = a_ref[pl.ds(r0, blk_rows), :]
        b_blk = b_ref[pl.ds(r0, blk_rows), :]
        hs, ps = [None] * blk, [None] * blk
        for k in order:
            a = a_blk[k * SUBLANES:(k + 1) * SUBLANES]
            h = a * h + b_blk[k * SUBLANES:(k + 1) * SUBLANES]
            p = a * p
            hs[k], ps[k] = h, p
        b_ref[pl.ds(r0, blk_rows), :] = jnp.concatenate(hs, axis=0)
        a_ref[pl.ds(r0, blk_rows), :] = jnp.concatenate(ps, axis=0)
        return h, p

    def scan(i, carry):
        hf, pf, hr, pr = carry
        rf = pl.multiple_of(i * blk_rows, blk_rows)
        rr = pl.multiple_of((n_blocks - 1 - i) * blk_rows, blk_rows)
        hf, pf = scan_block(af_ref, bf_ref, rf, hf, pf, range(blk))
        hr, pr = scan_block(ar_ref, br_ref, rr, hr, pr, range(blk - 1, -1, -1))
        return hf, pf, hr, pr

    zero = jnp.zeros((SUBLANES, width), F32)
    one = jnp.ones((SUBLANES, width), F32)
    hf, pf, hr, pr = lax.fori_loop(0, n_blocks, scan, (zero, one, zero, one))

    e = h0[0]
    init_f = []
    for s in range(SUBLANES):
        init_f.append(e)
        e = hf[s:s + 1, :] + pf[s:s + 1, :] * e
    fin_f = e
    e = h0[1]
    init_r = [None] * SUBLANES
    for s in range(SUBLANES - 1, -1, -1):
        init_r[s] = e
        e = hr[s:s + 1, :] + pr[s:s + 1, :] * e
    fin_r = e

    if o_ref is not None:
        init_f = jnp.concatenate(init_f, axis=0)[None]
        init_r = jnp.concatenate(init_r, axis=0)[None]

        def combine(c, _):
            r0 = pl.multiple_of(c * chunk, chunk)
            sl = pl.ds(r0, chunk)
            shp = (chunk // SUBLANES, SUBLANES, width)
            h = (bf_ref[sl, :].reshape(shp) + af_ref[sl, :].reshape(shp) * init_f
                 + br_ref[sl, :].reshape(shp) + ar_ref[sl, :].reshape(shp) * init_r)
            gt = g_ref[sl, :]
            gelu = 0.5 * gt * (1.0 + jnp.tanh(0.7978845608028654 * (gt + 0.044715 * gt * gt * gt)))
            o_ref[sl, :] = (h.reshape(chunk, width) * gelu).astype(o_ref.dtype)
            return 0

        lax.fori_loop(0, n_chunks, combine, 0)
    return fin_f, fin_r


def _lru_kernel(*refs, ctx_groups, lat_groups, ctx_out):
    xc_ref, gc_ref, xl_ref, gl_ref, cw_ref, cb_ref, wg_ref, bg_ref, lam_ref = refs[:9]
    n_out = 2 if ctx_out else 1
    outs = refs[9:9 + n_out]
    scr = refs[9 + n_out:]
    z = -lam_ref[...]
    softplus = jnp.maximum(z, 0.0) + jnp.log1p(jnp.exp(-jnp.abs(z)))
    hnsp = [(-0.5 * LRU_C) * softplus[d:d + 1, :] for d in range(2)]
    consts = (cw_ref, cb_ref, wg_ref, bg_ref, hnsp)
    zero = jnp.zeros((1, xc_ref.shape[1]), F32)
    fin = _lru_sequence(xc_ref, gc_ref, outs[1] if ctx_out else None, (zero, zero), consts, scr,
                        n_groups=ctx_groups)
    _lru_sequence(xl_ref, gl_ref, outs[0], fin, consts, scr, n_groups=lat_groups)


def _lru(x_ctx, g_ctx, x_lat, g_lat, conv_w, conv_b, wg, bg, lam, *, n_seq, ctx_len, lat_len, ctx_out):
    w = LRU_SLAB
    n_slab = D_LRU // w
    kern = functools.partial(_lru_kernel, ctx_groups=ctx_len // SUBLANES,
                             lat_groups=lat_len // SUBLANES, ctx_out=ctx_out)
    ctx_spec = pl.BlockSpec((ctx_len, w), lambda b, c: (b, c))
    lat_spec = pl.BlockSpec((lat_len, w), lambda b, c: (b, c))
    out_specs = [lat_spec]
    out_shape = [jax.ShapeDtypeStruct((n_seq * lat_len, D_LRU), BF16)]
    if ctx_out:
        out_specs.append(ctx_spec)
        out_shape.append(jax.ShapeDtypeStruct((n_seq * ctx_len, D_LRU), BF16))
    max_rows = max(ctx_len, lat_len)
    return pl.pallas_call(
        kern,
        grid=(n_seq, n_slab),
        in_specs=[ctx_spec, ctx_spec, lat_spec, lat_spec,
                  pl.BlockSpec((4, w), lambda b, c: (0, c)),
                  pl.BlockSpec((1, w), lambda b, c: (0, c)),
                  pl.BlockSpec((None, w, 4 * w), lambda b, c: (c, 0, 0)),
                  pl.BlockSpec((None, 1, 4 * w), lambda b, c: (c, 0, 0)),
                  pl.BlockSpec((2, w), lambda b, c: (0, c))],
        out_specs=out_specs,
        out_shape=out_shape,
        scratch_shapes=[pltpu.VMEM((max_rows + 3 * SUBLANES, w), F32)]
                       + [pltpu.VMEM((max_rows, w), F32)] * 4,
        compiler_params=_params("arbitrary", "arbitrary"),
        name="rglru_bidir",
    )(x_ctx, g_ctx, x_lat, g_lat, conv_w, conv_b, wg, bg, lam)


def _outproj_kernel(a_ref, r_ref, x_ref, gate_ref, nw_ref, w_ref, o_ref):
    n_sub = _n_sub(x_ref.shape[0])
    sub = x_ref.shape[0] // n_sub

    def project(r0):
        rows = pl.ds(r0, sub)
        cat = jnp.concatenate([_rms(a_ref[rows, :].astype(F32)), _rms(r_ref[rows, :].astype(F32))],
                              axis=1) * nw_ref[...]
        return _dot(cat.astype(BF16), w_ref[...])

    y = project(0)
    for s in range(n_sub):
        y_next = project((s + 1) * sub) if s + 1 < n_sub else None
        rows = pl.ds(s * sub, sub)
        o_ref[rows, :] = x_ref[rows, :] + gate_ref[...] * y
        y = y_next


def _outproj(a, r, x, mod, nw, w_out, *, n_seq, seq_len, shared_mod, mod_row, layer):
    tm = min(1024, seq_len)
    tps = seq_len // tm
    rows = n_seq * seq_len
    gate_idx = (lambda i: (mod_row, 0, 2)) if shared_mod else (lambda i: (mod_row + i // tps, 0, 2))
    return pl.pallas_call(
        _outproj_kernel,
        grid=(rows // tm,),
        in_specs=[pl.BlockSpec((tm, D_ATTN), lambda i: (i, 0)),
                  pl.BlockSpec((tm, D_LRU), lambda i: (i, 0)),
                  pl.BlockSpec((tm, D_MODEL), lambda i: (i, 0)),
                  pl.BlockSpec((None, 1, D_MODEL), gate_idx),
                  _resident((1, D_MODEL)),
                  _resident((D_MODEL, D_MODEL), layer)],
        out_specs=pl.BlockSpec((tm, D_MODEL), lambda i: (i, 0)),
        out_shape=jax.ShapeDtypeStruct((rows, D_MODEL), F32),
        compiler_params=_params("arbitrary"),
        name="out_proj",
    )(a, r, x, mod, nw, w_out)


def _ffn_kernel(*refs, tm, tps, final_norm):
    x_ref, xp_ref, xn_ref, mod_ref, wup_ref, cw_ref, cb_ref, wdn_ref = refs[:8]
    if final_norm:
        fnw_ref, o_ref, hext_ref, act_ref, ua_ref, ub_ref, slab_ref = refs[8:]
    else:
        o_ref, hext_ref, act_ref, ua_ref, ub_ref = refs[8:]
    assert (D_FF // FF_CHUNK) % 2 == 1
    jt = pl.program_id(0) % tps
    shift = mod_ref[:, 0:D_MODEL]
    scale = mod_ref[:, D_MODEL:2 * D_MODEL]
    gate = mod_ref[:, 2 * D_MODEL:3 * D_MODEL]

    def norm_mod(x):
        return _rms(x) * (1.0 + scale) + shift

    x = x_ref[...]
    hext_ref[pl.ds(BF16_ROWS, tm), :] = norm_mod(x).astype(BF16)
    hp = norm_mod(xp_ref[...])
    hp = jnp.where(jt == 0, _shift_down(hp), hp)
    hn = norm_mod(xn_ref[...])
    hn = jnp.where(jt == tps - 1, _shift_up(hn), hn)
    zeros = jnp.zeros_like(hp)
    hext_ref[pl.ds(0, BF16_ROWS), :] = jnp.concatenate([zeros, hp], axis=0).astype(BF16)
    hext_ref[pl.ds(BF16_ROWS + tm, BF16_ROWS), :] = jnp.concatenate([hn, zeros], axis=0).astype(BF16)

    def up(c, u_ref):
        for k in range(2):
            off = pl.multiple_of(k * D_FF + c * FF_CHUNK, FF_CHUNK)
            u_ref[k] = _dot(hext_ref[...], wup_ref[:, pl.ds(off, FF_CHUNK)])

    def conv(u_ref, k, off):
        sl = pl.ds(off, FF_CHUNK)
        return (cw_ref[0:1, sl] * u_ref[k, pl.ds(SUBLANES, tm), :]
                + cw_ref[1:2, sl] * u_ref[k, pl.ds(2 * SUBLANES, tm), :]
                + cw_ref[2:3, sl] * u_ref[k, pl.ds(3 * SUBLANES, tm), :]
                + cb_ref[:, sl])

    def activate(c, u_ref):
        og = pl.multiple_of(c * FF_CHUNK, FF_CHUNK)
        ov = pl.multiple_of(D_FF + c * FF_CHUNK, FF_CHUNK)
        yg = conv(u_ref, 0, og)
        yv = conv(u_ref, 1, ov)
        hg = 0.5 * yg
        act_ref[:, pl.ds(og, FF_CHUNK)] = ((hg + hg * jnp.tanh(hg)) * yv).astype(BF16)

    n_chunks = D_FF // FF_CHUNK
    up(0, ua_ref)

    def pair(j, _):
        up(2 * j + 1, ub_ref)
        activate(2 * j, ua_ref)
        up(2 * j + 2, ua_ref)
        activate(2 * j + 1, ub_ref)
        return 0

    for j in range((n_chunks - 1) // 2):
        pair(j, 0)
    activate(n_chunks - 1, ua_ref)
    out = x + gate * _dot(act_ref[...], wdn_ref[...])
    if not final_norm:
        o_ref[...] = out
        return
    out = _rms(out) * fnw_ref[...]
    for k in range(D_MODEL // LANES):
        slab_ref[k] = out[:, LANES * k:LANES * (k + 1)]
    for s in range(SUBLANES):
        for k in range(D_MODEL // LANES):
            o_ref[s, :, LANES * k:LANES * (k + 1)] = slab_ref[k, pl.ds(s, tm // SUBLANES, stride=SUBLANES), :]


def _ffn(x, mod, w_up, conv_w, conv_b, w_down, final_w, *, n_seq, seq_len, shared_mod, mod_row, layer):
    tm = min(1024, seq_len)
    tps = seq_len // tm
    rows = n_seq * seq_len
    gps = seq_len // SUBLANES
    gpt = tm // SUBLANES
    mod_idx = (lambda i: (mod_row, 0, 1)) if shared_mod else (lambda i: (mod_row + i // tps, 0, 1))

    def prev_idx(i):
        jt = i % tps
        return ((i // tps) * gps + jnp.where(jt == 0, gps - 1, jt * gpt - 1), 0)

    def next_idx(i):
        jt = i % tps
        return ((i // tps) * gps + jnp.where(jt == tps - 1, 0, (jt + 1) * gpt), 0)

    final_norm = final_w is not None
    in_specs = [pl.BlockSpec((tm, D_MODEL), lambda i: (i, 0)),
                pl.BlockSpec((SUBLANES, D_MODEL), prev_idx),
                pl.BlockSpec((SUBLANES, D_MODEL), next_idx),
                pl.BlockSpec((None, 1, 3 * D_MODEL), mod_idx),
                _resident((D_MODEL, 2 * D_FF), layer),
                _resident((3, 2 * D_FF), layer),
                _resident((1, 2 * D_FF), layer),
                _resident((D_FF, D_MODEL), layer)]
    args = [x, x, x, mod, w_up, conv_w, conv_b, w_down]
    if final_norm:
        in_specs.append(_resident((1, D_MODEL)))
        args.append(final_w)
    kern = functools.partial(_ffn_kernel, tm=tm, tps=tps, final_norm=final_norm)
    scratch = ([pltpu.VMEM((tm + 2 * BF16_ROWS, D_MODEL), BF16), pltpu.VMEM((tm, D_FF), BF16)]
               + [pltpu.VMEM((2, tm + 2 * BF16_ROWS, FF_CHUNK), F32)] * 2)
    if final_norm:
        out_spec = pl.BlockSpec((None, SUBLANES, tm // SUBLANES, D_MODEL),
                                lambda i: (i // tps, 0, i % tps, 0))
        out_shape = jax.ShapeDtypeStruct((n_seq, SUBLANES, gps, D_MODEL), F32)
        scratch.append(pltpu.VMEM((D_MODEL // LANES, tm, LANES), F32))
    else:
        out_spec = pl.BlockSpec((tm, D_MODEL), lambda i: (i, 0))
        out_shape = jax.ShapeDtypeStruct((rows, D_MODEL), F32)
    return pl.pallas_call(
        kern,
        grid=(rows // tm,),
        in_specs=in_specs,
        out_specs=out_spec,
        out_shape=out_shape,
        scratch_shapes=scratch,
        compiler_params=_params("arbitrary"),
        name="conv_ffn_final" if final_norm else "conv_ffn",
    )(*args)


def _to_segments(x):
    b, t, c = x.shape
    return x.reshape(b, SUBLANES, t // SUBLANES, c).transpose(0, 2, 1, 3).reshape(b * t, c)


def _from_segments(y, b, t):
    return y.reshape(b, t // SUBLANES, SUBLANES, -1).transpose(0, 2, 1, 3).reshape(b, t, -1)


def _rope_tables(seq_len):
    r = jnp.arange(seq_len)
    t = (r % SUBLANES) * (seq_len // SUBLANES) + r // SUBLANES
    row = (t // GRID_W).astype(F32)
    col = (t % GRID_W).astype(F32)
    pairs = HEAD_DIM // 4
    inv = ROPE_THETA ** (-jnp.arange(pairs, dtype=F32) / pairs)
    ang = jnp.concatenate([row[:, None] * inv, col[:, None] * inv], axis=-1)
    cos, sin = jnp.cos(ang), jnp.sin(ang)
    return (jnp.concatenate([cos, cos, cos, cos], axis=-1),
            jnp.concatenate([-sin, sin, -sin, sin], axis=-1))


def _gate_weights(wa, wx, ba, bx):
    n_slab = D_LRU // LRU_SLAB
    per = LRU_SLAB // wa.shape[-1]

    def slab_diag(w):
        w = w.reshape(n_slab, per, w.shape[-2], w.shape[-1])
        eye = jnp.eye(per, dtype=w.dtype)
        return jnp.einsum('spcd,pq->spcqd', w, eye).reshape(n_slab, LRU_SLAB, LRU_SLAB)

    wg = jnp.concatenate([slab_diag(wa[0]), slab_diag(wx[0]), slab_diag(wa[1]), slab_diag(wx[1])],
                         axis=-1)
    bg = jnp.stack([ba[0], bx[0], ba[1], bx[1]], axis=0).reshape(4, n_slab, LRU_SLAB)
    bg = bg.transpose(1, 0, 2).reshape(n_slab, 1, 4 * LRU_SLAB)
    return (0.5 * wg).astype(BF16), 0.5 * bg


def kernel(x, c, ctx, c_ctx, w_ada, b_ada, w_in, q_norm_w, k_norm_w, lru_conv_w, lru_conv_b, lru_wa,
           lru_ba, lru_wx, lru_bx, lru_lambda, attn_out_norm_w, lru_out_norm_w, w_out, ffn_w_up,
           ffn_conv_w, ffn_conv_b, ffn_w_down, final_norm_w):
    batch, seq, _ = x.shape
    ctx_len = ctx.shape[1]
    depth = w_in.shape[0]

    mod_rows = -(-(batch + 1) // SUBLANES) * SUBLANES
    cvec = jnp.zeros((mod_rows, D_MODEL), F32).at[:batch].set(c).at[batch].set(c_ctx)
    mods = _modulation(cvec, w_ada, b_ada)

    cos, sin = _rope_tables(seq)
    ones = jnp.ones((ctx_len, LANES), F32)
    bd = jnp.kron(jnp.eye(QK_W // HEAD_DIM, dtype=F32), jnp.ones((HEAD_DIM, HEAD_DIM), F32)).astype(BF16)

    x_lat = _to_segments(x)
    x_ctx = _to_segments(ctx)
    mods = mods.reshape(depth * mod_rows, 1, 6 * D_MODEL)
    w_in_b = w_in.astype(BF16)
    w_out_b = w_out.astype(BF16)
    w_up_b = ffn_w_up.astype(BF16)
    w_dn_b = ffn_w_down.astype(BF16)
    conv_b = ffn_conv_b.reshape(depth, 1, 2 * D_FF)

    for l in range(depth):
        ctx_out = l < depth - 1
        last = l == depth - 1
        lat = dict(n_seq=batch, seq_len=seq, shared_mod=False, mod_row=l * mod_rows, layer=l)
        cx = dict(n_seq=batch, seq_len=ctx_len, shared_mod=True, mod_row=l * mod_rows + batch, layer=l)
        nw_qk = jnp.concatenate([jnp.tile(q_norm_w[l], D_ATTN // HEAD_DIM),
                                 jnp.tile(k_norm_w[l], KV_W // HEAD_DIM)]).reshape(1, QK_W)
        qt_l, k_l, vt_l, xr_l, g_l = _inproj(x_lat, mods, w_in_b, bd, nw_qk, cos, sin,
                                             use_rope=True, **lat)
        qt_c, k_c, vt_c, xr_c, g_c = _inproj(x_ctx, mods, w_in_b, bd, nw_qk, ones, ones,
                                             use_rope=False, **cx)

        a_l = _attention(qt_l, k_c, vt_c, k_l, vt_l, n_seq=batch, q_len=seq, ctx_len=ctx_len,
                         lat_len=seq)
        wg, bg = _gate_weights(lru_wa[l], lru_wx[l], lru_ba[l], lru_bx[l])
        r = _lru(xr_c, g_c, xr_l, g_l, lru_conv_w[l], lru_conv_b[l].reshape(1, D_LRU), wg, bg,
                 lru_lambda[l], n_seq=batch, ctx_len=ctx_len, lat_len=seq, ctx_out=ctx_out)

        nw_out = jnp.concatenate([attn_out_norm_w[l], lru_out_norm_w[l]]).reshape(1, D_MODEL)
        x_lat = _outproj(a_l, r[0], x_lat, mods, nw_out, w_out_b, **lat)
        x_lat = _ffn(x_lat, mods, w_up_b, ffn_conv_w, conv_b, w_dn_b,
                     final_norm_w.reshape(1, D_MODEL) if last else None, **lat)
        if ctx_out:
            a_c = _attention(qt_c, k_c, vt_c, None, None, n_seq=batch, q_len=ctx_len,
                             ctx_len=ctx_len, lat_len=0)
            x_ctx = _outproj(a_c, r[1], x_ctx, mods, nw_out, w_out_b, **cx)
            x_ctx = _ffn(x_ctx, mods, w_up_b, ffn_conv_w, conv_b, w_dn_b, None, **cx)
    return x_lat.reshape(batch, seq, D_MODEL)
```

```python
import functools

import jax
import jax.numpy as jnp
import numpy as np
from jax import lax
from jax.experimental import pallas as pl
from jax.experimental.pallas import tpu as pltpu

F32 = jnp.float32
BF16 = jnp.bfloat16

D_MODEL = 1024
D_ATTN = 512
D_LRU = 512
KV_W = 128
HEAD_DIM = 64
N_HEADS = 8
N_KV = 2
KV_GROUP = 4
LOG2E = 1.4426950408889634
GELU_C0 = 0.7978845608028654
GELU_C1 = GELU_C0 * 0.044715
QK_W = D_ATTN + KV_W
D_IN = D_ATTN + 2 * KV_W + 2 * D_LRU
D_FF = 2816
LRU_C = 8.0
GRID_W = 64
ROPE_THETA = 10000.0
EPS = 1e-6
SUBLANES = 8
LANES = 128
BF16_ROWS = 16
V_ROWS = HEAD_DIM + BF16_ROWS
K_COLS = LANES
MAX_SHIFTED_WEIGHT = 2.0 ** 60
FF_CHUNK = 256
SUB_ROWS = 256
LRU_SLAB = 256
SCAN_BLOCK = 8
VMEM_LIMIT = 58 * 1024 * 1024


def _dot(a, b):
    return jnp.dot(a, b, preferred_element_type=F32)


def _sigmoid(z):
    return 0.5 * jnp.tanh(0.5 * z) + 0.5


def _rms(x):
    return x * lax.rsqrt(jnp.mean(x * x, axis=-1, keepdims=True) + EPS)


def _shift_down(g):
    rows = lax.broadcasted_iota(jnp.int32, g.shape, 0)
    return jnp.where(rows == 0, 0.0, pltpu.roll(g, 1, 0))


def _shift_up(g):
    rows = lax.broadcasted_iota(jnp.int32, g.shape, 0)
    return jnp.where(rows == SUBLANES - 1, 0.0, pltpu.roll(g, SUBLANES - 1, 0))


def _n_sub(tile_rows):
    return max(2, tile_rows // SUB_ROWS)


def _params(*sem):
    return pltpu.CompilerParams(dimension_semantics=sem, vmem_limit_bytes=VMEM_LIMIT)


def _resident(shape, layer=None):
    nd = len(shape)
    if layer is None:
        return pl.BlockSpec(shape, lambda *_: (0,) * nd, pipeline_mode=pl.Buffered(1))
    return pl.BlockSpec((None,) + tuple(shape), lambda *_: (layer,) + (0,) * nd,
                        pipeline_mode=pl.Buffered(1))


def _mod_kernel(c_ref, w_ref, b_ref, o_ref):
    c = c_ref[...]
    s = (c * _sigmoid(c)).astype(BF16)
    o_ref[...] = _dot(s, w_ref[...].astype(BF16)) + b_ref[...]


def _modulation(cvec, w_ada, b_ada, tn=1536):
    depth, d, n = w_ada.shape
    rows = cvec.shape[0]
    return pl.pallas_call(
        _mod_kernel,
        grid=(depth, n // tn),
        in_specs=[pl.BlockSpec((rows, d), lambda l, j: (0, 0)),
                  pl.BlockSpec((None, d, tn), lambda l, j: (l, 0, j)),
                  pl.BlockSpec((None, 1, tn), lambda l, j: (l, 0, j))],
        out_specs=pl.BlockSpec((None, rows, tn), lambda l, j: (l, 0, j)),
        out_shape=jax.ShapeDtypeStruct((depth, rows, n), F32),
        compiler_params=_params("arbitrary", "arbitrary"),
        name="adaln_mod",
    )(cvec, w_ada, b_ada.reshape(depth, 1, n))


def _inproj_kernel(x_ref, mod_ref, w_ref, bd_ref, nw_ref, cos_ref, sin_ref,
                   qt_ref, k_ref, vt_ref, xl_ref, g_ref, *, use_rope):
    shift = mod_ref[:, 0:D_MODEL]
    scale = mod_ref[:, D_MODEL:2 * D_MODEL]
    n_sub = _n_sub(x_ref.shape[0])
    sub = x_ref.shape[0] // n_sub
    w = w_ref[...].astype(BF16)

    def project(r0):
        h = _rms(x_ref[pl.ds(r0, sub), :]) * (1.0 + scale) + shift
        y = _dot(h.astype(BF16), w)
        qk = y[:, 0:QK_W]
        return y, _dot((qk * qk).astype(BF16), bd_ref[...])

    def finish(r0, y_ss):
        y, ss = y_ss
        rows = pl.ds(r0, sub)
        qk = y[:, 0:QK_W]
        qk = qk * lax.rsqrt(ss * (1.0 / HEAD_DIM) + EPS) * nw_ref[...]
        if use_rope:
            lane = lax.broadcasted_iota(jnp.int32, (sub, LANES), 1)
            first_half = (lane % HEAD_DIM) < (HEAD_DIM // 2)
            cos = cos_ref[rows, :]
            sin = sin_ref[rows, :]
            cols = []
            for k in range(QK_W // LANES):
                xc = qk[:, LANES * k:LANES * (k + 1)]
                other = jnp.where(first_half, pltpu.roll(xc, LANES - HEAD_DIM // 2, 1),
                                  pltpu.roll(xc, HEAD_DIM // 2, 1))
                cols.append(xc * cos + other * sin)
            qk = jnp.concatenate(cols, axis=1)
        qt = (qk[:, 0:D_ATTN] * (LOG2E * HEAD_DIM ** -0.5)).T
        for h in range(N_HEADS):
            qt_ref[h, :, rows] = qt[HEAD_DIM * h:HEAD_DIM * (h + 1)].astype(BF16)
        kk = qk[:, D_ATTN:QK_W]
        lane = lax.broadcasted_iota(jnp.int32, kk.shape, 1)
        one_hot = jnp.where(lane == HEAD_DIM, 1.0, 0.0)
        for g in range(N_KV):
            kg = kk if g == 0 else pltpu.roll(kk, HEAD_DIM, 1)
            k_ref[g, rows, :] = jnp.where(lane < HEAD_DIM, kg, one_hot).astype(BF16)
        vt = y[:, QK_W:QK_W + KV_W].T.astype(BF16)
        ones = jnp.ones((V_ROWS - HEAD_DIM, sub), BF16)
        for g in range(N_KV):
            vt_ref[g, :, rows] = jnp.concatenate([vt[HEAD_DIM * g:HEAD_DIM * (g + 1)], ones], axis=0)
        xl_ref[rows, :] = y[:, QK_W + KV_W:QK_W + KV_W + D_LRU]
        gt = y[:, QK_W + KV_W + D_LRU:D_IN]
        hg = 0.5 * gt
        g_ref[rows, :] = hg + hg * jnp.tanh(gt * (GELU_C0 + GELU_C1 * (gt * gt)))

    y = project(0)
    for s in range(n_sub):
        y_next = project((s + 1) * sub) if s + 1 < n_sub else None
        finish(s * sub, y)
        y = y_next


def _inproj(x, mod, w_in, bd, nw, cos, sin, *, n_seq, seq_len, shared_mod, mod_row, layer,
            use_rope):
    tm = min(1024, seq_len)
    tps = seq_len // tm
    rows = n_seq * seq_len
    mod_idx = (lambda i: (mod_row, 0, 0)) if shared_mod else (lambda i: (mod_row + i // tps, 0, 0))
    kern = functools.partial(_inproj_kernel, use_rope=use_rope)
    return pl.pallas_call(
        kern,
        grid=(rows // tm,),
        in_specs=[pl.BlockSpec((tm, D_MODEL), lambda i: (i, 0)),
                  pl.BlockSpec((None, 1, 2 * D_MODEL), mod_idx),
                  _resident((D_MODEL, D_IN), layer),
                  _resident((QK_W, QK_W)),
                  _resident((1, QK_W)),
                  pl.BlockSpec((tm, LANES), lambda i: (i % tps, 0)),
                  pl.BlockSpec((tm, LANES), lambda i: (i % tps, 0))],
        out_specs=[pl.BlockSpec((None, N_HEADS, HEAD_DIM, tm), lambda i: (i // tps, 0, 0, i % tps)),
                   pl.BlockSpec((None, N_KV, tm, K_COLS), lambda i: (i // tps, 0, i % tps, 0)),
                   pl.BlockSpec((None, N_KV, V_ROWS, tm), lambda i: (i // tps, 0, 0, i % tps)),
                   pl.BlockSpec((tm, D_LRU), lambda i: (i, 0)),
                   pl.BlockSpec((tm, D_LRU), lambda i: (i, 0))],
        out_shape=[jax.ShapeDtypeStruct((n_seq, N_HEADS, HEAD_DIM, seq_len), BF16),
                   jax.ShapeDtypeStruct((n_seq, N_KV, seq_len, K_COLS), BF16),
                   jax.ShapeDtypeStruct((n_seq, N_KV, V_ROWS, seq_len), BF16),
                   jax.ShapeDtypeStruct((rows, D_LRU), F32),
                   jax.ShapeDtypeStruct((rows, D_LRU), F32)],
        compiler_params=_params("arbitrary"),
        name="in_proj_rope" if use_rope else "in_proj",
    )(x, mod, w_in, bd, nw, cos, sin)


def _attn_kernel(*refs, tq, tk, n_lat_chunks):
    if n_lat_chunks:
        qt_ref, kc_ref, vtc_ref, kl_ref, vtl_ref, o_ref = refs
    else:
        qt_ref, kc_ref, vtc_ref, o_ref = refs
    n = KV_GROUP * tq

    def queries(g):
        return jnp.concatenate([qt_ref[KV_GROUP * g + h] for h in range(KV_GROUP)], axis=1)

    def ctx_scores(g, qt):
        return _dot(kc_ref[g, :, 0:HEAD_DIM], qt)

    def finish(g, acc):
        ot = acc[0:HEAD_DIM] * (1.0 / acc[HEAD_DIM:HEAD_DIM + 1])
        for k in range(KV_GROUP):
            h = KV_GROUP * g + k
            o_ref[:, HEAD_DIM * h:HEAD_DIM * (h + 1)] = ot[:, k * tq:(k + 1) * tq].T.astype(o_ref.dtype)

    def exact(g):
        qt = queries(g)
        s = ctx_scores(g, qt)
        m = jnp.max(s, axis=0, keepdims=True)
        acc = _dot(vtc_ref[g], jnp.exp2(s - m).astype(BF16))

        def body(c, carry):
            m, acc = carry
            off = pl.multiple_of(c * tk, tk)
            s = _dot(kl_ref[g, pl.ds(off, tk), 0:HEAD_DIM], qt)
            m_new = jnp.maximum(m, jnp.max(s, axis=0, keepdims=True))
            p = jnp.exp2(s - m_new).astype(BF16)
            return m_new, jnp.exp2(m - m_new) * acc + _dot(vtl_ref[g, :, pl.ds(off, tk)], p)

        if n_lat_chunks:
            m, acc = lax.fori_loop(0, n_lat_chunks, body, (m, acc))
        return acc

    if not n_lat_chunks:
        for g in range(N_KV):
            finish(g, exact(g))
        return

    row = lax.broadcasted_iota(jnp.int32, (BF16_ROWS, n), 0)
    pad = jnp.zeros((K_COLS - HEAD_DIM - BF16_ROWS, n), BF16)

    def start(g):
        qt = queries(g)
        s = ctx_scores(g, qt)
        running = jnp.max(s, axis=0, keepdims=True)
        shift = running.astype(BF16).astype(F32)
        acc = _dot(vtc_ref[g], jnp.exp2(s - shift).astype(BF16))
        return dict(qt=qt, running=running, shift=shift, shifts=[shift, shift], acc=acc,
                    largest=jnp.ones((1, n), F32))

    def issue(g, c, st):
        new_shift = st["shifts"].pop(0)
        shift_rows = jnp.where(row == 0, -new_shift, 0.0).astype(BF16)
        qa = jnp.concatenate([st["qt"], shift_rows, pad], axis=0)
        st["issued"] = new_shift, _dot(kl_ref[g, pl.ds(c * tk, tk), :], qa)

    def consume(g, c, st):
        new_shift, s = st.pop("issued")
        p = jnp.exp2(s)
        pv = _dot(vtl_ref[g, :, pl.ds(c * tk, tk)], p.astype(BF16))
        st["acc"] = st["acc"] * jnp.exp2(st["shift"] - new_shift) + pv
        st["shift"] = new_shift
        pmax = jnp.max(p, axis=0, keepdims=True)
        st["largest"] = jnp.maximum(st["largest"], pmax)
        st["running"] = jnp.maximum(st["running"], new_shift + jnp.log2(pmax))
        st["shifts"].append(st["running"].astype(BF16).astype(F32))

    states = [start(g) for g in range(N_KV)]
    items = [(g, c) for c in range(n_lat_chunks) for g in range(N_KV)]
    issue(*items[0], states[items[0][0]])
    for k, (g, c) in enumerate(items):
        if k + 1 < len(items):
            gn, cn = items[k + 1]
            issue(gn, cn, states[gn])
        consume(g, c, states[g])
    overflow = jnp.max(jnp.maximum(states[0]["largest"], states[1]["largest"])) > MAX_SHIFTED_WEIGHT

    @pl.when(overflow)
    def _():
        for g in range(N_KV):
            finish(g, exact(g))

    @pl.when(jnp.logical_not(overflow))
    def _():
        for g in range(N_KV):
            finish(g, states[g]["acc"])


def _attention(qt, k_ctx, vt_ctx, k_lat, vt_lat, *, n_seq, q_len, ctx_len, lat_len):
    tq = min(256, q_len)
    tk = min(512, lat_len // 2) if lat_len else 0
    n_lat_chunks = lat_len // tk if lat_len else 0
    assert n_lat_chunks % 2 == 0
    nq = q_len // tq
    kern = functools.partial(_attn_kernel, tq=tq, tk=tk, n_lat_chunks=n_lat_chunks)
    in_specs = [pl.BlockSpec((None, N_HEADS, HEAD_DIM, tq), lambda b, i: (b, 0, 0, i)),
                pl.BlockSpec((None, N_KV, ctx_len, K_COLS), lambda b, i: (b, 0, 0, 0)),
                pl.BlockSpec((None, N_KV, V_ROWS, ctx_len), lambda b, i: (b, 0, 0, 0))]
    args = [qt, k_ctx, vt_ctx]
    scratch = []
    if lat_len:
        in_specs += [pl.BlockSpec((None, N_KV, lat_len, K_COLS), lambda b, i: (b, 0, 0, 0)),
                     pl.BlockSpec((None, N_KV, V_ROWS, lat_len), lambda b, i: (b, 0, 0, 0))]
        args += [k_lat, vt_lat]
    return pl.pallas_call(
        kern,
        grid=(n_seq, nq),
        in_specs=in_specs,
        out_specs=pl.BlockSpec((tq, D_ATTN), lambda b, i: (b * nq + i, 0)),
        out_shape=jax.ShapeDtypeStruct((n_seq * q_len, D_ATTN), BF16),
        scratch_shapes=scratch,
        compiler_params=_params("arbitrary", "arbitrary"),
        name="attn_latent" if lat_len else "attn_context",
    )(*args)


def _lru_sequence(x_ref, g_ref, o_ref, h0, consts, scr, *, n_groups):
    cw_ref, cb_ref, wg_ref, bg_ref, hnsp = consts
    xe_ref, af_ref, bf_ref, ar_ref, br_ref = scr
    rows = SUBLANES * n_groups
    width = x_ref.shape[1]
    chunk = min(256, rows)
    n_chunks = rows // chunk

    xe_ref[pl.ds(2 * SUBLANES, rows), :] = x_ref[...]
    xe_ref[pl.ds(0, SUBLANES), :] = _shift_down(x_ref[pl.ds(rows - 2 * SUBLANES, SUBLANES), :])
    xe_ref[pl.ds(SUBLANES, SUBLANES), :] = _shift_down(x_ref[pl.ds(rows - SUBLANES, SUBLANES), :])
    xe_ref[pl.ds(rows + 2 * SUBLANES, SUBLANES), :] = _shift_up(x_ref[pl.ds(0, SUBLANES), :])

    def gates(c, _):
        r0 = pl.multiple_of(c * chunk, chunk)
        xc = cb_ref[...]
        for k in range(4):
            xc = xc + cw_ref[k:k + 1, :] * xe_ref[pl.ds(r0 + SUBLANES * k, chunk), :]
        t = jnp.tanh(_dot(xc.astype(BF16), wg_ref[...]) + bg_ref[...])
        hx = 0.5 * xc
        for d, (a_ref, b_ref) in enumerate(((af_ref, bf_ref), (ar_ref, br_ref))):
            t_r = t[:, (2 * d) * width:(2 * d + 1) * width]
            t_i = t[:, (2 * d + 1) * width:(2 * d + 2) * width]
            log_a = hnsp[d] * t_r + hnsp[d]
            a = jnp.exp(log_a)
            y = jnp.tanh(-log_a) * (1.0 + a * a)
            b = jnp.where(y > 0.0, y * lax.rsqrt(y), 0.0) * (hx * t_i + hx)
            a_ref[pl.ds(r0, chunk), :] = a
            b_ref[pl.ds(r0, chunk), :] = b
        return 0

    lax.fori_loop(0, n_chunks, gates, 0)

    blk = min(SCAN_BLOCK, n_groups)
    blk_rows = blk * SUBLANES
    n_blocks = n_groups // blk

    def scan_block(a_ref, b_ref, r0, h, p, order):
        a_blk = a_ref[pl.ds(r0, blk_rows), :]
        b_blk = b_ref[pl.ds(r0, blk_rows), :]
        hs, ps = [None] * blk, [None] * blk
        for k in order:
            a = a_blk[k * SUBLANES:(k + 1) * SUBLANES]
            h = a * h + b_blk[k * SUBLANES:(k + 1) * SUBLANES]
            p = a * p
            hs[k], ps[k] = h, p
        b_ref[pl.ds(r0, blk_rows), :] = jnp.concatenate(hs, axis=0)
        a_ref[pl.ds(r0, blk_rows), :] = jnp.concatenate(ps, axis=0)
        return h, p

    def scan(i, carry):
        hf, pf, hr, pr = carry
        rf = pl.multiple_of(i * blk_rows, blk_rows)
        rr = pl.multiple_of((n_blocks - 1 - i) * blk_rows, blk_rows)
        hf, pf = scan_block(af_ref, bf_ref, rf, hf, pf, range(blk))
        hr, pr = scan_block(ar_ref, br_ref, rr, hr, pr, range(blk - 1, -1, -1))
        return hf, pf, hr, pr

    zero = jnp.zeros((SUBLANES, width), F32)
    one = jnp.ones((SUBLANES, width), F32)
    hf, pf, hr, pr = lax.fori_loop(0, n_blocks, scan, (zero, one, zero, one))

    e = h0[0]
    init_f = []
    for s in range(SUBLANES):
        init_f.append(e)
        e = hf[s:s + 1, :] + pf[s:s + 1, :] * e
    fin_f = e
    e = h0[1]
    init_r = [None] * SUBLANES
    for s in range(SUBLANES - 1, -1, -1):
        init_r[s] = e
        e = hr[s:s + 1, :] + pr[s:s + 1, :] * e
    fin_r = e

    if o_ref is not None:
        init_f = jnp.concatenate(init_f, axis=0)[None]
        init_r = jnp.concatenate(init_r, axis=0)[None]

        def combine(c, _):
            r0 = pl.multiple_of(c * chunk, chunk)
            sl = pl.ds(r0, chunk)
            shp = (chunk // SUBLANES, SUBLANES, width)
            h = (bf_ref[sl, :].reshape(shp) + af_ref[sl, :].reshape(shp) * init_f
                 + br_ref[sl, :].reshape(shp) + ar_ref[sl, :].reshape(shp) * init_r)
            o_ref[sl, :] = (h.reshape(chunk, width) * g_ref[sl, :]).astype(o_ref.dtype)
            return 0

        lax.fori_loop(0, n_chunks, combine, 0)
    return fin_f, fin_r


def _lru_kernel(*refs, ctx_groups, lat_groups, ctx_out):
    xc_ref, gc_ref, xl_ref, gl_ref, cw_ref, cb_ref, wg_ref, bg_ref, lam_ref = refs[:9]
    n_out = 2 if ctx_out else 1
    outs = refs[9:9 + n_out]
    scr = refs[9 + n_out:]
    z = -lam_ref[...]
    softplus = jnp.maximum(z, 0.0) + jnp.log1p(jnp.exp(-jnp.abs(z)))
    hnsp = [(-0.5 * LRU_C) * softplus[d:d + 1, :] for d in range(2)]
    consts = (cw_ref, cb_ref, wg_ref, bg_ref, hnsp)
    zero = jnp.zeros((1, xc_ref.shape[1]), F32)
    fin = _lru_sequence(xc_ref, gc_ref, outs[1] if ctx_out else None, (zero, zero), consts, scr,
                        n_groups=ctx_groups)
    _lru_sequence(xl_ref, gl_ref, outs[0], fin, consts, scr, n_groups=lat_groups)


def _lru(x_ctx, g_ctx, x_lat, g_lat, conv_w, conv_b, wg, bg, lam, *, n_seq, ctx_len, lat_len, ctx_out):
    w = LRU_SLAB
    n_slab = D_LRU // w
    kern = functools.partial(_lru_kernel, ctx_groups=ctx_len // SUBLANES,
                             lat_groups=lat_len // SUBLANES, ctx_out=ctx_out)
    ctx_spec = pl.BlockSpec((ctx_len, w), lambda b, c: (b, c))
    lat_spec = pl.BlockSpec((lat_len, w), lambda b, c: (b, c))
    out_specs = [lat_spec]
    out_shape = [jax.ShapeDtypeStruct((n_seq * lat_len, D_LRU), BF16)]
    if ctx_out:
        out_specs.append(ctx_spec)
        out_shape.append(jax.ShapeDtypeStruct((n_seq * ctx_len, D_LRU), BF16))
    max_rows = max(ctx_len, lat_len)
    return pl.pallas_call(
        kern,
        grid=(n_seq, n_slab),
        in_specs=[ctx_spec, ctx_spec, lat_spec, lat_spec,
                  pl.BlockSpec((4, w), lambda b, c: (0, c)),
                  pl.BlockSpec((1, w), lambda b, c: (0, c)),
                  pl.BlockSpec((None, w, 4 * w), lambda b, c: (c, 0, 0)),
                  pl.BlockSpec((None, 1, 4 * w), lambda b, c: (c, 0, 0)),
                  pl.BlockSpec((2, w), lambda b, c: (0, c))],
        out_specs=out_specs,
        out_shape=out_shape,
        scratch_shapes=[pltpu.VMEM((max_rows + 3 * SUBLANES, w), F32)]
                       + [pltpu.VMEM((max_rows, w), F32)] * 4,
        compiler_params=_params("arbitrary", "arbitrary"),
        name="rglru_bidir",
    )(x_ctx, g_ctx, x_lat, g_lat, conv_w, conv_b, wg, bg, lam)


def _outproj_kernel(a_ref, r_ref, x_ref, gate_ref, nw_ref, w_ref, o_ref):
    n_sub = _n_sub(x_ref.shape[0])
    sub = x_ref.shape[0] // n_sub
    w = w_ref[...].astype(BF16)

    def project(r0):
        rows = pl.ds(r0, sub)
        cat = jnp.concatenate([_rms(a_ref[rows, :].astype(F32)), _rms(r_ref[rows, :].astype(F32))],
                              axis=1) * nw_ref[...]
        return _dot(cat.astype(BF16), w)

    y = project(0)
    for s in range(n_sub):
        y_next = project((s + 1) * sub) if s + 1 < n_sub else None
        rows = pl.ds(s * sub, sub)
        o_ref[rows, :] = x_ref[rows, :] + gate_ref[...] * y
        y = y_next


def _outproj(a, r, x, mod, nw, w_out, *, n_seq, seq_len, shared_mod, mod_row, layer):
    tm = min(1024, seq_len)
    tps = seq_len // tm
    rows = n_seq * seq_len
    gate_idx = (lambda i: (mod_row, 0, 2)) if shared_mod else (lambda i: (mod_row + i // tps, 0, 2))
    return pl.pallas_call(
        _outproj_kernel,
        grid=(rows // tm,),
        in_specs=[pl.BlockSpec((tm, D_ATTN), lambda i: (i, 0)),
                  pl.BlockSpec((tm, D_LRU), lambda i: (i, 0)),
                  pl.BlockSpec((tm, D_MODEL), lambda i: (i, 0)),
                  pl.BlockSpec((None, 1, D_MODEL), gate_idx),
                  _resident((1, D_MODEL)),
                  _resident((D_MODEL, D_MODEL), layer)],
        out_specs=pl.BlockSpec((tm, D_MODEL), lambda i: (i, 0)),
        out_shape=jax.ShapeDtypeStruct((rows, D_MODEL), F32),
        compiler_params=_params("arbitrary"),
        name="out_proj",
    )(a, r, x, mod, nw, w_out)


def _ffn_kernel(*refs, tm, tps, final_norm):
    x_ref, xp_ref, xn_ref, mod_ref, wup_ref, cw_ref, cb_ref, wdn_ref = refs[:8]
    if final_norm:
        fnw_ref, o_ref, hext_ref, act_ref, ua_ref, ub_ref, slab_ref = refs[8:]
    else:
        o_ref, hext_ref, act_ref, ua_ref, ub_ref = refs[8:]
    assert (D_FF // FF_CHUNK) % 2 == 1
    jt = pl.program_id(0) % tps
    shift = mod_ref[:, 0:D_MODEL]
    scale = mod_ref[:, D_MODEL:2 * D_MODEL]
    gate = mod_ref[:, 2 * D_MODEL:3 * D_MODEL]

    def norm_mod(x):
        return _rms(x) * (1.0 + scale) + shift

    x = x_ref[...]
    hext_ref[pl.ds(BF16_ROWS, tm), :] = norm_mod(x).astype(BF16)
    hp = norm_mod(xp_ref[...])
    hp = jnp.where(jt == 0, _shift_down(hp), hp)
    hn = norm_mod(xn_ref[...])
    hn = jnp.where(jt == tps - 1, _shift_up(hn), hn)
    zeros = jnp.zeros_like(hp)
    hext_ref[pl.ds(0, BF16_ROWS), :] = jnp.concatenate([zeros, hp], axis=0).astype(BF16)
    hext_ref[pl.ds(BF16_ROWS + tm, BF16_ROWS), :] = jnp.concatenate([hn, zeros], axis=0).astype(BF16)

    def up(c, u_ref):
        for k in range(2):
            off = pl.multiple_of(k * D_FF + c * FF_CHUNK, FF_CHUNK)
            u_ref[k] = _dot(hext_ref[...], wup_ref[:, pl.ds(off, FF_CHUNK)])

    def conv(u_ref, k, off):
        sl = pl.ds(off, FF_CHUNK)
        return (cw_ref[0:1, sl] * u_ref[k, pl.ds(SUBLANES, tm), :]
                + cw_ref[1:2, sl] * u_ref[k, pl.ds(2 * SUBLANES, tm), :]
                + cw_ref[2:3, sl] * u_ref[k, pl.ds(3 * SUBLANES, tm), :]
                + cb_ref[:, sl])

    def activate(c, u_ref):
        og = pl.multiple_of(c * FF_CHUNK, FF_CHUNK)
        ov = pl.multiple_of(D_FF + c * FF_CHUNK, FF_CHUNK)
        yg = conv(u_ref, 0, og)
        yv = conv(u_ref, 1, ov)
        hg = 0.5 * yg
        act_ref[:, pl.ds(og, FF_CHUNK)] = ((hg + hg * jnp.tanh(hg)) * yv).astype(BF16)

    n_chunks = D_FF // FF_CHUNK
    up(0, ua_ref)

    def pair(j, _):
        up(2 * j + 1, ub_ref)
        activate(2 * j, ua_ref)
        up(2 * j + 2, ua_ref)
        activate(2 * j + 1, ub_ref)
        return 0

    for j in range((n_chunks - 1) // 2):
        pair(j, 0)
    activate(n_chunks - 1, ua_ref)
    out = x + gate * _dot(act_ref[...], wdn_ref[...])
    if not final_norm:
        o_ref[...] = out
        return
    out = _rms(out) * fnw_ref[...]
    for k in range(D_MODEL // LANES):
        slab_ref[k] = out[:, LANES * k:LANES * (k + 1)]
    for s in range(SUBLANES):
        for k in range(D_MODEL // LANES):
            o_ref[s, :, LANES * k:LANES * (k + 1)] = slab_ref[k, pl.ds(s, tm // SUBLANES, stride=SUBLANES), :]


def _ffn(x, mod, w_up, conv_w, conv_b, w_down, final_w, *, n_seq, seq_len, shared_mod, mod_row, layer):
    tm = min(1024, seq_len)
    tps = seq_len // tm
    rows = n_seq * seq_len
    gps = seq_len // SUBLANES
    gpt = tm // SUBLANES
    mod_idx = (lambda i: (mod_row, 0, 1)) if shared_mod else (lambda i: (mod_row + i // tps, 0, 1))

    def prev_idx(i):
        jt = i % tps
        return ((i // tps) * gps + jnp.where(jt == 0, gps - 1, jt * gpt - 1), 0)

    def next_idx(i):
        jt = i % tps
        return ((i // tps) * gps + jnp.where(jt == tps - 1, 0, (jt + 1) * gpt), 0)

    final_norm = final_w is not None
    in_specs = [pl.BlockSpec((tm, D_MODEL), lambda i: (i, 0)),
                pl.BlockSpec((SUBLANES, D_MODEL), prev_idx),
                pl.BlockSpec((SUBLANES, D_MODEL), next_idx),
                pl.BlockSpec((None, 1, 3 * D_MODEL), mod_idx),
                _resident((D_MODEL, 2 * D_FF), layer),
                _resident((3, 2 * D_FF), layer),
                _resident((1, 2 * D_FF), layer),
                _resident((D_FF, D_MODEL), layer)]
    args = [x, x, x, mod, w_up, conv_w, conv_b, w_down]
    if final_norm:
        in_specs.append(_resident((1, D_MODEL)))
        args.append(final_w)
    kern = functools.partial(_ffn_kernel, tm=tm, tps=tps, final_norm=final_norm)
    scratch = ([pltpu.VMEM((tm + 2 * BF16_ROWS, D_MODEL), BF16), pltpu.VMEM((tm, D_FF), BF16)]
               + [pltpu.VMEM((2, tm + 2 * BF16_ROWS, FF_CHUNK), F32)] * 2)
    if final_norm:
        out_spec = pl.BlockSpec((None, SUBLANES, tm // SUBLANES, D_MODEL),
                                lambda i: (i // tps, 0, i % tps, 0))
        out_shape = jax.ShapeDtypeStruct((n_seq, SUBLANES, gps, D_MODEL), F32)
        scratch.append(pltpu.VMEM((D_MODEL // LANES, tm, LANES), F32))
    else:
        out_spec = pl.BlockSpec((tm, D_MODEL), lambda i: (i, 0))
        out_shape = jax.ShapeDtypeStruct((rows, D_MODEL), F32)
    return pl.pallas_call(
        kern,
        grid=(rows // tm,),
        in_specs=in_specs,
        out_specs=out_spec,
        out_shape=out_shape,
        scratch_shapes=scratch,
        compiler_params=_params("arbitrary"),
        name="conv_ffn_final" if final_norm else "conv_ffn",
    )(*args)


def _to_segments(x):
    b, t, c = x.shape
    return x.reshape(b, SUBLANES, t // SUBLANES, c).transpose(0, 2, 1, 3).reshape(b * t, c)


def _from_segments(y, b, t):
    return y.reshape(b, t // SUBLANES, SUBLANES, -1).transpose(0, 2, 1, 3).reshape(b, t, -1)


def _rope_tables(seq_len):
    r = np.arange(seq_len)
    t = (r % SUBLANES) * (seq_len // SUBLANES) + r // SUBLANES
    row = (t // GRID_W).astype(np.float64)
    col = (t % GRID_W).astype(np.float64)
    pairs = HEAD_DIM // 4
    inv = ROPE_THETA ** (-np.arange(pairs, dtype=np.float64) / pairs)
    ang = np.concatenate([row[:, None] * inv, col[:, None] * inv], axis=-1)
    cos, sin = np.cos(ang), np.sin(ang)
    return (jnp.asarray(np.concatenate([cos, cos, cos, cos], axis=-1), F32),
            jnp.asarray(np.concatenate([-sin, sin, -sin, sin], axis=-1), F32))


def _gate_weights(wa, wx, ba, bx):
    n_slab = D_LRU // LRU_SLAB
    per = LRU_SLAB // wa.shape[-1]

    def slab_diag(w):
        w = w.reshape(n_slab, per, w.shape[-2], w.shape[-1])
        eye = jnp.eye(per, dtype=w.dtype)
        return jnp.einsum('spcd,pq->spcqd', w, eye).reshape(n_slab, LRU_SLAB, LRU_SLAB)

    wg = jnp.concatenate([slab_diag(wa[0]), slab_diag(wx[0]), slab_diag(wa[1]), slab_diag(wx[1])],
                         axis=-1)
    bg = jnp.stack([ba[0], bx[0], ba[1], bx[1]], axis=0).reshape(4, n_slab, LRU_SLAB)
    bg = bg.transpose(1, 0, 2).reshape(n_slab, 1, 4 * LRU_SLAB)
    return (0.5 * wg).astype(BF16), 0.5 * bg


def kernel(x, c, ctx, c_ctx, w_ada, b_ada, w_in, q_norm_w, k_norm_w, lru_conv_w, lru_conv_b, lru_wa,
           lru_ba, lru_wx, lru_bx, lru_lambda, attn_out_norm_w, lru_out_norm_w, w_out, ffn_w_up,
           ffn_conv_w, ffn_conv_b, ffn_w_down, final_norm_w):
    batch, seq, _ = x.shape
    ctx_len = ctx.shape[1]
    depth = w_in.shape[0]

    mod_rows = -(-(batch + 1) // SUBLANES) * SUBLANES
    cvec = jnp.zeros((mod_rows, D_MODEL), F32).at[:batch].set(c).at[batch].set(c_ctx)
    mods = _modulation(cvec, w_ada, b_ada)

    cos, sin = _rope_tables(seq)
    ones = jnp.asarray(np.ones((ctx_len, LANES)), F32)
    bd = jnp.asarray(np.kron(np.eye(QK_W // HEAD_DIM), np.ones((HEAD_DIM, HEAD_DIM))), BF16)

    x_lat = _to_segments(x)
    x_ctx = _to_segments(ctx)
    mods = mods.reshape(depth * mod_rows, 1, 6 * D_MODEL)
    w_up_b = ffn_w_up.astype(BF16)
    w_dn_b = ffn_w_down.astype(BF16)
    conv_b = ffn_conv_b.reshape(depth, 1, 2 * D_FF)

    for l in range(depth):
        ctx_out = l < depth - 1
        last = l == depth - 1
        lat = dict(n_seq=batch, seq_len=seq, shared_mod=False, mod_row=l * mod_rows, layer=l)
        cx = dict(n_seq=batch, seq_len=ctx_len, shared_mod=True, mod_row=l * mod_rows + batch, layer=l)
        nw_qk = jnp.concatenate([jnp.tile(q_norm_w[l], D_ATTN // HEAD_DIM),
                                 jnp.tile(k_norm_w[l], KV_W // HEAD_DIM)]).reshape(1, QK_W)
        qt_l, k_l, vt_l, xr_l, g_l = _inproj(x_lat, mods, w_in, bd, nw_qk, cos, sin,
                                             use_rope=True, **lat)
        qt_c, k_c, vt_c, xr_c, g_c = _inproj(x_ctx, mods, w_in, bd, nw_qk, ones, ones,
                                             use_rope=False, **cx)

        a_l = _attention(qt_l, k_c, vt_c, k_l, vt_l, n_seq=batch, q_len=seq, ctx_len=ctx_len,
                         lat_len=seq)
        wg, bg = _gate_weights(lru_wa[l], lru_wx[l], lru_ba[l], lru_bx[l])
        r = _lru(xr_c, g_c, xr_l, g_l, lru_conv_w[l], lru_conv_b[l].reshape(1, D_LRU), wg, bg,
                 lru_lambda[l], n_seq=batch, ctx_len=ctx_len, lat_len=seq, ctx_out=ctx_out)

        nw_out = jnp.concatenate([attn_out_norm_w[l], lru_out_norm_w[l]]).reshape(1, D_MODEL)
        x_lat = _outproj(a_l, r[0], x_lat, mods, nw_out, w_out, **lat)
        x_lat = _ffn(x_lat, mods, w_up_b, ffn_conv_w, conv_b, w_dn_b,
                     final_norm_w.reshape(1, D_MODEL) if last else None, **lat)
        if ctx_out:
            a_c = _attention(qt_c, k_c, vt_c, None, None, n_seq=batch, q_len=ctx_len,
                             ctx_len=ctx_len, lat_len=0)
            x_ctx = _outproj(a_c, r[1], x_ctx, mods, nw_out, w_out, **cx)
            x_ctx = _ffn(x_ctx, mods, w_up_b, ffn_conv_w, conv_b, w_dn_b, None, **cx)
    return x_lat.reshape(batch, seq, D_MODEL)
```

```python
import functools

import jax
import jax.numpy as jnp
import numpy as np
from jax import lax
from jax.experimental import pallas as pl
from jax.experimental.pallas import tpu as pltpu

F32 = jnp.float32
BF16 = jnp.bfloat16

D_MODEL = 1024
D_ATTN = 512
D_LRU = 512
KV_W = 128
HEAD_DIM = 64
N_HEADS = 8
N_KV = 2
KV_GROUP = 4
LOG2E = 1.4426950408889634
GELU_C0 = 0.7978845608028654
GELU_C1 = GELU_C0 * 0.044715
QK_W = D_ATTN + KV_W
D_IN = D_ATTN + 2 * KV_W + 2 * D_LRU
D_FF = 2816
LRU_C = 8.0
GRID_W = 64
ROPE_THETA = 10000.0
EPS = 1e-6
SUBLANES = 8
LANES = 128
BF16_ROWS = 16
V_ROWS = HEAD_DIM + BF16_ROWS
K_COLS = LANES
MAX_SHIFTED_WEIGHT = 2.0 ** 60
FF_CHUNK = 256
SUB_ROWS = 256
LRU_SLAB = 256
SCAN_BLOCK = 8
VMEM_LIMIT = 58 * 1024 * 1024


def _dot(a, b):
    return jnp.dot(a, b, preferred_element_type=F32)


def _sigmoid(z):
    return 0.5 * jnp.tanh(0.5 * z) + 0.5


def _rms(x):
    return x * lax.rsqrt(jnp.mean(x * x, axis=-1, keepdims=True) + EPS)


def _shift_down(g):
    rows = lax.broadcasted_iota(jnp.int32, g.shape, 0)
    return jnp.where(rows == 0, 0.0, pltpu.roll(g, 1, 0))


def _shift_up(g):
    rows = lax.broadcasted_iota(jnp.int32, g.shape, 0)
    return jnp.where(rows == SUBLANES - 1, 0.0, pltpu.roll(g, SUBLANES - 1, 0))


def _n_sub(tile_rows):
    return max(2, tile_rows // SUB_ROWS)


def _params(*sem):
    return pltpu.CompilerParams(dimension_semantics=sem, vmem_limit_bytes=VMEM_LIMIT)


def _resident(shape, layer=None):
    nd = len(shape)
    if layer is None:
        return pl.BlockSpec(shape, lambda *_: (0,) * nd, pipeline_mode=pl.Buffered(1))
    return pl.BlockSpec((None,) + tuple(shape), lambda *_: (layer,) + (0,) * nd,
                        pipeline_mode=pl.Buffered(1))


def _mod_kernel(c_ref, w_ref, b_ref, o_ref):
    c = c_ref[...]
    s = (c * _sigmoid(c)).astype(BF16)
    o_ref[...] = _dot(s, w_ref[...].astype(BF16)) + b_ref[...]


def _modulation(cvec, w_ada, b_ada, tn=1536):
    depth, d, n = w_ada.shape
    rows = cvec.shape[0]
    return pl.pallas_call(
        _mod_kernel,
        grid=(depth, n // tn),
        in_specs=[pl.BlockSpec((rows, d), lambda l, j: (0, 0)),
                  pl.BlockSpec((None, d, tn), lambda l, j: (l, 0, j)),
                  pl.BlockSpec((None, 1, tn), lambda l, j: (l, 0, j))],
        out_specs=pl.BlockSpec((None, rows, tn), lambda l, j: (l, 0, j)),
        out_shape=jax.ShapeDtypeStruct((depth, rows, n), F32),
        compiler_params=_params("arbitrary", "arbitrary"),
        name="adaln_mod",
    )(cvec, w_ada, b_ada.reshape(depth, 1, n))


def _inproj_kernel(x_ref, mod_ref, w_ref, bd_ref, nw_ref, cos_ref, sin_ref,
                   qt_ref, k_ref, vt_ref, xl_ref, g_ref, *, use_rope):
    shift = mod_ref[:, 0:D_MODEL]
    scale = mod_ref[:, D_MODEL:2 * D_MODEL]
    n_sub = _n_sub(x_ref.shape[0])
    sub = x_ref.shape[0] // n_sub
    w = w_ref[...].astype(BF16)

    def project(r0):
        h = _rms(x_ref[pl.ds(r0, sub), :]) * (1.0 + scale) + shift
        y = _dot(h.astype(BF16), w)
        qk = y[:, 0:QK_W]
        return y, _dot((qk * qk).astype(BF16), bd_ref[...])

    def finish(r0, y_ss):
        y, ss = y_ss
        rows = pl.ds(r0, sub)
        qk = y[:, 0:QK_W]
        qk = qk * lax.rsqrt(ss * (1.0 / HEAD_DIM) + EPS) * nw_ref[...]
        if use_rope:
            lane = lax.broadcasted_iota(jnp.int32, (sub, LANES), 1)
            first_half = (lane % HEAD_DIM) < (HEAD_DIM // 2)
            cos = cos_ref[rows, :]
            sin = sin_ref[rows, :]
            cols = []
            for k in range(QK_W // LANES):
                xc = qk[:, LANES * k:LANES * (k + 1)]
                other = jnp.where(first_half, pltpu.roll(xc, LANES - HEAD_DIM // 2, 1),
                                  pltpu.roll(xc, HEAD_DIM // 2, 1))
                cols.append(xc * cos + other * sin)
            qk = jnp.concatenate(cols, axis=1)
        qt = (qk[:, 0:D_ATTN] * (LOG2E * HEAD_DIM ** -0.5)).T
        for h in range(N_HEADS):
            qt_ref[h, :, rows] = qt[HEAD_DIM * h:HEAD_DIM * (h + 1)].astype(BF16)
        kk = qk[:, D_ATTN:QK_W]
        lane = lax.broadcasted_iota(jnp.int32, kk.shape, 1)
        one_hot = jnp.where(lane == HEAD_DIM, 1.0, 0.0)
        for g in range(N_KV):
            kg = kk if g == 0 else pltpu.roll(kk, HEAD_DIM, 1)
            k_ref[g, rows, :] = jnp.where(lane < HEAD_DIM, kg, one_hot).astype(BF16)
        vt = y[:, QK_W:QK_W + KV_W].T.astype(BF16)
        ones = jnp.ones((V_ROWS - HEAD_DIM, sub), BF16)
        for g in range(N_KV):
            vt_ref[g, :, rows] = jnp.concatenate([vt[HEAD_DIM * g:HEAD_DIM * (g + 1)], ones], axis=0)
        xl_ref[rows, :] = y[:, QK_W + KV_W:QK_W + KV_W + D_LRU]
        gt = y[:, QK_W + KV_W + D_LRU:D_IN]
        hg = 0.5 * gt
        g_ref[rows, :] = hg + hg * jnp.tanh(gt * (GELU_C0 + GELU_C1 * (gt * gt)))

    y = project(0)
    for s in range(n_sub):
        y_next = project((s + 1) * sub) if s + 1 < n_sub else None
        finish(s * sub, y)
        y = y_next


def _inproj(x, mod, w_in, bd, nw, cos, sin, *, n_seq, seq_len, shared_mod, mod_row, layer,
            use_rope):
    tm = min(1024, seq_len)
    tps = seq_len // tm
    rows = n_seq * seq_len
    mod_idx = (lambda i: (mod_row, 0, 0)) if shared_mod else (lambda i: (mod_row + i // tps, 0, 0))
    kern = functools.partial(_inproj_kernel, use_rope=use_rope)
    return pl.pallas_call(
        kern,
        grid=(rows // tm,),
        in_specs=[pl.BlockSpec((tm, D_MODEL), lambda i: (i, 0)),
                  pl.BlockSpec((None, 1, 2 * D_MODEL), mod_idx),
                  _resident((D_MODEL, D_IN), layer),
                  _resident((QK_W, QK_W)),
                  _resident((1, QK_W)),
                  pl.BlockSpec((tm, LANES), lambda i: (i % tps, 0)),
                  pl.BlockSpec((tm, LANES), lambda i: (i % tps, 0))],
        out_specs=[pl.BlockSpec((None, N_HEADS, HEAD_DIM, tm), lambda i: (i // tps, 0, 0, i % tps)),
                   pl.BlockSpec((None, N_KV, tm, K_COLS), lambda i: (i // tps, 0, i % tps, 0)),
                   pl.BlockSpec((None, N_KV, V_ROWS, tm), lambda i: (i // tps, 0, 0, i % tps)),
                   pl.BlockSpec((tm, D_LRU), lambda i: (i, 0)),
                   pl.BlockSpec((tm, D_LRU), lambda i: (i, 0))],
        out_shape=[jax.ShapeDtypeStruct((n_seq, N_HEADS, HEAD_DIM, seq_len), BF16),
                   jax.ShapeDtypeStruct((n_seq, N_KV, seq_len, K_COLS), BF16),
                   jax.ShapeDtypeStruct((n_seq, N_KV, V_ROWS, seq_len), BF16),
                   jax.ShapeDtypeStruct((rows, D_LRU), F32),
                   jax.ShapeDtypeStruct((rows, D_LRU), F32)],
        compiler_params=_params("arbitrary"),
        name="in_proj_rope" if use_rope else "in_proj",
    )(x, mod, w_in, bd, nw, cos, sin)


def _attn_kernel(*refs, tq, tk, n_lat_chunks):
    if n_lat_chunks:
        qt_ref, kc_ref, vtc_ref, kl_ref, vtl_ref, o_ref = refs
    else:
        qt_ref, kc_ref, vtc_ref, o_ref = refs
    n = KV_GROUP * tq

    def queries(g):
        return jnp.concatenate([qt_ref[KV_GROUP * g + h] for h in range(KV_GROUP)], axis=1)

    def ctx_scores(g, qt):
        return _dot(kc_ref[g, :, 0:HEAD_DIM], qt)

    def finish(g, acc):
        ot = acc[0:HEAD_DIM] * (1.0 / acc[HEAD_DIM:HEAD_DIM + 1])
        for k in range(KV_GROUP):
            h = KV_GROUP * g + k
            o_ref[:, HEAD_DIM * h:HEAD_DIM * (h + 1)] = ot[:, k * tq:(k + 1) * tq].T.astype(o_ref.dtype)

    def exact(g):
        qt = queries(g)
        s = ctx_scores(g, qt)
        m = jnp.max(s, axis=0, keepdims=True)
        acc = _dot(vtc_ref[g], jnp.exp2(s - m).astype(BF16))

        def body(c, carry):
            m, acc = carry
            off = pl.multiple_of(c * tk, tk)
            s = _dot(kl_ref[g, pl.ds(off, tk), 0:HEAD_DIM], qt)
            m_new = jnp.maximum(m, jnp.max(s, axis=0, keepdims=True))
            p = jnp.exp2(s - m_new).astype(BF16)
            return m_new, jnp.exp2(m - m_new) * acc + _dot(vtl_ref[g, :, pl.ds(off, tk)], p)

        if n_lat_chunks:
            m, acc = lax.fori_loop(0, n_lat_chunks, body, (m, acc))
        return acc

    if not n_lat_chunks:
        for g in range(N_KV):
            finish(g, exact(g))
        return

    row = lax.broadcasted_iota(jnp.int32, (BF16_ROWS, n), 0)
    pad = jnp.zeros((K_COLS - HEAD_DIM - BF16_ROWS, n), BF16)

    def start(g):
        qt = queries(g)
        s = ctx_scores(g, qt)
        running = jnp.max(s, axis=0, keepdims=True)
        shift = running.astype(BF16).astype(F32)
        acc = _dot(vtc_ref[g], jnp.exp2(s - shift).astype(BF16))
        return dict(qt=qt, running=running, shift=shift, shifts=[shift, shift], acc=acc,
                    largest=jnp.ones((1, n), F32))

    def issue(g, c, st):
        new_shift = st["shifts"].pop(0)
        shift_rows = jnp.where(row == 0, -new_shift, 0.0).astype(BF16)
        qa = jnp.concatenate([st["qt"], shift_rows, pad], axis=0)
        st["issued"] = new_shift, _dot(kl_ref[g, pl.ds(c * tk, tk), :], qa)

    def consume(g, c, st):
        new_shift, s = st.pop("issued")
        p = jnp.exp2(s)
        pv = _dot(vtl_ref[g, :, pl.ds(c * tk, tk)], p.astype(BF16))
        st["acc"] = st["acc"] * jnp.exp2(st["shift"] - new_shift) + pv
        st["shift"] = new_shift
        pmax = jnp.max(p, axis=0, keepdims=True)
        st["largest"] = jnp.maximum(st["largest"], pmax)
        st["running"] = jnp.maximum(st["running"], new_shift + jnp.log2(pmax))
        st["shifts"].append(st["running"].astype(BF16).astype(F32))

    states = [start(g) for g in range(N_KV)]
    items = [(g, c) for c in range(n_lat_chunks) for g in range(N_KV)]
    issue(*items[0], states[items[0][0]])
    for k, (g, c) in enumerate(items):
        if k + 1 < len(items):
            gn, cn = items[k + 1]
            issue(gn, cn, states[gn])
        consume(g, c, states[g])
    for g in range(N_KV):
        finish(g, states[g]["acc"])
    overflow = jnp.max(jnp.maximum(states[0]["largest"], states[1]["largest"])) > MAX_SHIFTED_WEIGHT

    @pl.when(overflow)
    def _():
        for g in range(N_KV):
            finish(g, exact(g))


def _attention(qt, k_ctx, vt_ctx, k_lat, vt_lat, *, n_seq, q_len, ctx_len, lat_len):
    tq = min(256, q_len)
    tk = min(512, lat_len // 2) if lat_len else 0
    n_lat_chunks = lat_len // tk if lat_len else 0
    assert n_lat_chunks % 2 == 0
    nq = q_len // tq
    kern = functools.partial(_attn_kernel, tq=tq, tk=tk, n_lat_chunks=n_lat_chunks)
    in_specs = [pl.BlockSpec((None, N_HEADS, HEAD_DIM, tq), lambda b, i: (b, 0, 0, i)),
                pl.BlockSpec((None, N_KV, ctx_len, K_COLS), lambda b, i: (b, 0, 0, 0)),
                pl.BlockSpec((None, N_KV, V_ROWS, ctx_len), lambda b, i: (b, 0, 0, 0))]
    args = [qt, k_ctx, vt_ctx]
    scratch = []
    if lat_len:
        in_specs += [pl.BlockSpec((None, N_KV, lat_len, K_COLS), lambda b, i: (b, 0, 0, 0)),
                     pl.BlockSpec((None, N_KV, V_ROWS, lat_len), lambda b, i: (b, 0, 0, 0))]
        args += [k_lat, vt_lat]
    return pl.pallas_call(
        kern,
        grid=(n_seq, nq),
        in_specs=in_specs,
        out_specs=pl.BlockSpec((tq, D_ATTN), lambda b, i: (b * nq + i, 0)),
        out_shape=jax.ShapeDtypeStruct((n_seq * q_len, D_ATTN), BF16),
        scratch_shapes=scratch,
        compiler_params=_params("arbitrary", "arbitrary"),
        name="attn_latent" if lat_len else "attn_context",
    )(*args)


def _lru_sequence(x_ref, g_ref, o_ref, h0, consts, scr, *, n_groups):
    cw_ref, cb_ref, wg_ref, bg_ref, hnsp = consts
    xe_ref, af_ref, bf_ref, ar_ref, br_ref = scr
    rows = SUBLANES * n_groups
    width = x_ref.shape[1]
    chunk = min(256, rows)
    n_chunks = rows // chunk

    xe_ref[pl.ds(2 * SUBLANES, rows), :] = x_ref[...]
    xe_ref[pl.ds(0, SUBLANES), :] = _shift_down(x_ref[pl.ds(rows - 2 * SUBLANES, SUBLANES), :])
    xe_ref[pl.ds(SUBLANES, SUBLANES), :] = _shift_down(x_ref[pl.ds(rows - SUBLANES, SUBLANES), :])
    xe_ref[pl.ds(rows + 2 * SUBLANES, SUBLANES), :] = _shift_up(x_ref[pl.ds(0, SUBLANES), :])

    def gates(c, _):
        r0 = pl.multiple_of(c * chunk, chunk)
        xc = cb_ref[...]
        for k in range(4):
            xc = xc + cw_ref[k:k + 1, :] * xe_ref[pl.ds(r0 + SUBLANES * k, chunk), :]
        t = jnp.tanh(_dot(xc.astype(BF16), wg_ref[...]) + bg_ref[...])
        hx = 0.5 * xc
        for d, (a_ref, b_ref) in enumerate(((af_ref, bf_ref), (ar_ref, br_ref))):
            t_r = t[:, (2 * d) * width:(2 * d + 1) * width]
            t_i = t[:, (2 * d + 1) * width:(2 * d + 2) * width]
            log_a = hnsp[d] * t_r + hnsp[d]
            a = jnp.exp(log_a)
            y = jnp.tanh(-log_a) * (1.0 + a * a)
            b = jnp.where(y > 0.0, y * lax.rsqrt(y), 0.0) * (hx * t_i + hx)
            a_ref[pl.ds(r0, chunk), :] = a
            b_ref[pl.ds(r0, chunk), :] = b
        return 0

    lax.fori_loop(0, n_chunks, gates, 0)

    blk = min(SCAN_BLOCK, n_groups)
    blk_rows = blk * SUBLANES
    n_blocks = n_groups // blk

    def scan_block(a_ref, b_ref, r0, h, p, order):
        a_blk = a_ref[pl.ds(r0, blk_rows), :]
        b_blk = b_ref[pl.ds(r0, blk_rows), :]
        hs, ps = [None] * blk, [None] * blk
        for k in order:
            a = a_blk[k * SUBLANES:(k + 1) * SUBLANES]
            h = a * h + b_blk[k * SUBLANES:(k + 1) * SUBLANES]
            p = a * p
            hs[k], ps[k] = h, p
        b_ref[pl.ds(r0, blk_rows), :] = jnp.concatenate(hs, axis=0)
        a_ref[pl.ds(r0, blk_rows), :] = jnp.concatenate(ps, axis=0)
        return h, p

    def scan(i, carry):
        hf, pf, hr, pr = carry
        rf = pl.multiple_of(i * blk_rows, blk_rows)
        rr = pl.multiple_of((n_blocks - 1 - i) * blk_rows, blk_rows)
        hf, pf = scan_block(af_ref, bf_ref, rf, hf, pf, range(blk))
        hr, pr = scan_block(ar_ref, br_ref, rr, hr, pr, range(blk - 1, -1, -1))
        return hf, pf, hr, pr

    zero = jnp.zeros((SUBLANES, width), F32)
    one = jnp.ones((SUBLANES, width), F32)
    hf, pf, hr, pr = lax.fori_loop(0, n_blocks, scan, (zero, one, zero, one))

    e = h0[0]
    init_f = []
    for s in range(SUBLANES):
        init_f.append(e)
        e = hf[s:s + 1, :] + pf[s:s + 1, :] * e
    fin_f = e
    e = h0[1]
    init_r = [None] * SUBLANES
    for s in range(SUBLANES - 1, -1, -1):
        init_r[s] = e
        e = hr[s:s + 1, :] + pr[s:s + 1, :] * e
    fin_r = e

    if o_ref is not None:
        init_f = jnp.concatenate(init_f, axis=0)[None]
        init_r = jnp.concatenate(init_r, axis=0)[None]

        def combine(c, _):
            r0 = pl.multiple_of(c * chunk, chunk)
            sl = pl.ds(r0, chunk)
            shp = (chunk // SUBLANES, SUBLANES, width)
            h = (bf_ref[sl, :].reshape(shp) + af_ref[sl, :].reshape(shp) * init_f
                 + br_ref[sl, :].reshape(shp) + ar_ref[sl, :].reshape(shp) * init_r)
            o_ref[sl, :] = (h.reshape(chunk, width) * g_ref[sl, :]).astype(o_ref.dtype)
            return 0

        lax.fori_loop(0, n_chunks, combine, 0)
    return fin_f, fin_r


def _lru_kernel(*refs, ctx_groups, lat_groups, ctx_out):
    xc_ref, gc_ref, xl_ref, gl_ref, cw_ref, cb_ref, wa_ref, wx_ref, ba_ref, bx_ref, lam_ref = refs[:11]
    n_out = 2 if ctx_out else 1
    outs = refs[11:11 + n_out]
    scr = refs[11 + n_out:-2]
    wg_ref, bg_ref = refs[-2:]
    width = xc_ref.shape[1]
    blk = wa_ref.shape[-1]
    wg_ref[...] = jnp.zeros(wg_ref.shape, BF16)
    for t, (w_ref, b_ref, d) in enumerate(((wa_ref, ba_ref, 0), (wx_ref, bx_ref, 0),
                                           (wa_ref, ba_ref, 1), (wx_ref, bx_ref, 1))):
        for p in range(width // blk):
            wg_ref[blk * p:blk * (p + 1), t * width + blk * p:t * width + blk * (p + 1)] = (
                (0.5 * w_ref[d, p]).astype(BF16))
        bg_ref[:, t * width:(t + 1) * width] = 0.5 * b_ref[d:d + 1, :]
    z = -lam_ref[...]
    softplus = jnp.maximum(z, 0.0) + jnp.log1p(jnp.exp(-jnp.abs(z)))
    hnsp = [(-0.5 * LRU_C) * softplus[d:d + 1, :] for d in range(2)]
    consts = (cw_ref, cb_ref, wg_ref, bg_ref, hnsp)
    zero = jnp.zeros((1, xc_ref.shape[1]), F32)
    fin = _lru_sequence(xc_ref, gc_ref, outs[1] if ctx_out else None, (zero, zero), consts, scr,
                        n_groups=ctx_groups)
    _lru_sequence(xl_ref, gl_ref, outs[0], fin, consts, scr, n_groups=lat_groups)


def _lru(x_ctx, g_ctx, x_lat, g_lat, conv_w, conv_b, wa, wx, ba, bx, lam, *, n_seq, ctx_len, lat_len,
         ctx_out, layer):
    w = LRU_SLAB
    n_slab = D_LRU // w
    blk = wa.shape[-1]
    per = w // blk

    def vec(rows):
        return pl.BlockSpec((None, rows, w), lambda b, c: (layer, 0, c))

    gate_w = pl.BlockSpec((None, 2, per, blk, blk), lambda b, c: (layer, 0, c, 0, 0))
    kern = functools.partial(_lru_kernel, ctx_groups=ctx_len // SUBLANES,
                             lat_groups=lat_len // SUBLANES, ctx_out=ctx_out)
    ctx_spec = pl.BlockSpec((ctx_len, w), lambda b, c: (b, c))
    lat_spec = pl.BlockSpec((lat_len, w), lambda b, c: (b, c))
    out_specs = [lat_spec]
    out_shape = [jax.ShapeDtypeStruct((n_seq * lat_len, D_LRU), BF16)]
    if ctx_out:
        out_specs.append(ctx_spec)
        out_shape.append(jax.ShapeDtypeStruct((n_seq * ctx_len, D_LRU), BF16))
    max_rows = max(ctx_len, lat_len)
    return pl.pallas_call(
        kern,
        grid=(n_seq, n_slab),
        in_specs=[ctx_spec, ctx_spec, lat_spec, lat_spec,
                  vec(conv_w.shape[1]), vec(1), gate_w, gate_w, vec(2), vec(2), vec(2)],
        out_specs=out_specs,
        out_shape=out_shape,
        scratch_shapes=[pltpu.VMEM((max_rows + 3 * SUBLANES, w), F32)]
                       + [pltpu.VMEM((max_rows, w), F32)] * 4
                       + [pltpu.VMEM((w, 4 * w), BF16), pltpu.VMEM((1, 4 * w), F32)],
        compiler_params=_params("arbitrary", "arbitrary"),
        name="rglru_bidir",
    )(x_ctx, g_ctx, x_lat, g_lat, conv_w, conv_b, wa, wx, ba, bx, lam)


def _outproj_kernel(a_ref, r_ref, x_ref, gate_ref, nw_ref, w_ref, o_ref):
    n_sub = _n_sub(x_ref.shape[0])
    sub = x_ref.shape[0] // n_sub
    w = w_ref[...].astype(BF16)

    def project(r0):
        rows = pl.ds(r0, sub)
        cat = jnp.concatenate([_rms(a_ref[rows, :].astype(F32)), _rms(r_ref[rows, :].astype(F32))],
                              axis=1) * nw_ref[...]
        return _dot(cat.astype(BF16), w)

    y = project(0)
    for s in range(n_sub):
        y_next = project((s + 1) * sub) if s + 1 < n_sub else None
        rows = pl.ds(s * sub, sub)
        o_ref[rows, :] = x_ref[rows, :] + gate_ref[...] * y
        y = y_next


def _outproj(a, r, x, mod, nw, w_out, *, n_seq, seq_len, shared_mod, mod_row, layer):
    tm = min(1024, seq_len)
    tps = seq_len // tm
    rows = n_seq * seq_len
    gate_idx = (lambda i: (mod_row, 0, 2)) if shared_mod else (lambda i: (mod_row + i // tps, 0, 2))
    return pl.pallas_call(
        _outproj_kernel,
        grid=(rows // tm,),
        in_specs=[pl.BlockSpec((tm, D_ATTN), lambda i: (i, 0)),
                  pl.BlockSpec((tm, D_LRU), lambda i: (i, 0)),
                  pl.BlockSpec((tm, D_MODEL), lambda i: (i, 0)),
                  pl.BlockSpec((None, 1, D_MODEL), gate_idx),
                  _resident((1, D_MODEL)),
                  _resident((D_MODEL, D_MODEL), layer)],
        out_specs=pl.BlockSpec((tm, D_MODEL), lambda i: (i, 0)),
        out_shape=jax.ShapeDtypeStruct((rows, D_MODEL), F32),
        compiler_params=_params("arbitrary"),
        name="out_proj",
    )(a, r, x, mod, nw, w_out)


def _ffn_kernel(*refs, tm, tps, final_norm):
    x_ref, xp_ref, xn_ref, mod_ref, wup_ref, cw_ref, cb_ref, wdn_ref = refs[:8]
    if final_norm:
        fnw_ref, o_ref, hext_ref, act_ref, ua_ref, ub_ref, slab_ref = refs[8:]
    else:
        o_ref, hext_ref, act_ref, ua_ref, ub_ref = refs[8:]
    assert (D_FF // FF_CHUNK) % 2 == 1
    jt = pl.program_id(0) % tps
    shift = mod_ref[:, 0:D_MODEL]
    scale = mod_ref[:, D_MODEL:2 * D_MODEL]
    gate = mod_ref[:, 2 * D_MODEL:3 * D_MODEL]

    def norm_mod(x):
        return _rms(x) * (1.0 + scale) + shift

    x = x_ref[...]
    hext_ref[pl.ds(BF16_ROWS, tm), :] = norm_mod(x).astype(BF16)
    hp = norm_mod(xp_ref[...])
    hp = jnp.where(jt == 0, _shift_down(hp), hp)
    hn = norm_mod(xn_ref[...])
    hn = jnp.where(jt == tps - 1, _shift_up(hn), hn)
    zeros = jnp.zeros_like(hp)
    hext_ref[pl.ds(0, BF16_ROWS), :] = jnp.concatenate([zeros, hp], axis=0).astype(BF16)
    hext_ref[pl.ds(BF16_ROWS + tm, BF16_ROWS), :] = jnp.concatenate([hn, zeros], axis=0).astype(BF16)

    def up(c, u_ref):
        for k in range(2):
            off = pl.multiple_of(k * D_FF + c * FF_CHUNK, FF_CHUNK)
            u_ref[k] = _dot(hext_ref[...], wup_ref[:, pl.ds(off, FF_CHUNK)])

    def conv(u_ref, k, off):
        sl = pl.ds(off, FF_CHUNK)
        return (cw_ref[0:1, sl] * u_ref[k, pl.ds(SUBLANES, tm), :]
                + cw_ref[1:2, sl] * u_ref[k, pl.ds(2 * SUBLANES, tm), :]
                + cw_ref[2:3, sl] * u_ref[k, pl.ds(3 * SUBLANES, tm), :]
                + cb_ref[:, sl])

    def activate(c, u_ref):
        og = pl.multiple_of(c * FF_CHUNK, FF_CHUNK)
        ov = pl.multiple_of(D_FF + c * FF_CHUNK, FF_CHUNK)
        yg = conv(u_ref, 0, og)
        yv = conv(u_ref, 1, ov)
        hg = 0.5 * yg
        act_ref[:, pl.ds(og, FF_CHUNK)] = ((hg + hg * jnp.tanh(hg)) * yv).astype(BF16)

    n_chunks = D_FF // FF_CHUNK
    up(0, ua_ref)

    def pair(j, _):
        up(2 * j + 1, ub_ref)
        activate(2 * j, ua_ref)
        up(2 * j + 2, ua_ref)
        activate(2 * j + 1, ub_ref)
        return 0

    for j in range((n_chunks - 1) // 2):
        pair(j, 0)
    activate(n_chunks - 1, ua_ref)
    out = x + gate * _dot(act_ref[...], wdn_ref[...])
    if not final_norm:
        o_ref[...] = out
        return
    out = _rms(out) * fnw_ref[...]
    for k in range(D_MODEL // LANES):
        slab_ref[k] = out[:, LANES * k:LANES * (k + 1)]
    for s in range(SUBLANES):
        for k in range(D_MODEL // LANES):
            o_ref[s, :, LANES * k:LANES * (k + 1)] = slab_ref[k, pl.ds(s, tm // SUBLANES, stride=SUBLANES), :]


def _ffn(x, mod, w_up, conv_w, conv_b, w_down, final_w, *, n_seq, seq_len, shared_mod, mod_row, layer):
    tm = min(1024, seq_len)
    tps = seq_len // tm
    rows = n_seq * seq_len
    gps = seq_len // SUBLANES
    gpt = tm // SUBLANES
    mod_idx = (lambda i: (mod_row, 0, 1)) if shared_mod else (lambda i: (mod_row + i // tps, 0, 1))

    def prev_idx(i):
        jt = i % tps
        return ((i // tps) * gps + jnp.where(jt == 0, gps - 1, jt * gpt - 1), 0)

    def next_idx(i):
        jt = i % tps
        return ((i // tps) * gps + jnp.where(jt == tps - 1, 0, (jt + 1) * gpt), 0)

    final_norm = final_w is not None
    in_specs = [pl.BlockSpec((tm, D_MODEL), lambda i: (i, 0)),
                pl.BlockSpec((SUBLANES, D_MODEL), prev_idx),
                pl.BlockSpec((SUBLANES, D_MODEL), next_idx),
                pl.BlockSpec((None, 1, 3 * D_MODEL), mod_idx),
                _resident((D_MODEL, 2 * D_FF), layer),
                _resident((3, 2 * D_FF), layer),
                _resident((1, 2 * D_FF), layer),
                _resident((D_FF, D_MODEL), layer)]
    args = [x, x, x, mod, w_up, conv_w, conv_b, w_down]
    if final_norm:
        in_specs.append(_resident((1, D_MODEL)))
        args.append(final_w)
    kern = functools.partial(_ffn_kernel, tm=tm, tps=tps, final_norm=final_norm)
    scratch = ([pltpu.VMEM((tm + 2 * BF16_ROWS, D_MODEL), BF16), pltpu.VMEM((tm, D_FF), BF16)]
               + [pltpu.VMEM((2, tm + 2 * BF16_ROWS, FF_CHUNK), F32)] * 2)
    if final_norm:
        out_spec = pl.BlockSpec((None, SUBLANES, tm // SUBLANES, D_MODEL),
                                lambda i: (i // tps, 0, i % tps, 0))
        out_shape = jax.ShapeDtypeStruct((n_seq, SUBLANES, gps, D_MODEL), F32)
        scratch.append(pltpu.VMEM((D_MODEL // LANES, tm, LANES), F32))
    else:
        out_spec = pl.BlockSpec((tm, D_MODEL), lambda i: (i, 0))
        out_shape = jax.ShapeDtypeStruct((rows, D_MODEL), F32)
    return pl.pallas_call(
        kern,
        grid=(rows // tm,),
        in_specs=in_specs,
        out_specs=out_spec,
        out_shape=out_shape,
        scratch_shapes=scratch,
        compiler_params=_params("arbitrary"),
        name="conv_ffn_final" if final_norm else "conv_ffn",
    )(*args)


def _to_segments(x):
    b, t, c = x.shape
    return x.reshape(b, SUBLANES, t // SUBLANES, c).transpose(0, 2, 1, 3).reshape(b * t, c)


def _from_segments(y, b, t):
    return y.reshape(b, t // SUBLANES, SUBLANES, -1).transpose(0, 2, 1, 3).reshape(b, t, -1)


def _rope_tables(seq_len):
    r = np.arange(seq_len)
    t = (r % SUBLANES) * (seq_len // SUBLANES) + r // SUBLANES
    row = (t // GRID_W).astype(np.float64)
    col = (t % GRID_W).astype(np.float64)
    pairs = HEAD_DIM // 4
    inv = ROPE_THETA ** (-np.arange(pairs, dtype=np.float64) / pairs)
    ang = np.concatenate([row[:, None] * inv, col[:, None] * inv], axis=-1)
    cos, sin = np.cos(ang), np.sin(ang)
    return (jnp.asarray(np.concatenate([cos, cos, cos, cos], axis=-1), F32),
            jnp.asarray(np.concatenate([-sin, sin, -sin, sin], axis=-1), F32))


def kernel(x, c, ctx, c_ctx, w_ada, b_ada, w_in, q_norm_w, k_norm_w, lru_conv_w, lru_conv_b, lru_wa,
           lru_ba, lru_wx, lru_bx, lru_lambda, attn_out_norm_w, lru_out_norm_w, w_out, ffn_w_up,
           ffn_conv_w, ffn_conv_b, ffn_w_down, final_norm_w):
    batch, seq, _ = x.shape
    ctx_len = ctx.shape[1]
    depth = w_in.shape[0]

    mod_rows = -(-(batch + 1) // SUBLANES) * SUBLANES
    cvec = jnp.zeros((mod_rows, D_MODEL), F32).at[:batch].set(c).at[batch].set(c_ctx)
    mods = _modulation(cvec, w_ada, b_ada)

    cos, sin = _rope_tables(seq)
    ones = jnp.asarray(np.ones((ctx_len, LANES)), F32)
    bd = jnp.asarray(np.kron(np.eye(QK_W // HEAD_DIM), np.ones((HEAD_DIM, HEAD_DIM))), BF16)

    x_lat = _to_segments(x)
    x_ctx = _to_segments(ctx)
    mods = mods.reshape(depth * mod_rows, 1, 6 * D_MODEL)
    w_up_b = ffn_w_up.astype(BF16)
    w_dn_b = ffn_w_down.astype(BF16)
    conv_b = ffn_conv_b.reshape(depth, 1, 2 * D_FF)

    for l in range(depth):
        ctx_out = l < depth - 1
        last = l == depth - 1
        lat = dict(n_seq=batch, seq_len=seq, shared_mod=False, mod_row=l * mod_rows, layer=l)
        cx = dict(n_seq=batch, seq_len=ctx_len, shared_mod=True, mod_row=l * mod_rows + batch, layer=l)
        nw_qk = jnp.concatenate([jnp.tile(q_norm_w[l], D_ATTN // HEAD_DIM),
                                 jnp.tile(k_norm_w[l], KV_W // HEAD_DIM)]).reshape(1, QK_W)
        qt_l, k_l, vt_l, xr_l, g_l = _inproj(x_lat, mods, w_in, bd, nw_qk, cos, sin,
                                             use_rope=True, **lat)
        qt_c, k_c, vt_c, xr_c, g_c = _inproj(x_ctx, mods, w_in, bd, nw_qk, ones, ones,
                                             use_rope=False, **cx)

        a_l = _attention(qt_l, k_c, vt_c, k_l, vt_l, n_seq=batch, q_len=seq, ctx_len=ctx_len,
                         lat_len=seq)
        r = _lru(xr_c, g_c, xr_l, g_l, lru_conv_w, lru_conv_b.reshape(depth, 1, D_LRU), lru_wa, lru_wx,
                 lru_ba, lru_bx, lru_lambda, n_seq=batch, ctx_len=ctx_len, lat_len=seq,
                 ctx_out=ctx_out, layer=l)

        nw_out = jnp.concatenate([attn_out_norm_w[l], lru_out_norm_w[l]]).reshape(1, D_MODEL)
        x_lat = _outproj(a_l, r[0], x_lat, mods, nw_out, w_out, **lat)
        x_lat = _ffn(x_lat, mods, w_up_b, ffn_conv_w, conv_b, w_dn_b,
                     final_norm_w.reshape(1, D_MODEL) if last else None, **lat)
        if ctx_out:
            a_c = _attention(qt_c, k_c, vt_c, None, None, n_seq=batch, q_len=ctx_len,
                             ctx_len=ctx_len, lat_len=0)
            x_ctx = _outproj(a_c, r[1], x_ctx, mods, nw_out, w_out, **cx)
            x_ctx = _ffn(x_ctx, mods, w_up_b, ffn_conv_w, conv_b, w_dn_b, None, **cx)
    return x_lat.reshape(batch, seq, D_MODEL)
```

```python
import functools

import jax
import jax.numpy as jnp
import numpy as np
from jax import lax
from jax.experimental import pallas as pl
from jax.experimental.pallas import tpu as pltpu

F32 = jnp.float32
BF16 = jnp.bfloat16

D_MODEL = 1024
D_ATTN = 512
D_LRU = 512
KV_W = 128
HEAD_DIM = 64
N_HEADS = 8
N_KV = 2
KV_GROUP = 4
LOG2E = 1.4426950408889634
LN2 = 0.6931471805599453
GELU_C0 = 0.7978845608028654
GELU_C1 = GELU_C0 * 0.044715
QK_W = D_ATTN + KV_W
D_IN = D_ATTN + 2 * KV_W + 2 * D_LRU
D_FF = 2816
LRU_C = 8.0
GRID_W = 64
ROPE_THETA = 10000.0
EPS = 1e-6
SUBLANES = 8
LANES = 128
BF16_ROWS = 16
V_ROWS = HEAD_DIM + BF16_ROWS
K_COLS = LANES
MAX_SHIFTED_WEIGHT = 2.0 ** 60
FF_CHUNK = 256
SUB_ROWS = 256
LRU_SLAB = 256
SCAN_BLOCK = 8
VMEM_LIMIT = 58 * 1024 * 1024


def _dot(a, b):
    return jnp.dot(a, b, preferred_element_type=F32)


def _sigmoid(z):
    return 0.5 * jnp.tanh(0.5 * z) + 0.5


def _rms(x):
    return x * lax.rsqrt(jnp.mean(x * x, axis=-1, keepdims=True) + EPS)


def _shift_down(g):
    rows = lax.broadcasted_iota(jnp.int32, g.shape, 0)
    return jnp.where(rows == 0, 0.0, pltpu.roll(g, 1, 0))


def _shift_up(g):
    rows = lax.broadcasted_iota(jnp.int32, g.shape, 0)
    return jnp.where(rows == SUBLANES - 1, 0.0, pltpu.roll(g, SUBLANES - 1, 0))


def _n_sub(tile_rows):
    return max(2, tile_rows // SUB_ROWS)


def _params(*sem):
    return pltpu.CompilerParams(dimension_semantics=sem, vmem_limit_bytes=VMEM_LIMIT)


def _resident(shape, layer=None):
    nd = len(shape)
    if layer is None:
        return pl.BlockSpec(shape, lambda *_: (0,) * nd, pipeline_mode=pl.Buffered(1))
    return pl.BlockSpec((None,) + tuple(shape), lambda *_: (layer,) + (0,) * nd,
                        pipeline_mode=pl.Buffered(1))


def _mod_kernel(c_ref, w_ref, b_ref, o_ref):
    c = c_ref[...]
    s = (c * _sigmoid(c)).astype(BF16)
    o_ref[...] = _dot(s, w_ref[...].astype(BF16)) + b_ref[...]


def _modulation(cvec, w_ada, b_ada, tn=1536):
    depth, d, n = w_ada.shape
    rows = cvec.shape[0]
    return pl.pallas_call(
        _mod_kernel,
        grid=(depth, n // tn),
        in_specs=[pl.BlockSpec((rows, d), lambda l, j: (0, 0)),
                  pl.BlockSpec((None, d, tn), lambda l, j: (l, 0, j)),
                  pl.BlockSpec((None, 1, tn), lambda l, j: (l, 0, j))],
        out_specs=pl.BlockSpec((None, rows, tn), lambda l, j: (l, 0, j)),
        out_shape=jax.ShapeDtypeStruct((depth, rows, n), F32),
        compiler_params=_params("arbitrary", "arbitrary"),
        name="adaln_mod",
    )(cvec, w_ada, b_ada.reshape(depth, 1, n))


def _inproj_kernel(x_ref, mod_ref, w_ref, bd_ref, nw_ref, cos_ref, sin_ref,
                   qt_ref, k_ref, vt_ref, xl_ref, g_ref, *, use_rope):
    shift = mod_ref[:, 0:D_MODEL]
    scale = mod_ref[:, D_MODEL:2 * D_MODEL]
    n_sub = _n_sub(x_ref.shape[0])
    sub = x_ref.shape[0] // n_sub
    w = w_ref[...].astype(BF16)

    def project(r0):
        h = _rms(x_ref[pl.ds(r0, sub), :]) * (1.0 + scale) + shift
        y = _dot(h.astype(BF16), w)
        qk = y[:, 0:QK_W]
        return y, _dot((qk * qk).astype(BF16), bd_ref[...])

    def finish(r0, y_ss):
        y, ss = y_ss
        rows = pl.ds(r0, sub)
        qk = y[:, 0:QK_W]
        qk = qk * lax.rsqrt(ss * (1.0 / HEAD_DIM) + EPS) * nw_ref[...]
        if use_rope:
            lane = lax.broadcasted_iota(jnp.int32, (sub, LANES), 1)
            first_half = (lane % HEAD_DIM) < (HEAD_DIM // 2)
            cos = cos_ref[rows, :]
            sin = sin_ref[rows, :]
            cols = []
            for k in range(QK_W // LANES):
                xc = qk[:, LANES * k:LANES * (k + 1)]
                other = jnp.where(first_half, pltpu.roll(xc, LANES - HEAD_DIM // 2, 1),
                                  pltpu.roll(xc, HEAD_DIM // 2, 1))
                cols.append(xc * cos + other * sin)
            qk = jnp.concatenate(cols, axis=1)
        qt = (qk[:, 0:D_ATTN] * (LOG2E * HEAD_DIM ** -0.5)).T
        for h in range(N_HEADS):
            qt_ref[h, :, rows] = qt[HEAD_DIM * h:HEAD_DIM * (h + 1)].astype(BF16)
        kk = qk[:, D_ATTN:QK_W]
        lane = lax.broadcasted_iota(jnp.int32, kk.shape, 1)
        one_hot = jnp.where(lane == HEAD_DIM, 1.0, 0.0)
        for g in range(N_KV):
            kg = kk if g == 0 else pltpu.roll(kk, HEAD_DIM, 1)
            k_ref[g, rows, :] = jnp.where(lane < HEAD_DIM, kg, one_hot).astype(BF16)
        vt = y[:, QK_W:QK_W + KV_W].T.astype(BF16)
        ones = jnp.ones((V_ROWS - HEAD_DIM, sub), BF16)
        for g in range(N_KV):
            vt_ref[g, :, rows] = jnp.concatenate([vt[HEAD_DIM * g:HEAD_DIM * (g + 1)], ones], axis=0)
        xl_ref[rows, :] = y[:, QK_W + KV_W:QK_W + KV_W + D_LRU]
        gt = y[:, QK_W + KV_W + D_LRU:D_IN]
        hg = 0.5 * gt
        g_ref[rows, :] = hg + hg * jnp.tanh(gt * (GELU_C0 + GELU_C1 * (gt * gt)))

    y = project(0)
    for s in range(n_sub):
        y_next = project((s + 1) * sub) if s + 1 < n_sub else None
        finish(s * sub, y)
        y = y_next


def _inproj(x, mod, w_in, bd, nw, cos, sin, *, n_seq, seq_len, shared_mod, mod_row, layer,
            use_rope):
    tm = min(1024, seq_len)
    tps = seq_len // tm
    rows = n_seq * seq_len
    mod_idx = (lambda i: (mod_row, 0, 0)) if shared_mod else (lambda i: (mod_row + i // tps, 0, 0))
    kern = functools.partial(_inproj_kernel, use_rope=use_rope)
    return pl.pallas_call(
        kern,
        grid=(rows // tm,),
        in_specs=[pl.BlockSpec((tm, D_MODEL), lambda i: (i, 0)),
                  pl.BlockSpec((None, 1, 2 * D_MODEL), mod_idx),
                  _resident((D_MODEL, D_IN), layer),
                  _resident((QK_W, QK_W)),
                  _resident((1, QK_W)),
                  pl.BlockSpec((tm, LANES), lambda i: (i % tps, 0)),
                  pl.BlockSpec((tm, LANES), lambda i: (i % tps, 0))],
        out_specs=[pl.BlockSpec((None, N_HEADS, HEAD_DIM, tm), lambda i: (i // tps, 0, 0, i % tps)),
                   pl.BlockSpec((None, N_KV, tm, K_COLS), lambda i: (i // tps, 0, i % tps, 0)),
                   pl.BlockSpec((None, N_KV, V_ROWS, tm), lambda i: (i // tps, 0, 0, i % tps)),
                   pl.BlockSpec((tm, D_LRU), lambda i: (i, 0)),
                   pl.BlockSpec((tm, D_LRU), lambda i: (i, 0))],
        out_shape=[jax.ShapeDtypeStruct((n_seq, N_HEADS, HEAD_DIM, seq_len), BF16),
                   jax.ShapeDtypeStruct((n_seq, N_KV, seq_len, K_COLS), BF16),
                   jax.ShapeDtypeStruct((n_seq, N_KV, V_ROWS, seq_len), BF16),
                   jax.ShapeDtypeStruct((rows, D_LRU), F32),
                   jax.ShapeDtypeStruct((rows, D_LRU), F32)],
        compiler_params=_params("arbitrary"),
        name="in_proj_rope" if use_rope else "in_proj",
    )(x, mod, w_in, bd, nw, cos, sin)


def _attn_kernel(*refs, tq, tk, n_lat_chunks, q_tiles):
    if n_lat_chunks:
        qt_ref, kc_ref, vtc_ref, kl_ref, vtl_ref, o_ref = refs
    else:
        qt_ref, kc_ref, vtc_ref, o_ref = refs
    n = KV_GROUP * tq

    def queries(u):
        tile, g = u
        cols = slice(tile * tq, (tile + 1) * tq)
        return jnp.concatenate([qt_ref[KV_GROUP * g + h, :, cols] for h in range(KV_GROUP)], axis=1)

    def ctx_scores(g, qt):
        return _dot(kc_ref[g, :, 0:HEAD_DIM], qt)

    def finish(u, acc):
        tile, g = u
        ot = acc[0:HEAD_DIM] * (1.0 / acc[HEAD_DIM:HEAD_DIM + 1])
        for k in range(KV_GROUP):
            h = KV_GROUP * g + k
            o_ref[tile * tq:(tile + 1) * tq, HEAD_DIM * h:HEAD_DIM * (h + 1)] = (
                ot[:, k * tq:(k + 1) * tq].T.astype(o_ref.dtype))

    def exact(u):
        g = u[1]
        qt = queries(u)
        s = ctx_scores(g, qt)
        m = jnp.max(s, axis=0, keepdims=True)
        acc = _dot(vtc_ref[g], jnp.exp2(s - m).astype(BF16))

        def body(c, carry):
            m, acc = carry
            off = pl.multiple_of(c * tk, tk)
            s = _dot(kl_ref[g, pl.ds(off, tk), 0:HEAD_DIM], qt)
            m_new = jnp.maximum(m, jnp.max(s, axis=0, keepdims=True))
            p = jnp.exp2(s - m_new).astype(BF16)
            return m_new, jnp.exp2(m - m_new) * acc + _dot(vtl_ref[g, :, pl.ds(off, tk)], p)

        if n_lat_chunks:
            m, acc = lax.fori_loop(0, n_lat_chunks, body, (m, acc))
        return acc

    units = [(tile, g) for tile in range(q_tiles) for g in range(N_KV)]
    if not n_lat_chunks:
        for u in units:
            finish(u, exact(u))
        return

    row = lax.broadcasted_iota(jnp.int32, (BF16_ROWS, n), 0)
    pad = jnp.zeros((K_COLS - HEAD_DIM - BF16_ROWS, n), BF16)

    def start(u):
        g = u[1]
        qt = queries(u)
        s = ctx_scores(g, qt)
        running = jnp.max(s, axis=0, keepdims=True)
        shift = running.astype(BF16).astype(F32)
        acc = _dot(vtc_ref[g], jnp.exp2(s - shift).astype(BF16))
        return dict(qt=qt, running=running, shift=shift, shifts=[shift, shift], acc=acc,
                    largest=jnp.ones((1, n), F32))

    def issue(g, c, st):
        new_shift = st["shifts"].pop(0)
        shift_rows = jnp.where(row == 0, -new_shift, 0.0).astype(BF16)
        qa = jnp.concatenate([st["qt"], shift_rows, pad], axis=0)
        st["issued"] = new_shift, _dot(kl_ref[g, pl.ds(c * tk, tk), :], qa)

    def consume(g, c, st):
        new_shift, s = st.pop("issued")
        p = jnp.exp2(s)
        pv = _dot(vtl_ref[g, :, pl.ds(c * tk, tk)], p.astype(BF16))
        st["acc"] = st["acc"] * jnp.exp2(st["shift"] - new_shift) + pv
        st["shift"] = new_shift
        pmax = jnp.max(p, axis=0, keepdims=True)
        st["largest"] = jnp.maximum(st["largest"], pmax)
        st["running"] = jnp.maximum(st["running"], new_shift + jnp.log2(pmax))
        st["shifts"].append(st["running"].astype(BF16).astype(F32))

    largest = None
    states = {u: start(u) for u in units if u[0] == 0}
    for tile in range(q_tiles):
        items = [(g, c) for c in range(n_lat_chunks) for g in range(N_KV)]
        issue(*items[0], states[(tile, items[0][0])])
        for k, (g, c) in enumerate(items):
            if k + 1 < len(items):
                gn, cn = items[k + 1]
                issue(gn, cn, states[(tile, gn)])
            consume(g, c, states[(tile, g)])
        if tile + 1 < q_tiles:
            states.update({u: start(u) for u in units if u[0] == tile + 1})
        for g in range(N_KV):
            st = states.pop((tile, g))
            finish((tile, g), st["acc"])
            largest = st["largest"] if largest is None else jnp.maximum(largest, st["largest"])
    overflow = jnp.max(largest) > MAX_SHIFTED_WEIGHT

    @pl.when(overflow)
    def _():
        for u in units:
            finish(u, exact(u))


def _attention(qt, k_ctx, vt_ctx, k_lat, vt_lat, *, n_seq, q_len, ctx_len, lat_len):
    tq = min(256, q_len)
    tk = min(512, lat_len // 2) if lat_len else 0
    n_lat_chunks = lat_len // tk if lat_len else 0
    q_tiles = 2 if (q_len // tq) % 2 == 0 else 1
    tqs = q_tiles * tq
    nq = q_len // tqs
    kern = functools.partial(_attn_kernel, tq=tq, tk=tk, n_lat_chunks=n_lat_chunks, q_tiles=q_tiles)
    in_specs = [pl.BlockSpec((None, N_HEADS, HEAD_DIM, tqs), lambda b, i: (b, 0, 0, i)),
                pl.BlockSpec((None, N_KV, ctx_len, K_COLS), lambda b, i: (b, 0, 0, 0)),
                pl.BlockSpec((None, N_KV, V_ROWS, ctx_len), lambda b, i: (b, 0, 0, 0))]
    args = [qt, k_ctx, vt_ctx]
    scratch = []
    if lat_len:
        in_specs += [pl.BlockSpec((None, N_KV, lat_len, K_COLS), lambda b, i: (b, 0, 0, 0)),
                     pl.BlockSpec((None, N_KV, V_ROWS, lat_len), lambda b, i: (b, 0, 0, 0))]
        args += [k_lat, vt_lat]
    return pl.pallas_call(
        kern,
        grid=(n_seq, nq),
        in_specs=in_specs,
        out_specs=pl.BlockSpec((tqs, D_ATTN), lambda b, i: (b * nq + i, 0)),
        out_shape=jax.ShapeDtypeStruct((n_seq * q_len, D_ATTN), BF16),
        scratch_shapes=scratch,
        compiler_params=_params("arbitrary", "arbitrary"),
        name="attn_latent" if lat_len else "attn_context",
    )(*args)


def _lru_sequence(x_ref, g_ref, o_ref, h0, consts, scr, *, n_groups):
    cw_ref, cb_ref, wg_ref, bg_ref, hnsp = consts
    xe_ref, af_ref, bf_ref, ar_ref, br_ref = scr
    rows = SUBLANES * n_groups
    width = x_ref.shape[1]
    chunk = min(256, rows)
    n_chunks = rows // chunk

    xe_ref[pl.ds(2 * SUBLANES, rows), :] = x_ref[...]
    xe_ref[pl.ds(0, SUBLANES), :] = _shift_down(x_ref[pl.ds(rows - 2 * SUBLANES, SUBLANES), :])
    xe_ref[pl.ds(SUBLANES, SUBLANES), :] = _shift_down(x_ref[pl.ds(rows - SUBLANES, SUBLANES), :])
    xe_ref[pl.ds(rows + 2 * SUBLANES, SUBLANES), :] = _shift_up(x_ref[pl.ds(0, SUBLANES), :])

    def gates(c, _):
        r0 = pl.multiple_of(c * chunk, chunk)
        xc = cb_ref[...]
        for k in range(4):
            xc = xc + cw_ref[k:k + 1, :] * xe_ref[pl.ds(r0 + SUBLANES * k, chunk), :]
        t = jnp.tanh(_dot(xc.astype(BF16), wg_ref[...]) + bg_ref[...])
        hx = 0.5 * xc
        for d, (a_ref, b_ref) in enumerate(((af_ref, bf_ref), (ar_ref, br_ref))):
            t_r = t[:, (2 * d) * width:(2 * d + 1) * width]
            t_i = t[:, (2 * d + 1) * width:(2 * d + 2) * width]
            log2_a = hnsp[d] * t_r + hnsp[d]
            a = jnp.exp2(log2_a)
            y = jnp.tanh(LN2 * log2_a) * (-1.0 - a * a)
            b = jnp.where(y > 0.0, y * lax.rsqrt(y), 0.0) * (hx * t_i + hx)
            a_ref[pl.ds(r0, chunk), :] = a
            b_ref[pl.ds(r0, chunk), :] = b
        return 0

    lax.fori_loop(0, n_chunks, gates, 0)

    blk = min(SCAN_BLOCK, n_groups)
    blk_rows = blk * SUBLANES
    n_blocks = n_groups // blk

    def scan_block(a_ref, b_ref, r0, h, p, order):
        a_blk = a_ref[pl.ds(r0, blk_rows), :]
        b_blk = b_ref[pl.ds(r0, blk_rows), :]
        hs, ps = [None] * blk, [None] * blk
        for k in order:
            a = a_blk[k * SUBLANES:(k + 1) * SUBLANES]
            h = a * h + b_blk[k * SUBLANES:(k + 1) * SUBLANES]
            p = a * p
            hs[k], ps[k] = h, p
        b_ref[pl.ds(r0, blk_rows), :] = jnp.concatenate(hs, axis=0)
        a_ref[pl.ds(r0, blk_rows), :] = jnp.concatenate(ps, axis=0)
        return h, p

    def scan(i, carry):
        hf, pf, hr, pr = carry
        rf = pl.multiple_of(i * blk_rows, blk_rows)
        rr = pl.multiple_of((n_blocks - 1 - i) * blk_rows, blk_rows)
        hf, pf = scan_block(af_ref, bf_ref, rf, hf, pf, range(blk))
        hr, pr = scan_block(ar_ref, br_ref, rr, hr, pr, range(blk - 1, -1, -1))
        return hf, pf, hr, pr

    zero = jnp.zeros((SUBLANES, width), F32)
    one = jnp.ones((SUBLANES, width), F32)
    hf, pf, hr, pr = lax.fori_loop(0, n_blocks, scan, (zero, one, zero, one))

    e = h0[0]
    init_f = []
    for s in range(SUBLANES):
        init_f.append(e)
        e = hf[s:s + 1, :] + pf[s:s + 1, :] * e
    fin_f = e
    e = h0[1]
    init_r = [None] * SUBLANES
    for s in range(SUBLANES - 1, -1, -1):
        init_r[s] = e
        e = hr[s:s + 1, :] + pr[s:s + 1, :] * e
    fin_r = e

    if o_ref is not None:
        init_f = jnp.concatenate(init_f, axis=0)[None]
        init_r = jnp.concatenate(init_r, axis=0)[None]

        def combine(c, _):
            r0 = pl.multiple_of(c * chunk, chunk)
            sl = pl.ds(r0, chunk)
            shp = (chunk // SUBLANES, SUBLANES, width)
            h = (bf_ref[sl, :].reshape(shp) + af_ref[sl, :].reshape(shp) * init_f
                 + br_ref[sl, :].reshape(shp) + ar_ref[sl, :].reshape(shp) * init_r)
            o_ref[sl, :] = (h.reshape(chunk, width) * g_ref[sl, :]).astype(o_ref.dtype)
            return 0

        lax.fori_loop(0, n_chunks, combine, 0)
    return fin_f, fin_r


def _lru_kernel(*refs, ctx_groups, lat_groups, ctx_out):
    xc_ref, gc_ref, xl_ref, gl_ref, cw_ref, cb_ref, wa_ref, wx_ref, ba_ref, bx_ref, lam_ref = refs[:11]
    n_out = 2 if ctx_out else 1
    outs = refs[11:11 + n_out]
    scr = refs[11 + n_out:-2]
    wg_ref, bg_ref = refs[-2:]
    width = xc_ref.shape[1]
    blk = wa_ref.shape[-1]
    wg_ref[...] = jnp.zeros(wg_ref.shape, BF16)
    for t, (w_ref, b_ref, d) in enumerate(((wa_ref, ba_ref, 0), (wx_ref, bx_ref, 0),
                                           (wa_ref, ba_ref, 1), (wx_ref, bx_ref, 1))):
        for p in range(width // blk):
            wg_ref[blk * p:blk * (p + 1), t * width + blk * p:t * width + blk * (p + 1)] = (
                (0.5 * w_ref[d, p]).astype(BF16))
        bg_ref[:, t * width:(t + 1) * width] = 0.5 * b_ref[d:d + 1, :]
    z = -lam_ref[...]
    softplus = jnp.maximum(z, 0.0) + jnp.log1p(jnp.exp(-jnp.abs(z)))
    hnsp = [(-0.5 * LRU_C * LOG2E) * softplus[d:d + 1, :] for d in range(2)]
    consts = (cw_ref, cb_ref, wg_ref, bg_ref, hnsp)
    zero = jnp.zeros((1, xc_ref.shape[1]), F32)
    fin = _lru_sequence(xc_ref, gc_ref, outs[1] if ctx_out else None, (zero, zero), consts, scr,
                        n_groups=ctx_groups)
    _lru_sequence(xl_ref, gl_ref, outs[0], fin, consts, scr, n_groups=lat_groups)


def _lru(x_ctx, g_ctx, x_lat, g_lat, conv_w, conv_b, wa, wx, ba, bx, lam, *, n_seq, ctx_len, lat_len,
         ctx_out, layer):
    w = LRU_SLAB
    n_slab = D_LRU // w
    blk = wa.shape[-1]
    per = w // blk

    def vec(rows):
        return pl.BlockSpec((None, rows, w), lambda b, c: (layer, 0, c))

    gate_w = pl.BlockSpec((None, 2, per, blk, blk), lambda b, c: (layer, 0, c, 0, 0))
    kern = functools.partial(_lru_kernel, ctx_groups=ctx_len // SUBLANES,
                             lat_groups=lat_len // SUBLANES, ctx_out=ctx_out)
    ctx_spec = pl.BlockSpec((ctx_len, w), lambda b, c: (b, c))
    lat_spec = pl.BlockSpec((lat_len, w), lambda b, c: (b, c))
    out_specs = [lat_spec]
    out_shape = [jax.ShapeDtypeStruct((n_seq * lat_len, D_LRU), BF16)]
    if ctx_out:
        out_specs.append(ctx_spec)
        out_shape.append(jax.ShapeDtypeStruct((n_seq * ctx_len, D_LRU), BF16))
    max_rows = max(ctx_len, lat_len)
    return pl.pallas_call(
        kern,
        grid=(n_seq, n_slab),
        in_specs=[ctx_spec, ctx_spec, lat_spec, lat_spec,
                  vec(conv_w.shape[1]), vec(1), gate_w, gate_w, vec(2), vec(2), vec(2)],
        out_specs=out_specs,
        out_shape=out_shape,
        scratch_shapes=[pltpu.VMEM((max_rows + 3 * SUBLANES, w), F32)]
                       + [pltpu.VMEM((max_rows, w), F32)] * 4
                       + [pltpu.VMEM((w, 4 * w), BF16), pltpu.VMEM((1, 4 * w), F32)],
        compiler_params=_params("arbitrary", "arbitrary"),
        name="rglru_bidir",
    )(x_ctx, g_ctx, x_lat, g_lat, conv_w, conv_b, wa, wx, ba, bx, lam)


def _outproj_kernel(a_ref, r_ref, x_ref, gate_ref, nw_ref, w_ref, o_ref):
    n_sub = _n_sub(x_ref.shape[0])
    sub = x_ref.shape[0] // n_sub
    w = w_ref[...].astype(BF16)

    def project(r0):
        rows = pl.ds(r0, sub)
        cat = jnp.concatenate([_rms(a_ref[rows, :].astype(F32)), _rms(r_ref[rows, :].astype(F32))],
                              axis=1) * nw_ref[...]
        return _dot(cat.astype(BF16), w)

    y = project(0)
    for s in range(n_sub):
        y_next = project((s + 1) * sub) if s + 1 < n_sub else None
        rows = pl.ds(s * sub, sub)
        o_ref[rows, :] = x_ref[rows, :] + gate_ref[...] * y
        y = y_next


def _outproj(a, r, x, mod, nw, w_out, *, n_seq, seq_len, shared_mod, mod_row, layer):
    tm = min(1024, seq_len)
    tps = seq_len // tm
    rows = n_seq * seq_len
    gate_idx = (lambda i: (mod_row, 0, 2)) if shared_mod else (lambda i: (mod_row + i // tps, 0, 2))
    return pl.pallas_call(
        _outproj_kernel,
        grid=(rows // tm,),
        in_specs=[pl.BlockSpec((tm, D_ATTN), lambda i: (i, 0)),
                  pl.BlockSpec((tm, D_LRU), lambda i: (i, 0)),
                  pl.BlockSpec((tm, D_MODEL), lambda i: (i, 0)),
                  pl.BlockSpec((None, 1, D_MODEL), gate_idx),
                  _resident((1, D_MODEL)),
                  _resident((D_MODEL, D_MODEL), layer)],
        out_specs=pl.BlockSpec((tm, D_MODEL), lambda i: (i, 0)),
        out_shape=jax.ShapeDtypeStruct((rows, D_MODEL), F32),
        compiler_params=_params("arbitrary"),
        name="out_proj",
    )(a, r, x, mod, nw, w_out)


def _ffn_kernel(*refs, tm, tps, final_norm):
    x_ref, xp_ref, xn_ref, mod_ref, wup_ref, cw_ref, cb_ref, wdn_ref = refs[:8]
    if final_norm:
        fnw_ref, o_ref, hext_ref, act_ref, ua_ref, ub_ref, slab_ref = refs[8:]
    else:
        o_ref, hext_ref, act_ref, ua_ref, ub_ref = refs[8:]
    assert (D_FF // FF_CHUNK) % 2 == 1
    jt = pl.program_id(0) % tps
    shift = mod_ref[:, 0:D_MODEL]
    scale = mod_ref[:, D_MODEL:2 * D_MODEL]
    gate = mod_ref[:, 2 * D_MODEL:3 * D_MODEL]

    def norm_mod(x):
        return _rms(x) * (1.0 + scale) + shift

    x = x_ref[...]
    hext_ref[pl.ds(BF16_ROWS, tm), :] = norm_mod(x).astype(BF16)
    hp = norm_mod(xp_ref[...])
    hp = jnp.where(jt == 0, _shift_down(hp), hp)
    hn = norm_mod(xn_ref[...])
    hn = jnp.where(jt == tps - 1, _shift_up(hn), hn)
    zeros = jnp.zeros_like(hp)
    hext_ref[pl.ds(0, BF16_ROWS), :] = jnp.concatenate([zeros, hp], axis=0).astype(BF16)
    hext_ref[pl.ds(BF16_ROWS + tm, BF16_ROWS), :] = jnp.concatenate([hn, zeros], axis=0).astype(BF16)

    def up(c, u_ref):
        for k in range(2):
            off = pl.multiple_of(k * D_FF + c * FF_CHUNK, FF_CHUNK)
            u_ref[k] = _dot(hext_ref[...], wup_ref[:, pl.ds(off, FF_CHUNK)])

    def conv(u_ref, k, off):
        sl = pl.ds(off, FF_CHUNK)
        return (cw_ref[0:1, sl] * u_ref[k, pl.ds(SUBLANES, tm), :]
                + cw_ref[1:2, sl] * u_ref[k, pl.ds(2 * SUBLANES, tm), :]
                + cw_ref[2:3, sl] * u_ref[k, pl.ds(3 * SUBLANES, tm), :]
                + cb_ref[:, sl])

    def activate(c, u_ref):
        og = pl.multiple_of(c * FF_CHUNK, FF_CHUNK)
        ov = pl.multiple_of(D_FF + c * FF_CHUNK, FF_CHUNK)
        yg = conv(u_ref, 0, og)
        yv = conv(u_ref, 1, ov)
        hg = 0.5 * yg
        act_ref[:, pl.ds(og, FF_CHUNK)] = ((hg + hg * jnp.tanh(hg)) * yv).astype(BF16)

    n_chunks = D_FF // FF_CHUNK
    up(0, ua_ref)

    def pair(j, _):
        up(2 * j + 1, ub_ref)
        activate(2 * j, ua_ref)
        up(2 * j + 2, ua_ref)
        activate(2 * j + 1, ub_ref)
        return 0

    for j in range((n_chunks - 1) // 2):
        pair(j, 0)
    activate(n_chunks - 1, ua_ref)
    out = x + gate * _dot(act_ref[...], wdn_ref[...])
    if not final_norm:
        o_ref[...] = out
        return
    out = _rms(out) * fnw_ref[...]
    for k in range(D_MODEL // LANES):
        slab_ref[k] = out[:, LANES * k:LANES * (k + 1)]
    for s in range(SUBLANES):
        for k in range(D_MODEL // LANES):
            o_ref[s, :, LANES * k:LANES * (k + 1)] = slab_ref[k, pl.ds(s, tm // SUBLANES, stride=SUBLANES), :]


def _ffn(x, mod, w_up, conv_w, conv_b, w_down, final_w, *, n_seq, seq_len, shared_mod, mod_row, layer):
    tm = min(1024, seq_len)
    tps = seq_len // tm
    rows = n_seq * seq_len
    gps = seq_len // SUBLANES
    gpt = tm // SUBLANES
    mod_idx = (lambda i: (mod_row, 0, 1)) if shared_mod else (lambda i: (mod_row + i // tps, 0, 1))

    def prev_idx(i):
        jt = i % tps
        return ((i // tps) * gps + jnp.where(jt == 0, gps - 1, jt * gpt - 1), 0)

    def next_idx(i):
        jt = i % tps
        return ((i // tps) * gps + jnp.where(jt == tps - 1, 0, (jt + 1) * gpt), 0)

    final_norm = final_w is not None
    in_specs = [pl.BlockSpec((tm, D_MODEL), lambda i: (i, 0)),
                pl.BlockSpec((SUBLANES, D_MODEL), prev_idx),
                pl.BlockSpec((SUBLANES, D_MODEL), next_idx),
                pl.BlockSpec((None, 1, 3 * D_MODEL), mod_idx),
                _resident((D_MODEL, 2 * D_FF), layer),
                _resident((3, 2 * D_FF), layer),
                _resident((1, 2 * D_FF), layer),
                _resident((D_FF, D_MODEL), layer)]
    args = [x, x, x, mod, w_up, conv_w, conv_b, w_down]
    if final_norm:
        in_specs.append(_resident((1, D_MODEL)))
        args.append(final_w)
    kern = functools.partial(_ffn_kernel, tm=tm, tps=tps, final_norm=final_norm)
    scratch = ([pltpu.VMEM((tm + 2 * BF16_ROWS, D_MODEL), BF16), pltpu.VMEM((tm, D_FF), BF16)]
               + [pltpu.VMEM((2, tm + 2 * BF16_ROWS, FF_CHUNK), F32)] * 2)
    if final_norm:
        out_spec = pl.BlockSpec((None, SUBLANES, tm // SUBLANES, D_MODEL),
                                lambda i: (i // tps, 0, i % tps, 0))
        out_shape = jax.ShapeDtypeStruct((n_seq, SUBLANES, gps, D_MODEL), F32)
        scratch.append(pltpu.VMEM((D_MODEL // LANES, tm, LANES), F32))
    else:
        out_spec = pl.BlockSpec((tm, D_MODEL), lambda i: (i, 0))
        out_shape = jax.ShapeDtypeStruct((rows, D_MODEL), F32)
    return pl.pallas_call(
        kern,
        grid=(rows // tm,),
        in_specs=in_specs,
        out_specs=out_spec,
        out_shape=out_shape,
        scratch_shapes=scratch,
        compiler_params=_params("arbitrary"),
        name="conv_ffn_final" if final_norm else "conv_ffn",
    )(*args)


def _to_segments(x):
    b, t, c = x.shape
    return x.reshape(b, SUBLANES, t // SUBLANES, c).transpose(0, 2, 1, 3).reshape(b * t, c)


def _from_segments(y, b, t):
    return y.reshape(b, t // SUBLANES, SUBLANES, -1).transpose(0, 2, 1, 3).reshape(b, t, -1)


def _rope_tables(seq_len):
    r = np.arange(seq_len)
    t = (r % SUBLANES) * (seq_len // SUBLANES) + r // SUBLANES
    row = (t // GRID_W).astype(np.float64)
    col = (t % GRID_W).astype(np.float64)
    pairs = HEAD_DIM // 4
    inv = ROPE_THETA ** (-np.arange(pairs, dtype=np.float64) / pairs)
    ang = np.concatenate([row[:, None] * inv, col[:, None] * inv], axis=-1)
    cos, sin = np.cos(ang), np.sin(ang)
    return (jnp.asarray(np.concatenate([cos, cos, cos, cos], axis=-1), F32),
            jnp.asarray(np.concatenate([-sin, sin, -sin, sin], axis=-1), F32))


def kernel(x, c, ctx, c_ctx, w_ada, b_ada, w_in, q_norm_w, k_norm_w, lru_conv_w, lru_conv_b, lru_wa,
           lru_ba, lru_wx, lru_bx, lru_lambda, attn_out_norm_w, lru_out_norm_w, w_out, ffn_w_up,
           ffn_conv_w, ffn_conv_b, ffn_w_down, final_norm_w):
    batch, seq, _ = x.shape
    ctx_len = ctx.shape[1]
    depth = w_in.shape[0]

    mod_rows = -(-(batch + 1) // SUBLANES) * SUBLANES
    cvec = jnp.zeros((mod_rows, D_MODEL), F32).at[:batch].set(c).at[batch].set(c_ctx)
    mods = _modulation(cvec, w_ada, b_ada)

    cos, sin = _rope_tables(seq)
    ones = jnp.asarray(np.ones((ctx_len, LANES)), F32)
    bd = jnp.asarray(np.kron(np.eye(QK_W // HEAD_DIM), np.ones((HEAD_DIM, HEAD_DIM))), BF16)

    x_lat = _to_segments(x)
    x_ctx = _to_segments(ctx)
    mods = mods.reshape(depth * mod_rows, 1, 6 * D_MODEL)
    w_up_b = ffn_w_up.astype(BF16)
    w_dn_b = ffn_w_down.astype(BF16)
    conv_b = ffn_conv_b.reshape(depth, 1, 2 * D_FF)

    for l in range(depth):
        ctx_out = l < depth - 1
        last = l == depth - 1
        lat = dict(n_seq=batch, seq_len=seq, shared_mod=False, mod_row=l * mod_rows, layer=l)
        cx = dict(n_seq=batch, seq_len=ctx_len, shared_mod=True, mod_row=l * mod_rows + batch, layer=l)
        nw_qk = jnp.concatenate([jnp.tile(q_norm_w[l], D_ATTN // HEAD_DIM),
                                 jnp.tile(k_norm_w[l], KV_W // HEAD_DIM)]).reshape(1, QK_W)
        qt_l, k_l, vt_l, xr_l, g_l = _inproj(x_lat, mods, w_in, bd, nw_qk, cos, sin,
                                             use_rope=True, **lat)
        qt_c, k_c, vt_c, xr_c, g_c = _inproj(x_ctx, mods, w_in, bd, nw_qk, ones, ones,
                                             use_rope=False, **cx)

        a_l = _attention(qt_l, k_c, vt_c, k_l, vt_l, n_seq=batch, q_len=seq, ctx_len=ctx_len,
                         lat_len=seq)
        r = _lru(xr_c, g_c, xr_l, g_l, lru_conv_w, lru_conv_b.reshape(depth, 1, D_LRU), lru_wa, lru_wx,
                 lru_ba, lru_bx, lru_lambda, n_seq=batch, ctx_len=ctx_len, lat_len=seq,
                 ctx_out=ctx_out, layer=l)

        nw_out = jnp.concatenate([attn_out_norm_w[l], lru_out_norm_w[l]]).reshape(1, D_MODEL)
        x_lat = _outproj(a_l, r[0], x_lat, mods, nw_out, w_out, **lat)
        x_lat = _ffn(x_lat, mods, w_up_b, ffn_conv_w, conv_b, w_dn_b,
                     final_norm_w.reshape(1, D_MODEL) if last else None, **lat)
        if ctx_out:
            a_c = _attention(qt_c, k_c, vt_c, None, None, n_seq=batch, q_len=ctx_len,
                             ctx_len=ctx_len, lat_len=0)
            x_ctx = _outproj(a_c, r[1], x_ctx, mods, nw_out, w_out, **cx)
            x_ctx = _ffn(x_ctx, mods, w_up_b, ffn_conv_w, conv_b, w_dn_b, None, **cx)
    return x_lat.reshape(batch, seq, D_MODEL)
```

```python
import functools

import jax
import jax.numpy as jnp
import numpy as np
from jax import lax
from jax.experimental import pallas as pl
from jax.experimental.pallas import tpu as pltpu

F32 = jnp.float32
BF16 = jnp.bfloat16

D_MODEL = 1024
D_ATTN = 512
D_LRU = 512
KV_W = 128
HEAD_DIM = 64
N_HEADS = 8
N_KV = 2
KV_GROUP = 4
LOG2E = 1.4426950408889634
LN2 = 0.6931471805599453
GELU_C0 = 0.7978845608028654
GELU_C1 = GELU_C0 * 0.044715
QK_W = D_ATTN + KV_W
D_IN = D_ATTN + 2 * KV_W + 2 * D_LRU
D_FF = 2816
LRU_C = 8.0
GRID_W = 64
ROPE_THETA = 10000.0
EPS = 1e-6
SUBLANES = 8
LANES = 128
BF16_ROWS = 16
V_ROWS = HEAD_DIM + BF16_ROWS
K_COLS = LANES
MAX_SHIFTED_WEIGHT = 2.0 ** 60
FF_CHUNK = 256
SUB_ROWS = 256
LRU_SLAB = 256
SCAN_BLOCK = 8
VMEM_LIMIT = 58 * 1024 * 1024


def _dot(a, b):
    return jnp.dot(a, b, preferred_element_type=F32)


def _sigmoid(z):
    return 0.5 * jnp.tanh(0.5 * z) + 0.5


def _rms(x):
    return x * lax.rsqrt(jnp.mean(x * x, axis=-1, keepdims=True) + EPS)


def _shift_down(g):
    rows = lax.broadcasted_iota(jnp.int32, g.shape, 0)
    return jnp.where(rows == 0, 0.0, pltpu.roll(g, 1, 0))


def _shift_up(g):
    rows = lax.broadcasted_iota(jnp.int32, g.shape, 0)
    return jnp.where(rows == SUBLANES - 1, 0.0, pltpu.roll(g, SUBLANES - 1, 0))


def _n_sub(tile_rows):
    return max(2, tile_rows // SUB_ROWS)


def _params(*sem):
    return pltpu.CompilerParams(dimension_semantics=sem, vmem_limit_bytes=VMEM_LIMIT)


def _resident(shape, layer=None):
    nd = len(shape)
    if layer is None:
        return pl.BlockSpec(shape, lambda *_: (0,) * nd, pipeline_mode=pl.Buffered(1))
    return pl.BlockSpec((None,) + tuple(shape), lambda *_: (layer,) + (0,) * nd,
                        pipeline_mode=pl.Buffered(1))


def _mod_kernel(c_ref, w_ref, b_ref, o_ref):
    c = c_ref[...]
    s = (c * _sigmoid(c)).astype(BF16)
    o_ref[...] = _dot(s, w_ref[...].astype(BF16)) + b_ref[...]


def _modulation(cvec, w_ada, b_ada, tn=1536):
    depth, d, n = w_ada.shape
    rows = cvec.shape[0]
    return pl.pallas_call(
        _mod_kernel,
        grid=(depth, n // tn),
        in_specs=[pl.BlockSpec((rows, d), lambda l, j: (0, 0)),
                  pl.BlockSpec((None, d, tn), lambda l, j: (l, 0, j)),
                  pl.BlockSpec((None, 1, tn), lambda l, j: (l, 0, j))],
        out_specs=pl.BlockSpec((None, rows, tn), lambda l, j: (l, 0, j)),
        out_shape=jax.ShapeDtypeStruct((depth, rows, n), F32),
        compiler_params=_params("arbitrary", "arbitrary"),
        name="adaln_mod",
    )(cvec, w_ada, b_ada.reshape(depth, 1, n))


def _inproj_kernel(x_ref, mod_ref, w_ref, bd_ref, nw_ref, cos_ref, sin_ref,
                   qt_ref, k_ref, vt_ref, xl_ref, g_ref, *, use_rope):
    shift = mod_ref[:, 0:D_MODEL]
    scale = mod_ref[:, D_MODEL:2 * D_MODEL]
    n_sub = _n_sub(x_ref.shape[0])
    sub = x_ref.shape[0] // n_sub
    w = w_ref[...].astype(BF16)

    def project(r0):
        h = _rms(x_ref[pl.ds(r0, sub), :]) * (1.0 + scale) + shift
        y = _dot(h.astype(BF16), w)
        qk = y[:, 0:QK_W]
        return y, _dot((qk * qk).astype(BF16), bd_ref[...])

    def finish(r0, y_ss):
        y, ss = y_ss
        rows = pl.ds(r0, sub)
        qk = y[:, 0:QK_W]
        qk = qk * lax.rsqrt(ss * (1.0 / HEAD_DIM) + EPS) * nw_ref[...]
        if use_rope:
            lane = lax.broadcasted_iota(jnp.int32, (sub, LANES), 1)
            first_half = (lane % HEAD_DIM) < (HEAD_DIM // 2)
            cos = cos_ref[rows, :]
            sin = sin_ref[rows, :]
            cols = []
            for k in range(QK_W // LANES):
                xc = qk[:, LANES * k:LANES * (k + 1)]
                other = jnp.where(first_half, pltpu.roll(xc, LANES - HEAD_DIM // 2, 1),
                                  pltpu.roll(xc, HEAD_DIM // 2, 1))
                cols.append(xc * cos + other * sin)
            qk = jnp.concatenate(cols, axis=1)
        qt = (qk[:, 0:D_ATTN] * (LOG2E * HEAD_DIM ** -0.5)).T
        for h in range(N_HEADS):
            qt_ref[h, :, rows] = qt[HEAD_DIM * h:HEAD_DIM * (h + 1)].astype(BF16)
        kk = qk[:, D_ATTN:QK_W]
        lane = lax.broadcasted_iota(jnp.int32, kk.shape, 1)
        one_hot = jnp.where(lane == HEAD_DIM, 1.0, 0.0)
        for g in range(N_KV):
            kg = kk if g == 0 else pltpu.roll(kk, HEAD_DIM, 1)
            k_ref[g, rows, :] = jnp.where(lane < HEAD_DIM, kg, one_hot).astype(BF16)
        vt = y[:, QK_W:QK_W + KV_W].T.astype(BF16)
        ones = jnp.ones((V_ROWS - HEAD_DIM, sub), BF16)
        for g in range(N_KV):
            vt_ref[g, :, rows] = jnp.concatenate([vt[HEAD_DIM * g:HEAD_DIM * (g + 1)], ones], axis=0)
        xl_ref[rows, :] = y[:, QK_W + KV_W:QK_W + KV_W + D_LRU]
        gt = y[:, QK_W + KV_W + D_LRU:D_IN]
        hg = 0.5 * gt
        g_ref[rows, :] = hg + hg * jnp.tanh(gt * (GELU_C0 + GELU_C1 * (gt * gt)))

    y = project(0)
    for s in range(n_sub):
        y_next = project((s + 1) * sub) if s + 1 < n_sub else None
        finish(s * sub, y)
        y = y_next


def _inproj(x, mod, w_in, bd, nw, cos, sin, *, n_seq, seq_len, shared_mod, mod_row, layer,
            use_rope):
    tm = min(1024, seq_len)
    tps = seq_len // tm
    rows = n_seq * seq_len
    mod_idx = (lambda i: (mod_row, 0, 0)) if shared_mod else (lambda i: (mod_row + i // tps, 0, 0))
    kern = functools.partial(_inproj_kernel, use_rope=use_rope)
    return pl.pallas_call(
        kern,
        grid=(rows // tm,),
        in_specs=[pl.BlockSpec((tm, D_MODEL), lambda i: (i, 0)),
                  pl.BlockSpec((None, 1, 2 * D_MODEL), mod_idx),
                  _resident((D_MODEL, D_IN), layer),
                  _resident((QK_W, QK_W)),
                  _resident((1, QK_W)),
                  pl.BlockSpec((tm, LANES), lambda i: (i % tps, 0)),
                  pl.BlockSpec((tm, LANES), lambda i: (i % tps, 0))],
        out_specs=[pl.BlockSpec((None, N_HEADS, HEAD_DIM, tm), lambda i: (i // tps, 0, 0, i % tps)),
                   pl.BlockSpec((None, N_KV, tm, K_COLS), lambda i: (i // tps, 0, i % tps, 0)),
                   pl.BlockSpec((None, N_KV, V_ROWS, tm), lambda i: (i // tps, 0, 0, i % tps)),
                   pl.BlockSpec((tm, D_LRU), lambda i: (i, 0)),
                   pl.BlockSpec((tm, D_LRU), lambda i: (i, 0))],
        out_shape=[jax.ShapeDtypeStruct((n_seq, N_HEADS, HEAD_DIM, seq_len), BF16),
                   jax.ShapeDtypeStruct((n_seq, N_KV, seq_len, K_COLS), BF16),
                   jax.ShapeDtypeStruct((n_seq, N_KV, V_ROWS, seq_len), BF16),
                   jax.ShapeDtypeStruct((rows, D_LRU), F32),
                   jax.ShapeDtypeStruct((rows, D_LRU), F32)],
        compiler_params=_params("arbitrary"),
        name="in_proj_rope" if use_rope else "in_proj",
    )(x, mod, w_in, bd, nw, cos, sin)


def _attn_kernel(*refs, tq, tk, n_lat_chunks):
    if n_lat_chunks:
        qt_ref, kc_ref, vtc_ref, kl_ref, vtl_ref, o_ref = refs
    else:
        qt_ref, kc_ref, vtc_ref, o_ref = refs
    n = KV_GROUP * tq

    def queries(g):
        return jnp.concatenate([qt_ref[KV_GROUP * g + h] for h in range(KV_GROUP)], axis=1)

    def ctx_scores(g, qt):
        return _dot(kc_ref[g, :, 0:HEAD_DIM], qt)

    def finish(g, acc):
        ot = acc[0:HEAD_DIM] * (1.0 / acc[HEAD_DIM:HEAD_DIM + 1])
        for k in range(KV_GROUP):
            h = KV_GROUP * g + k
            o_ref[HEAD_DIM * h:HEAD_DIM * (h + 1), :] = ot[:, k * tq:(k + 1) * tq].astype(o_ref.dtype)

    def exact(g):
        qt = queries(g)
        s = ctx_scores(g, qt)
        m = jnp.max(s, axis=0, keepdims=True)
        acc = _dot(vtc_ref[g], jnp.exp2(s - m).astype(BF16))

        def body(c, carry):
            m, acc = carry
            off = pl.multiple_of(c * tk, tk)
            s = _dot(kl_ref[g, pl.ds(off, tk), 0:HEAD_DIM], qt)
            m_new = jnp.maximum(m, jnp.max(s, axis=0, keepdims=True))
            p = jnp.exp2(s - m_new).astype(BF16)
            return m_new, jnp.exp2(m - m_new) * acc + _dot(vtl_ref[g, :, pl.ds(off, tk)], p)

        if n_lat_chunks:
            m, acc = lax.fori_loop(0, n_lat_chunks, body, (m, acc))
        return acc

    if not n_lat_chunks:
        for g in range(N_KV):
            finish(g, exact(g))
        return

    row = lax.broadcasted_iota(jnp.int32, (BF16_ROWS, n), 0)
    pad = jnp.zeros((K_COLS - HEAD_DIM - BF16_ROWS, n), BF16)

    def start(g):
        qt = queries(g)
        s = ctx_scores(g, qt)
        running = jnp.max(s, axis=0, keepdims=True)
        shift = running.astype(BF16).astype(F32)
        acc = _dot(vtc_ref[g], jnp.exp2(s - shift).astype(BF16))
        return dict(qt=qt, running=running, shift=shift, shifts=[shift, shift], acc=acc,
                    largest=jnp.ones((1, n), F32))

    def issue(g, c, st):
        new_shift = st["shifts"].pop(0)
        shift_rows = jnp.where(row == 0, -new_shift, 0.0).astype(BF16)
        qa = jnp.concatenate([st["qt"], shift_rows, pad], axis=0)
        st["issued"] = new_shift, _dot(kl_ref[g, pl.ds(c * tk, tk), :], qa)

    def consume(g, c, st):
        new_shift, s = st.pop("issued")
        p = jnp.exp2(s)
        pv = _dot(vtl_ref[g, :, pl.ds(c * tk, tk)], p.astype(BF16))
        st["acc"] = st["acc"] * jnp.exp2(st["shift"] - new_shift) + pv
        st["shift"] = new_shift
        pmax = jnp.max(p, axis=0, keepdims=True)
        st["largest"] = jnp.maximum(st["largest"], pmax)
        st["running"] = jnp.maximum(st["running"], new_shift + jnp.log2(pmax))
        st["shifts"].append(st["running"].astype(BF16).astype(F32))

    states = [start(g) for g in range(N_KV)]
    items = [(g, c) for c in range(n_lat_chunks) for g in range(N_KV)]
    issue(*items[0], states[items[0][0]])
    for k, (g, c) in enumerate(items):
        if k + 1 < len(items):
            gn, cn = items[k + 1]
            issue(gn, cn, states[gn])
        consume(g, c, states[g])
    for g in range(N_KV):
        finish(g, states[g]["acc"])
    overflow = jnp.max(jnp.maximum(states[0]["largest"], states[1]["largest"])) > MAX_SHIFTED_WEIGHT

    @pl.when(overflow)
    def _():
        for g in range(N_KV):
            finish(g, exact(g))


def _attention(qt, k_ctx, vt_ctx, k_lat, vt_lat, *, n_seq, q_len, ctx_len, lat_len):
    tq = min(256, q_len)
    tk = min(512, lat_len // 2) if lat_len else 0
    n_lat_chunks = lat_len // tk if lat_len else 0
    nq = q_len // tq
    kern = functools.partial(_attn_kernel, tq=tq, tk=tk, n_lat_chunks=n_lat_chunks)
    in_specs = [pl.BlockSpec((None, N_HEADS, HEAD_DIM, tq), lambda b, i: (b, 0, 0, i)),
                pl.BlockSpec((None, N_KV, ctx_len, K_COLS), lambda b, i: (b, 0, 0, 0)),
                pl.BlockSpec((None, N_KV, V_ROWS, ctx_len), lambda b, i: (b, 0, 0, 0))]
    args = [qt, k_ctx, vt_ctx]
    scratch = []
    if lat_len:
        in_specs += [pl.BlockSpec((None, N_KV, lat_len, K_COLS), lambda b, i: (b, 0, 0, 0)),
                     pl.BlockSpec((None, N_KV, V_ROWS, lat_len), lambda b, i: (b, 0, 0, 0))]
        args += [k_lat, vt_lat]
    return pl.pallas_call(
        kern,
        grid=(n_seq, nq),
        in_specs=in_specs,
        out_specs=pl.BlockSpec((None, D_ATTN, tq), lambda b, i: (b, 0, i)),
        out_shape=jax.ShapeDtypeStruct((n_seq, D_ATTN, q_len), BF16),
        scratch_shapes=scratch,
        compiler_params=_params("arbitrary", "arbitrary"),
        name="attn_latent" if lat_len else "attn_context",
    )(*args)


def _lru_sequence(x_ref, g_ref, o_ref, h0, consts, scr, *, n_groups):
    cw_ref, cb_ref, wg_ref, bg_ref, hnsp = consts
    xe_ref, af_ref, bf_ref, ar_ref, br_ref = scr
    rows = SUBLANES * n_groups
    width = x_ref.shape[1]
    chunk = min(256, rows)
    n_chunks = rows // chunk

    xe_ref[pl.ds(2 * SUBLANES, rows), :] = x_ref[...]
    xe_ref[pl.ds(0, SUBLANES), :] = _shift_down(x_ref[pl.ds(rows - 2 * SUBLANES, SUBLANES), :])
    xe_ref[pl.ds(SUBLANES, SUBLANES), :] = _shift_down(x_ref[pl.ds(rows - SUBLANES, SUBLANES), :])
    xe_ref[pl.ds(rows + 2 * SUBLANES, SUBLANES), :] = _shift_up(x_ref[pl.ds(0, SUBLANES), :])

    def gates(c, _):
        r0 = pl.multiple_of(c * chunk, chunk)
        xc = cb_ref[...]
        for k in range(4):
            xc = xc + cw_ref[k:k + 1, :] * xe_ref[pl.ds(r0 + SUBLANES * k, chunk), :]
        t = jnp.tanh(_dot(xc.astype(BF16), wg_ref[...]) + bg_ref[...])
        hx = 0.5 * xc
        for d, (a_ref, b_ref) in enumerate(((af_ref, bf_ref), (ar_ref, br_ref))):
            t_r = t[:, (2 * d) * width:(2 * d + 1) * width]
            t_i = t[:, (2 * d + 1) * width:(2 * d + 2) * width]
            log2_a = hnsp[d] * t_r + hnsp[d]
            a = jnp.exp2(log2_a)
            y = jnp.tanh(LN2 * log2_a) * (-1.0 - a * a)
            b = jnp.where(y > 0.0, y * lax.rsqrt(y), 0.0) * (hx * t_i + hx)
            a_ref[pl.ds(r0, chunk), :] = a
            b_ref[pl.ds(r0, chunk), :] = b
        return 0

    lax.fori_loop(0, n_chunks, gates, 0)

    blk = min(SCAN_BLOCK, n_groups)
    blk_rows = blk * SUBLANES
    n_blocks = n_groups // blk

    def scan_block(a_ref, b_ref, r0, h, p, order):
        a_blk = a_ref[pl.ds(r0, blk_rows), :]
        b_blk = b_ref[pl.ds(r0, blk_rows), :]
        hs, ps = [None] * blk, [None] * blk
        for k in order:
            a = a_blk[k * SUBLANES:(k + 1) * SUBLANES]
            h = a * h + b_blk[k * SUBLANES:(k + 1) * SUBLANES]
            p = a * p
            hs[k], ps[k] = h, p
        b_ref[pl.ds(r0, blk_rows), :] = jnp.concatenate(hs, axis=0)
        a_ref[pl.ds(r0, blk_rows), :] = jnp.concatenate(ps, axis=0)
        return h, p

    def scan(i, carry):
        hf, pf, hr, pr = carry
        rf = pl.multiple_of(i * blk_rows, blk_rows)
        rr = pl.multiple_of((n_blocks - 1 - i) * blk_rows, blk_rows)
        hf, pf = scan_block(af_ref, bf_ref, rf, hf, pf, range(blk))
        hr, pr = scan_block(ar_ref, br_ref, rr, hr, pr, range(blk - 1, -1, -1))
        return hf, pf, hr, pr

    zero = jnp.zeros((SUBLANES, width), F32)
    one = jnp.ones((SUBLANES, width), F32)
    hf, pf, hr, pr = lax.fori_loop(0, n_blocks, scan, (zero, one, zero, one))

    e = h0[0]
    init_f = []
    for s in range(SUBLANES):
        init_f.append(e)
        e = hf[s:s + 1, :] + pf[s:s + 1, :] * e
    fin_f = e
    e = h0[1]
    init_r = [None] * SUBLANES
    for s in range(SUBLANES - 1, -1, -1):
        init_r[s] = e
        e = hr[s:s + 1, :] + pr[s:s + 1, :] * e
    fin_r = e

    if o_ref is not None:
        init_f = jnp.concatenate(init_f, axis=0)[None]
        init_r = jnp.concatenate(init_r, axis=0)[None]

        def combine(c, _):
            r0 = pl.multiple_of(c * chunk, chunk)
            sl = pl.ds(r0, chunk)
            shp = (chunk // SUBLANES, SUBLANES, width)
            h = (bf_ref[sl, :].reshape(shp) + af_ref[sl, :].reshape(shp) * init_f
                 + br_ref[sl, :].reshape(shp) + ar_ref[sl, :].reshape(shp) * init_r)
            o_ref[sl, :] = (h.reshape(chunk, width) * g_ref[sl, :]).astype(o_ref.dtype)
            return 0

        lax.fori_loop(0, n_chunks, combine, 0)
    return fin_f, fin_r


def _lru_kernel(*refs, ctx_groups, lat_groups, ctx_out):
    xc_ref, gc_ref, xl_ref, gl_ref, cw_ref, cb_ref, wa_ref, wx_ref, ba_ref, bx_ref, lam_ref = refs[:11]
    n_out = 2 if ctx_out else 1
    outs = refs[11:11 + n_out]
    scr = refs[11 + n_out:-2]
    wg_ref, bg_ref = refs[-2:]
    width = xc_ref.shape[1]
    blk = wa_ref.shape[-1]
    wg_ref[...] = jnp.zeros(wg_ref.shape, BF16)
    for t, (w_ref, b_ref, d) in enumerate(((wa_ref, ba_ref, 0), (wx_ref, bx_ref, 0),
                                           (wa_ref, ba_ref, 1), (wx_ref, bx_ref, 1))):
        for p in range(width // blk):
            wg_ref[blk * p:blk * (p + 1), t * width + blk * p:t * width + blk * (p + 1)] = (
                (0.5 * w_ref[d, p]).astype(BF16))
        bg_ref[:, t * width:(t + 1) * width] = 0.5 * b_ref[d:d + 1, :]
    z = -lam_ref[...]
    softplus = jnp.maximum(z, 0.0) + jnp.log1p(jnp.exp(-jnp.abs(z)))
    hnsp = [(-0.5 * LRU_C * LOG2E) * softplus[d:d + 1, :] for d in range(2)]
    consts = (cw_ref, cb_ref, wg_ref, bg_ref, hnsp)
    zero = jnp.zeros((1, xc_ref.shape[1]), F32)
    fin = _lru_sequence(xc_ref, gc_ref, outs[1] if ctx_out else None, (zero, zero), consts, scr,
                        n_groups=ctx_groups)
    _lru_sequence(xl_ref, gl_ref, outs[0], fin, consts, scr, n_groups=lat_groups)


def _lru(x_ctx, g_ctx, x_lat, g_lat, conv_w, conv_b, wa, wx, ba, bx, lam, *, n_seq, ctx_len, lat_len,
         ctx_out, layer):
    w = LRU_SLAB
    n_slab = D_LRU // w
    blk = wa.shape[-1]
    per = w // blk

    def vec(rows):
        return pl.BlockSpec((None, rows, w), lambda b, c: (layer, 0, c))

    gate_w = pl.BlockSpec((None, 2, per, blk, blk), lambda b, c: (layer, 0, c, 0, 0))
    kern = functools.partial(_lru_kernel, ctx_groups=ctx_len // SUBLANES,
                             lat_groups=lat_len // SUBLANES, ctx_out=ctx_out)
    ctx_spec = pl.BlockSpec((ctx_len, w), lambda b, c: (b, c))
    lat_spec = pl.BlockSpec((lat_len, w), lambda b, c: (b, c))
    out_specs = [lat_spec]
    out_shape = [jax.ShapeDtypeStruct((n_seq * lat_len, D_LRU), BF16)]
    if ctx_out:
        out_specs.append(ctx_spec)
        out_shape.append(jax.ShapeDtypeStruct((n_seq * ctx_len, D_LRU), BF16))
    max_rows = max(ctx_len, lat_len)
    return pl.pallas_call(
        kern,
        grid=(n_seq, n_slab),
        in_specs=[ctx_spec, ctx_spec, lat_spec, lat_spec,
                  vec(conv_w.shape[1]), vec(1), gate_w, gate_w, vec(2), vec(2), vec(2)],
        out_specs=out_specs,
        out_shape=out_shape,
        scratch_shapes=[pltpu.VMEM((max_rows + 3 * SUBLANES, w), F32)]
                       + [pltpu.VMEM((max_rows, w), F32)] * 4
                       + [pltpu.VMEM((w, 4 * w), BF16), pltpu.VMEM((1, 4 * w), F32)],
        compiler_params=_params("arbitrary", "arbitrary"),
        name="rglru_bidir",
    )(x_ctx, g_ctx, x_lat, g_lat, conv_w, conv_b, wa, wx, ba, bx, lam)


def _outproj_kernel(at_ref, r_ref, x_ref, gate_ref, nw_ref, w_ref, o_ref):
    n_sub = _n_sub(x_ref.shape[0])
    sub = x_ref.shape[0] // n_sub
    w = w_ref[...].astype(BF16)

    def project(r0):
        rows = pl.ds(r0, sub)
        a = at_ref[:, rows].astype(F32).T
        cat = jnp.concatenate([_rms(a), _rms(r_ref[rows, :].astype(F32))], axis=1) * nw_ref[...]
        return _dot(cat.astype(BF16), w)

    y = project(0)
    for s in range(n_sub):
        y_next = project((s + 1) * sub) if s + 1 < n_sub else None
        rows = pl.ds(s * sub, sub)
        o_ref[rows, :] = x_ref[rows, :] + gate_ref[...] * y
        y = y_next


def _outproj(a, r, x, mod, nw, w_out, *, n_seq, seq_len, shared_mod, mod_row, layer):
    tm = min(1024, seq_len)
    tps = seq_len // tm
    rows = n_seq * seq_len
    gate_idx = (lambda i: (mod_row, 0, 2)) if shared_mod else (lambda i: (mod_row + i // tps, 0, 2))
    return pl.pallas_call(
        _outproj_kernel,
        grid=(rows // tm,),
        in_specs=[pl.BlockSpec((None, D_ATTN, tm), lambda i: (i // tps, 0, i % tps)),
                  pl.BlockSpec((tm, D_LRU), lambda i: (i, 0)),
                  pl.BlockSpec((tm, D_MODEL), lambda i: (i, 0)),
                  pl.BlockSpec((None, 1, D_MODEL), gate_idx),
                  _resident((1, D_MODEL)),
                  _resident((D_MODEL, D_MODEL), layer)],
        out_specs=pl.BlockSpec((tm, D_MODEL), lambda i: (i, 0)),
        out_shape=jax.ShapeDtypeStruct((rows, D_MODEL), F32),
        compiler_params=_params("arbitrary"),
        name="out_proj",
    )(a, r, x, mod, nw, w_out)


def _ffn_kernel(*refs, tm, tps, final_norm):
    x_ref, xp_ref, xn_ref, mod_ref, wup_ref, cw_ref, cb_ref, wdn_ref = refs[:8]
    if final_norm:
        fnw_ref, o_ref, hext_ref, act_ref, ua_ref, ub_ref, slab_ref = refs[8:]
    else:
        o_ref, hext_ref, act_ref, ua_ref, ub_ref = refs[8:]
    assert (D_FF // FF_CHUNK) % 2 == 1
    jt = pl.program_id(0) % tps
    shift = mod_ref[:, 0:D_MODEL]
    scale = mod_ref[:, D_MODEL:2 * D_MODEL]
    gate = mod_ref[:, 2 * D_MODEL:3 * D_MODEL]

    def norm_mod(x):
        return _rms(x) * (1.0 + scale) + shift

    x = x_ref[...]
    hext_ref[pl.ds(BF16_ROWS, tm), :] = norm_mod(x).astype(BF16)
    hp = norm_mod(xp_ref[...])
    hp = jnp.where(jt == 0, _shift_down(hp), hp)
    hn = norm_mod(xn_ref[...])
    hn = jnp.where(jt == tps - 1, _shift_up(hn), hn)
    zeros = jnp.zeros_like(hp)
    hext_ref[pl.ds(0, BF16_ROWS), :] = jnp.concatenate([zeros, hp], axis=0).astype(BF16)
    hext_ref[pl.ds(BF16_ROWS + tm, BF16_ROWS), :] = jnp.concatenate([hn, zeros], axis=0).astype(BF16)

    def up(c, u_ref):
        for k in range(2):
            off = pl.multiple_of(k * D_FF + c * FF_CHUNK, FF_CHUNK)
            u_ref[k] = _dot(hext_ref[...], wup_ref[:, pl.ds(off, FF_CHUNK)])

    def conv(u_ref, k, off):
        sl = pl.ds(off, FF_CHUNK)
        return (cw_ref[0:1, sl] * u_ref[k, pl.ds(SUBLANES, tm), :]
                + cw_ref[1:2, sl] * u_ref[k, pl.ds(2 * SUBLANES, tm), :]
                + cw_ref[2:3, sl] * u_ref[k, pl.ds(3 * SUBLANES, tm), :]
                + cb_ref[:, sl])

    def activate(c, u_ref):
        og = pl.multiple_of(c * FF_CHUNK, FF_CHUNK)
        ov = pl.multiple_of(D_FF + c * FF_CHUNK, FF_CHUNK)
        yg = conv(u_ref, 0, og)
        yv = conv(u_ref, 1, ov)
        hg = 0.5 * yg
        act_ref[:, pl.ds(og, FF_CHUNK)] = ((hg + hg * jnp.tanh(hg)) * yv).astype(BF16)

    n_chunks = D_FF // FF_CHUNK
    up(0, ua_ref)

    def pair(j, _):
        up(2 * j + 1, ub_ref)
        activate(2 * j, ua_ref)
        up(2 * j + 2, ua_ref)
        activate(2 * j + 1, ub_ref)
        return 0

    for j in range((n_chunks - 1) // 2):
        pair(j, 0)
    activate(n_chunks - 1, ua_ref)
    out = x + gate * _dot(act_ref[...], wdn_ref[...])
    if not final_norm:
        o_ref[...] = out
        return
    out = _rms(out) * fnw_ref[...]
    for k in range(D_MODEL // LANES):
        slab_ref[k] = out[:, LANES * k:LANES * (k + 1)]
    for s in range(SUBLANES):
        for k in range(D_MODEL // LANES):
            o_ref[s, :, LANES * k:LANES * (k + 1)] = slab_ref[k, pl.ds(s, tm // SUBLANES, stride=SUBLANES), :]


def _ffn(x, mod, w_up, conv_w, conv_b, w_down, final_w, *, n_seq, seq_len, shared_mod, mod_row, layer):
    tm = min(1024, seq_len)
    tps = seq_len // tm
    rows = n_seq * seq_len
    gps = seq_len // SUBLANES
    gpt = tm // SUBLANES
    mod_idx = (lambda i: (mod_row, 0, 1)) if shared_mod else (lambda i: (mod_row + i // tps, 0, 1))

    def prev_idx(i):
        jt = i % tps
        return ((i // tps) * gps + jnp.where(jt == 0, gps - 1, jt * gpt - 1), 0)

    def next_idx(i):
        jt = i % tps
        return ((i // tps) * gps + jnp.where(jt == tps - 1, 0, (jt + 1) * gpt), 0)

    final_norm = final_w is not None
    in_specs = [pl.BlockSpec((tm, D_MODEL), lambda i: (i, 0)),
                pl.BlockSpec((SUBLANES, D_MODEL), prev_idx),
                pl.BlockSpec((SUBLANES, D_MODEL), next_idx),
                pl.BlockSpec((None, 1, 3 * D_MODEL), mod_idx),
                _resident((D_MODEL, 2 * D_FF), layer),
                _resident((3, 2 * D_FF), layer),
                _resident((1, 2 * D_FF), layer),
                _resident((D_FF, D_MODEL), layer)]
    args = [x, x, x, mod, w_up, conv_w, conv_b, w_down]
    if final_norm:
        in_specs.append(_resident((1, D_MODEL)))
        args.append(final_w)
    kern = functools.partial(_ffn_kernel, tm=tm, tps=tps, final_norm=final_norm)
    scratch = ([pltpu.VMEM((tm + 2 * BF16_ROWS, D_MODEL), BF16), pltpu.VMEM((tm, D_FF), BF16)]
               + [pltpu.VMEM((2, tm + 2 * BF16_ROWS, FF_CHUNK), F32)] * 2)
    if final_norm:
        out_spec = pl.BlockSpec((None, SUBLANES, tm // SUBLANES, D_MODEL),
                                lambda i: (i // tps, 0, i % tps, 0))
        out_shape = jax.ShapeDtypeStruct((n_seq, SUBLANES, gps, D_MODEL), F32)
        scratch.append(pltpu.VMEM((D_MODEL // LANES, tm, LANES), F32))
    else:
        out_spec = pl.BlockSpec((tm, D_MODEL), lambda i: (i, 0))
        out_shape = jax.ShapeDtypeStruct((rows, D_MODEL), F32)
    return pl.pallas_call(
        kern,
        grid=(rows // tm,),
        in_specs=in_specs,
        out_specs=out_spec,
        out_shape=out_shape,
        scratch_shapes=scratch,
        compiler_params=_params("arbitrary"),
        name="conv_ffn_final" if final_norm else "conv_ffn",
    )(*args)


def _to_segments(x):
    b, t, c = x.shape
    return x.reshape(b, SUBLANES, t // SUBLANES, c).transpose(0, 2, 1, 3).reshape(b * t, c)


def _from_segments(y, b, t):
    return y.reshape(b, t // SUBLANES, SUBLANES, -1).transpose(0, 2, 1, 3).reshape(b, t, -1)


def _rope_tables(seq_len):
    r = np.arange(seq_len)
    t = (r % SUBLANES) * (seq_len // SUBLANES) + r // SUBLANES
    row = (t // GRID_W).astype(np.float64)
    col = (t % GRID_W).astype(np.float64)
    pairs = HEAD_DIM // 4
    inv = ROPE_THETA ** (-np.arange(pairs, dtype=np.float64) / pairs)
    ang = np.concatenate([row[:, None] * inv, col[:, None] * inv], axis=-1)
    cos, sin = np.cos(ang), np.sin(ang)
    return (jnp.asarray(np.concatenate([cos, cos, cos, cos], axis=-1), F32),
            jnp.asarray(np.concatenate([-sin, sin, -sin, sin], axis=-1), F32))


def kernel(x, c, ctx, c_ctx, w_ada, b_ada, w_in, q_norm_w, k_norm_w, lru_conv_w, lru_conv_b, lru_wa,
           lru_ba, lru_wx, lru_bx, lru_lambda, attn_out_norm_w, lru_out_norm_w, w_out, ffn_w_up,
           ffn_conv_w, ffn_conv_b, ffn_w_down, final_norm_w):
    batch, seq, _ = x.shape
    ctx_len = ctx.shape[1]
    depth = w_in.shape[0]

    mod_rows = -(-(batch + 1) // SUBLANES) * SUBLANES
    cvec = jnp.zeros((mod_rows, D_MODEL), F32).at[:batch].set(c).at[batch].set(c_ctx)
    mods = _modulation(cvec, w_ada, b_ada)

    cos, sin = _rope_tables(seq)
    ones = jnp.asarray(np.ones((ctx_len, LANES)), F32)
    bd = jnp.asarray(np.kron(np.eye(QK_W // HEAD_DIM), np.ones((HEAD_DIM, HEAD_DIM))), BF16)

    x_lat = _to_segments(x)
    x_ctx = _to_segments(ctx)
    mods = mods.reshape(depth * mod_rows, 1, 6 * D_MODEL)
    w_up_b = ffn_w_up.astype(BF16)
    w_dn_b = ffn_w_down.astype(BF16)
    conv_b = ffn_conv_b.reshape(depth, 1, 2 * D_FF)

    for l in range(depth):
        ctx_out = l < depth - 1
        last = l == depth - 1
        lat = dict(n_seq=batch, seq_len=seq, shared_mod=False, mod_row=l * mod_rows, layer=l)
        cx = dict(n_seq=batch, seq_len=ctx_len, shared_mod=True, mod_row=l * mod_rows + batch, layer=l)
        nw_qk = jnp.concatenate([jnp.tile(q_norm_w[l], D_ATTN // HEAD_DIM),
                                 jnp.tile(k_norm_w[l], KV_W // HEAD_DIM)]).reshape(1, QK_W)
        qt_l, k_l, vt_l, xr_l, g_l = _inproj(x_lat, mods, w_in, bd, nw_qk, cos, sin,
                                             use_rope=True, **lat)
        qt_c, k_c, vt_c, xr_c, g_c = _inproj(x_ctx, mods, w_in, bd, nw_qk, ones, ones,
                                             use_rope=False, **cx)

        a_l = _attention(qt_l, k_c, vt_c, k_l, vt_l, n_seq=batch, q_len=seq, ctx_len=ctx_len,
                         lat_len=seq)
        r = _lru(xr_c, g_c, xr_l, g_l, lru_conv_w, lru_conv_b.reshape(depth, 1, D_LRU), lru_wa, lru_wx,
                 lru_ba, lru_bx, lru_lambda, n_seq=batch, ctx_len=ctx_len, lat_len=seq,
                 ctx_out=ctx_out, layer=l)

        nw_out = jnp.concatenate([attn_out_norm_w[l], lru_out_norm_w[l]]).reshape(1, D_MODEL)
        x_lat = _outproj(a_l, r[0], x_lat, mods, nw_out, w_out, **lat)
        x_lat = _ffn(x_lat, mods, w_up_b, ffn_conv_w, conv_b, w_dn_b,
                     final_norm_w.reshape(1, D_MODEL) if last else None, **lat)
        if ctx_out:
            a_c = _attention(qt_c, k_c, vt_c, None, None, n_seq=batch, q_len=ctx_len,
                             ctx_len=ctx_len, lat_len=0)
            x_ctx = _outproj(a_c, r[1], x_ctx, mods, nw_out, w_out, **cx)
            x_ctx = _ffn(x_ctx, mods, w_up_b, ffn_conv_w, conv_b, w_dn_b, None, **cx)
    return x_lat.reshape(batch, seq, D_MODEL)
```

```python
import functools

import jax
import jax.numpy as jnp
import numpy as np
from jax import lax
from jax.experimental import pallas as pl
from jax.experimental.pallas import tpu as pltpu

F32 = jnp.float32
BF16 = jnp.bfloat16

D_MODEL = 1024
D_ATTN = 512
D_LRU = 512
KV_W = 128
HEAD_DIM = 64
N_HEADS = 8
N_KV = 2
KV_GROUP = 4
LOG2E = 1.4426950408889634
LN2 = 0.6931471805599453
GELU_C0 = 0.7978845608028654
GELU_C1 = GELU_C0 * 0.044715
QK_W = D_ATTN + KV_W
D_IN = D_ATTN + 2 * KV_W + 2 * D_LRU
D_FF = 2816
LRU_C = 8.0
GRID_W = 64
ROPE_THETA = 10000.0
EPS = 1e-6
SUBLANES = 8
LANES = 128
BF16_ROWS = 16
V_ROWS = HEAD_DIM + BF16_ROWS
K_COLS = LANES
MAX_SHIFTED_WEIGHT = 2.0 ** 60
FF_CHUNK = 256
SUB_ROWS = 256
LRU_SLAB = 256
LRU_CHUNK = 1024
SCAN_BLOCK = 8
VMEM_LIMIT = 58 * 1024 * 1024


def _dot(a, b):
    return jnp.dot(a, b, preferred_element_type=F32)


def _sigmoid(z):
    return 0.5 * jnp.tanh(0.5 * z) + 0.5


def _rms(x):
    return x * lax.rsqrt(jnp.mean(x * x, axis=-1, keepdims=True) + EPS)


def _shift_down(g):
    rows = lax.broadcasted_iota(jnp.int32, g.shape, 0)
    return jnp.where(rows == 0, 0.0, pltpu.roll(g, 1, 0))


def _shift_up(g):
    rows = lax.broadcasted_iota(jnp.int32, g.shape, 0)
    return jnp.where(rows == SUBLANES - 1, 0.0, pltpu.roll(g, SUBLANES - 1, 0))


def _n_sub(tile_rows):
    return max(2, tile_rows // SUB_ROWS)


def _params(*sem):
    return pltpu.CompilerParams(dimension_semantics=sem, vmem_limit_bytes=VMEM_LIMIT)


def _resident(shape, layer=None):
    nd = len(shape)
    if layer is None:
        return pl.BlockSpec(shape, lambda *_: (0,) * nd, pipeline_mode=pl.Buffered(1))
    return pl.BlockSpec((None,) + tuple(shape), lambda *_: (layer,) + (0,) * nd,
                        pipeline_mode=pl.Buffered(1))


def _mod_kernel(c_ref, w_ref, b_ref, o_ref):
    c = c_ref[...]
    s = (c * _sigmoid(c)).astype(BF16)
    o_ref[...] = _dot(s, w_ref[...].astype(BF16)) + b_ref[...]


def _modulation(cvec, w_ada, b_ada, tn=1536):
    depth, d, n = w_ada.shape
    rows = cvec.shape[0]
    return pl.pallas_call(
        _mod_kernel,
        grid=(depth, n // tn),
        in_specs=[pl.BlockSpec((rows, d), lambda l, j: (0, 0)),
                  pl.BlockSpec((None, d, tn), lambda l, j: (l, 0, j)),
                  pl.BlockSpec((None, 1, tn), lambda l, j: (l, 0, j))],
        out_specs=pl.BlockSpec((None, rows, tn), lambda l, j: (l, 0, j)),
        out_shape=jax.ShapeDtypeStruct((depth, rows, n), F32),
        compiler_params=_params("arbitrary", "arbitrary"),
        name="adaln_mod",
    )(cvec, w_ada, b_ada.reshape(depth, 1, n))


def _inproj_kernel(x_ref, mod_ref, w_ref, bd_ref, nw_ref, cos_ref, sin_ref,
                   qt_ref, k_ref, vt_ref, xl_ref, g_ref, *, use_rope):
    shift = mod_ref[:, 0:D_MODEL]
    scale = mod_ref[:, D_MODEL:2 * D_MODEL]
    n_sub = _n_sub(x_ref.shape[0])
    sub = x_ref.shape[0] // n_sub
    w = w_ref[...].astype(BF16)

    def project(r0):
        h = _rms(x_ref[pl.ds(r0, sub), :]) * (1.0 + scale) + shift
        y = _dot(h.astype(BF16), w)
        qk = y[:, 0:QK_W]
        return y, _dot((qk * qk).astype(BF16), bd_ref[...])

    def finish(r0, y_ss):
        y, ss = y_ss
        rows = pl.ds(r0, sub)
        qk = y[:, 0:QK_W]
        qk = qk * lax.rsqrt(ss * (1.0 / HEAD_DIM) + EPS) * nw_ref[...]
        if use_rope:
            lane = lax.broadcasted_iota(jnp.int32, (sub, LANES), 1)
            first_half = (lane % HEAD_DIM) < (HEAD_DIM // 2)
            cos = cos_ref[rows, :]
            sin = sin_ref[rows, :]
            cols = []
            for k in range(QK_W // LANES):
                xc = qk[:, LANES * k:LANES * (k + 1)]
                other = jnp.where(first_half, pltpu.roll(xc, LANES - HEAD_DIM // 2, 1),
                                  pltpu.roll(xc, HEAD_DIM // 2, 1))
                cols.append(xc * cos + other * sin)
            qk = jnp.concatenate(cols, axis=1)
        qt = (qk[:, 0:D_ATTN] * (LOG2E * HEAD_DIM ** -0.5)).T
        for h in range(N_HEADS):
            qt_ref[h, :, rows] = qt[HEAD_DIM * h:HEAD_DIM * (h + 1)].astype(BF16)
        kk = qk[:, D_ATTN:QK_W]
        lane = lax.broadcasted_iota(jnp.int32, kk.shape, 1)
        one_hot = jnp.where(lane == HEAD_DIM, 1.0, 0.0)
        for g in range(N_KV):
            kg = kk if g == 0 else pltpu.roll(kk, HEAD_DIM, 1)
            k_ref[g, rows, :] = jnp.where(lane < HEAD_DIM, kg, one_hot).astype(BF16)
        vt = y[:, QK_W:QK_W + KV_W].T.astype(BF16)
        ones = jnp.ones((V_ROWS - HEAD_DIM, sub), BF16)
        for g in range(N_KV):
            vt_ref[g, :, rows] = jnp.concatenate([vt[HEAD_DIM * g:HEAD_DIM * (g + 1)], ones], axis=0)
        xl_ref[rows, :] = y[:, QK_W + KV_W:QK_W + KV_W + D_LRU]
        gt = y[:, QK_W + KV_W + D_LRU:D_IN]
        hg = 0.5 * gt
        g_ref[rows, :] = hg + hg * jnp.tanh(gt * (GELU_C0 + GELU_C1 * (gt * gt)))

    y = project(0)
    for s in range(n_sub):
        y_next = project((s + 1) * sub) if s + 1 < n_sub else None
        finish(s * sub, y)
        y = y_next


def _inproj(x, mod, w_in, bd, nw, cos, sin, *, n_seq, seq_len, shared_mod, mod_row, layer,
            use_rope):
    tm = min(1024, seq_len)
    tps = seq_len // tm
    rows = n_seq * seq_len
    mod_idx = (lambda i: (mod_row, 0, 0)) if shared_mod else (lambda i: (mod_row + i // tps, 0, 0))
    kern = functools.partial(_inproj_kernel, use_rope=use_rope)
    return pl.pallas_call(
        kern,
        grid=(rows // tm,),
        in_specs=[pl.BlockSpec((tm, D_MODEL), lambda i: (i, 0)),
                  pl.BlockSpec((None, 1, 2 * D_MODEL), mod_idx),
                  _resident((D_MODEL, D_IN), layer),
                  _resident((QK_W, QK_W)),
                  _resident((1, QK_W)),
                  pl.BlockSpec((tm, LANES), lambda i: (i % tps, 0)),
                  pl.BlockSpec((tm, LANES), lambda i: (i % tps, 0))],
        out_specs=[pl.BlockSpec((None, N_HEADS, HEAD_DIM, tm), lambda i: (i // tps, 0, 0, i % tps)),
                   pl.BlockSpec((None, N_KV, tm, K_COLS), lambda i: (i // tps, 0, i % tps, 0)),
                   pl.BlockSpec((None, N_KV, V_ROWS, tm), lambda i: (i // tps, 0, 0, i % tps)),
                   pl.BlockSpec((tm, D_LRU), lambda i: (i, 0)),
                   pl.BlockSpec((tm, D_LRU), lambda i: (i, 0))],
        out_shape=[jax.ShapeDtypeStruct((n_seq, N_HEADS, HEAD_DIM, seq_len), BF16),
                   jax.ShapeDtypeStruct((n_seq, N_KV, seq_len, K_COLS), BF16),
                   jax.ShapeDtypeStruct((n_seq, N_KV, V_ROWS, seq_len), BF16),
                   jax.ShapeDtypeStruct((rows, D_LRU), F32),
                   jax.ShapeDtypeStruct((rows, D_LRU), F32)],
        compiler_params=_params("arbitrary"),
        name="in_proj_rope" if use_rope else "in_proj",
    )(x, mod, w_in, bd, nw, cos, sin)


def _attn_kernel(*refs, tq, tk, n_lat_chunks):
    if n_lat_chunks:
        qt_ref, kc_ref, vtc_ref, kl_ref, vtl_ref, o_ref = refs
    else:
        qt_ref, kc_ref, vtc_ref, o_ref = refs
    n = KV_GROUP * tq

    def queries(g):
        return jnp.concatenate([qt_ref[KV_GROUP * g + h] for h in range(KV_GROUP)], axis=1)

    def ctx_scores(g, qt):
        return _dot(kc_ref[g, :, 0:HEAD_DIM], qt)

    def finish(g, acc):
        ot = acc[0:HEAD_DIM] * (1.0 / acc[HEAD_DIM:HEAD_DIM + 1])
        for k in range(KV_GROUP):
            h = KV_GROUP * g + k
            o_ref[HEAD_DIM * h:HEAD_DIM * (h + 1), :] = ot[:, k * tq:(k + 1) * tq].astype(o_ref.dtype)

    def exact(g):
        qt = queries(g)
        s = ctx_scores(g, qt)
        m = jnp.max(s, axis=0, keepdims=True)
        acc = _dot(vtc_ref[g], jnp.exp2(s - m).astype(BF16))

        def body(c, carry):
            m, acc = carry
            off = pl.multiple_of(c * tk, tk)
            s = _dot(kl_ref[g, pl.ds(off, tk), 0:HEAD_DIM], qt)
            m_new = jnp.maximum(m, jnp.max(s, axis=0, keepdims=True))
            p = jnp.exp2(s - m_new).astype(BF16)
            return m_new, jnp.exp2(m - m_new) * acc + _dot(vtl_ref[g, :, pl.ds(off, tk)], p)

        if n_lat_chunks:
            m, acc = lax.fori_loop(0, n_lat_chunks, body, (m, acc))
        return acc

    if not n_lat_chunks:
        for g in range(N_KV):
            finish(g, exact(g))
        return

    row = lax.broadcasted_iota(jnp.int32, (BF16_ROWS, n), 0)
    pad = jnp.zeros((K_COLS - HEAD_DIM - BF16_ROWS, n), BF16)

    def start(g):
        qt = queries(g)
        s = ctx_scores(g, qt)
        running = jnp.max(s, axis=0, keepdims=True)
        shift = running.astype(BF16).astype(F32)
        acc = _dot(vtc_ref[g], jnp.exp2(s - shift).astype(BF16))
        return dict(qt=qt, running=running, shift=shift, shifts=[shift, shift], acc=acc,
                    largest=jnp.ones((1, n), F32))

    def issue(g, c, st):
        new_shift = st["shifts"].pop(0)
        shift_rows = jnp.where(row == 0, -new_shift, 0.0).astype(BF16)
        qa = jnp.concatenate([st["qt"], shift_rows, pad], axis=0)
        st["issued"] = new_shift, _dot(kl_ref[g, pl.ds(c * tk, tk), :], qa)

    def consume(g, c, st):
        new_shift, s = st.pop("issued")
        p = jnp.exp2(s)
        pv = _dot(vtl_ref[g, :, pl.ds(c * tk, tk)], p.astype(BF16))
        st["acc"] = st["acc"] * jnp.exp2(st["shift"] - new_shift) + pv
        st["shift"] = new_shift
        pmax = jnp.max(p, axis=0, keepdims=True)
        st["largest"] = jnp.maximum(st["largest"], pmax)
        st["running"] = jnp.maximum(st["running"], new_shift + jnp.log2(pmax))
        st["shifts"].append(st["running"].astype(BF16).astype(F32))

    states = [start(g) for g in range(N_KV)]
    items = [(g, c) for c in range(n_lat_chunks) for g in range(N_KV)]
    issue(*items[0], states[items[0][0]])
    for k, (g, c) in enumerate(items):
        if k + 1 < len(items):
            gn, cn = items[k + 1]
            issue(gn, cn, states[gn])
        consume(g, c, states[g])
    for g in range(N_KV):
        finish(g, states[g]["acc"])
    overflow = jnp.max(jnp.maximum(states[0]["largest"], states[1]["largest"])) > MAX_SHIFTED_WEIGHT

    @pl.when(overflow)
    def _():
        for g in range(N_KV):
            finish(g, exact(g))


def _attention(qt, k_ctx, vt_ctx, k_lat, vt_lat, *, n_seq, q_len, ctx_len, lat_len):
    tq = min(256, q_len)
    tk = min(512, lat_len // 2) if lat_len else 0
    n_lat_chunks = lat_len // tk if lat_len else 0
    nq = q_len // tq
    kern = functools.partial(_attn_kernel, tq=tq, tk=tk, n_lat_chunks=n_lat_chunks)
    in_specs = [pl.BlockSpec((None, N_HEADS, HEAD_DIM, tq), lambda b, i: (b, 0, 0, i)),
                pl.BlockSpec((None, N_KV, ctx_len, K_COLS), lambda b, i: (b, 0, 0, 0)),
                pl.BlockSpec((None, N_KV, V_ROWS, ctx_len), lambda b, i: (b, 0, 0, 0))]
    args = [qt, k_ctx, vt_ctx]
    if lat_len:
        in_specs += [pl.BlockSpec((None, N_KV, lat_len, K_COLS), lambda b, i: (b, 0, 0, 0)),
                     pl.BlockSpec((None, N_KV, V_ROWS, lat_len), lambda b, i: (b, 0, 0, 0))]
        args += [k_lat, vt_lat]
    return pl.pallas_call(
        kern,
        grid=(n_seq, nq),
        in_specs=in_specs,
        out_specs=pl.BlockSpec((None, D_ATTN, tq), lambda b, i: (b, 0, i)),
        out_shape=jax.ShapeDtypeStruct((n_seq, D_ATTN, q_len), BF16),
        compiler_params=_params("arbitrary", "arbitrary"),
        name="attn_latent" if lat_len else "attn_context",
    )(*args)


def _lru_sequence(x_ref, g_ref, o_ref, h0, consts, scr, *, n_groups):
    cw_ref, cb_ref, wg_ref, bg_ref, hnsp = consts
    xe_ref, af_ref, bf_ref, ar_ref, br_ref = scr
    rows = SUBLANES * n_groups
    width = x_ref.shape[1]
    chunk = min(LRU_CHUNK, rows)
    n_chunks = rows // chunk

    xe_ref[pl.ds(2 * SUBLANES, rows), :] = x_ref[...]
    xe_ref[pl.ds(0, SUBLANES), :] = _shift_down(x_ref[pl.ds(rows - 2 * SUBLANES, SUBLANES), :])
    xe_ref[pl.ds(SUBLANES, SUBLANES), :] = _shift_down(x_ref[pl.ds(rows - SUBLANES, SUBLANES), :])
    xe_ref[pl.ds(rows + 2 * SUBLANES, SUBLANES), :] = _shift_up(x_ref[pl.ds(0, SUBLANES), :])

    def gates(c, _):
        r0 = pl.multiple_of(c * chunk, chunk)
        xc = cb_ref[...]
        for k in range(4):
            xc = xc + cw_ref[k:k + 1, :] * xe_ref[pl.ds(r0 + SUBLANES * k, chunk), :]
        t = jnp.tanh(_dot(xc.astype(BF16), wg_ref[...]) + bg_ref[...])
        hx = 0.5 * xc
        for d, (a_ref, b_ref) in enumerate(((af_ref, bf_ref), (ar_ref, br_ref))):
            t_r = t[:, (2 * d) * width:(2 * d + 1) * width]
            t_i = t[:, (2 * d + 1) * width:(2 * d + 2) * width]
            log2_a = hnsp[d] * t_r + hnsp[d]
            a = jnp.exp2(log2_a)
            y = jnp.tanh(LN2 * log2_a) * (-1.0 - a * a)
            b = jnp.where(y > 0.0, y * lax.rsqrt(y), 0.0) * (hx * t_i + hx)
            a_ref[pl.ds(r0, chunk), :] = a
            b_ref[pl.ds(r0, chunk), :] = b
        return 0

    lax.fori_loop(0, n_chunks, gates, 0)

    blk = min(SCAN_BLOCK, n_groups)
    blk_rows = blk * SUBLANES
    n_blocks = n_groups // blk

    def scan_block(a_ref, b_ref, r0, h, p, order):
        a_blk = a_ref[pl.ds(r0, blk_rows), :]
        b_blk = b_ref[pl.ds(r0, blk_rows), :]
        hs, ps = [None] * blk, [None] * blk
        for k in order:
            a = a_blk[k * SUBLANES:(k + 1) * SUBLANES]
            h = a * h + b_blk[k * SUBLANES:(k + 1) * SUBLANES]
            p = a * p
            hs[k], ps[k] = h, p
        b_ref[pl.ds(r0, blk_rows), :] = jnp.concatenate(hs, axis=0)
        a_ref[pl.ds(r0, blk_rows), :] = jnp.concatenate(ps, axis=0)
        return h, p

    def scan(i, carry):
        hf, pf, hr, pr = carry
        rf = pl.multiple_of(i * blk_rows, blk_rows)
        rr = pl.multiple_of((n_blocks - 1 - i) * blk_rows, blk_rows)
        hf, pf = scan_block(af_ref, bf_ref, rf, hf, pf, range(blk))
        hr, pr = scan_block(ar_ref, br_ref, rr, hr, pr, range(blk - 1, -1, -1))
        return hf, pf, hr, pr

    zero = jnp.zeros((SUBLANES, width), F32)
    one = jnp.ones((SUBLANES, width), F32)
    hf, pf, hr, pr = lax.fori_loop(0, n_blocks, scan, (zero, one, zero, one))

    e = h0[0]
    init_f = []
    for s in range(SUBLANES):
        init_f.append(e)
        e = hf[s:s + 1, :] + pf[s:s + 1, :] * e
    fin_f = e
    e = h0[1]
    init_r = [None] * SUBLANES
    for s in range(SUBLANES - 1, -1, -1):
        init_r[s] = e
        e = hr[s:s + 1, :] + pr[s:s + 1, :] * e
    fin_r = e

    if o_ref is not None:
        init_f = jnp.concatenate(init_f, axis=0)[None]
        init_r = jnp.concatenate(init_r, axis=0)[None]

        def combine(c, _):
            r0 = pl.multiple_of(c * chunk, chunk)
            sl = pl.ds(r0, chunk)
            shp = (chunk // SUBLANES, SUBLANES, width)
            h = (bf_ref[sl, :].reshape(shp) + af_ref[sl, :].reshape(shp) * init_f
                 + br_ref[sl, :].reshape(shp) + ar_ref[sl, :].reshape(shp) * init_r)
            o_ref[sl, :] = (h.reshape(chunk, width) * g_ref[sl, :]).astype(o_ref.dtype)
            return 0

        lax.fori_loop(0, n_chunks, combine, 0)
    return fin_f, fin_r


def _lru_kernel(*refs, ctx_groups, lat_groups, ctx_out):
    xc_ref, gc_ref, xl_ref, gl_ref, cw_ref, cb_ref, wa_ref, wx_ref, ba_ref, bx_ref, lam_ref = refs[:11]
    n_out = 2 if ctx_out else 1
    outs = refs[11:11 + n_out]
    scr = refs[11 + n_out:-2]
    wg_ref, bg_ref = refs[-2:]
    width = xc_ref.shape[1]
    blk = wa_ref.shape[-1]
    wg_ref[...] = jnp.zeros(wg_ref.shape, BF16)
    for t, (w_ref, b_ref, d) in enumerate(((wa_ref, ba_ref, 0), (wx_ref, bx_ref, 0),
                                           (wa_ref, ba_ref, 1), (wx_ref, bx_ref, 1))):
        for p in range(width // blk):
            wg_ref[blk * p:blk * (p + 1), t * width + blk * p:t * width + blk * (p + 1)] = (
                (0.5 * w_ref[d, p]).astype(BF16))
        bg_ref[:, t * width:(t + 1) * width] = 0.5 * b_ref[d:d + 1, :]
    z = -lam_ref[...]
    softplus = jnp.maximum(z, 0.0) + jnp.log1p(jnp.exp(-jnp.abs(z)))
    hnsp = [(-0.5 * LRU_C * LOG2E) * softplus[d:d + 1, :] for d in range(2)]
    consts = (cw_ref, cb_ref, wg_ref, bg_ref, hnsp)
    zero = jnp.zeros((1, xc_ref.shape[1]), F32)
    fin = _lru_sequence(xc_ref, gc_ref, outs[1] if ctx_out else None, (zero, zero), consts, scr,
                        n_groups=ctx_groups)
    _lru_sequence(xl_ref, gl_ref, outs[0], fin, consts, scr, n_groups=lat_groups)


def _lru(x_ctx, g_ctx, x_lat, g_lat, conv_w, conv_b, wa, wx, ba, bx, lam, *, n_seq, ctx_len, lat_len,
         ctx_out, layer):
    w = LRU_SLAB
    n_slab = D_LRU // w
    blk = wa.shape[-1]
    per = w // blk

    def vec(rows):
        return pl.BlockSpec((None, rows, w), lambda b, c: (layer, 0, c))

    gate_w = pl.BlockSpec((None, 2, per, blk, blk), lambda b, c: (layer, 0, c, 0, 0))
    kern = functools.partial(_lru_kernel, ctx_groups=ctx_len // SUBLANES,
                             lat_groups=lat_len // SUBLANES, ctx_out=ctx_out)
    ctx_spec = pl.BlockSpec((ctx_len, w), lambda b, c: (b, c))
    lat_spec = pl.BlockSpec((lat_len, w), lambda b, c: (b, c))
    out_specs = [lat_spec]
    out_shape = [jax.ShapeDtypeStruct((n_seq * lat_len, D_LRU), BF16)]
    if ctx_out:
        out_specs.append(ctx_spec)
        out_shape.append(jax.ShapeDtypeStruct((n_seq * ctx_len, D_LRU), BF16))
    max_rows = max(ctx_len, lat_len)
    return pl.pallas_call(
        kern,
        grid=(n_seq, n_slab),
        in_specs=[ctx_spec, ctx_spec, lat_spec, lat_spec,
                  vec(conv_w.shape[1]), vec(1), gate_w, gate_w, vec(2), vec(2), vec(2)],
        out_specs=out_specs,
        out_shape=out_shape,
        scratch_shapes=[pltpu.VMEM((max_rows + 3 * SUBLANES, w), F32)]
                       + [pltpu.VMEM((max_rows, w), F32)] * 4
                       + [pltpu.VMEM((w, 4 * w), BF16), pltpu.VMEM((1, 4 * w), F32)],
        compiler_params=_params("arbitrary", "arbitrary"),
        name="rglru_bidir",
    )(x_ctx, g_ctx, x_lat, g_lat, conv_w, conv_b, wa, wx, ba, bx, lam)


def _outproj_kernel(at_ref, r_ref, x_ref, gate_ref, nw_ref, w_ref, o_ref):
    n_sub = _n_sub(x_ref.shape[0])
    sub = x_ref.shape[0] // n_sub
    w = w_ref[...].astype(BF16)

    def project(r0):
        rows = pl.ds(r0, sub)
        a = at_ref[:, rows].astype(F32).T
        cat = jnp.concatenate([_rms(a), _rms(r_ref[rows, :].astype(F32))], axis=1) * nw_ref[...]
        return _dot(cat.astype(BF16), w)

    y = project(0)
    for s in range(n_sub):
        y_next = project((s + 1) * sub) if s + 1 < n_sub else None
        rows = pl.ds(s * sub, sub)
        o_ref[rows, :] = x_ref[rows, :] + gate_ref[...] * y
        y = y_next


def _outproj(a, r, x, mod, nw, w_out, *, n_seq, seq_len, shared_mod, mod_row, layer):
    tm = min(1024, seq_len)
    tps = seq_len // tm
    rows = n_seq * seq_len
    gate_idx = (lambda i: (mod_row, 0, 2)) if shared_mod else (lambda i: (mod_row + i // tps, 0, 2))
    return pl.pallas_call(
        _outproj_kernel,
        grid=(rows // tm,),
        in_specs=[pl.BlockSpec((None, D_ATTN, tm), lambda i: (i // tps, 0, i % tps)),
                  pl.BlockSpec((tm, D_LRU), lambda i: (i, 0)),
                  pl.BlockSpec((tm, D_MODEL), lambda i: (i, 0)),
                  pl.BlockSpec((None, 1, D_MODEL), gate_idx),
                  _resident((1, D_MODEL)),
                  _resident((D_MODEL, D_MODEL), layer)],
        out_specs=pl.BlockSpec((tm, D_MODEL), lambda i: (i, 0)),
        out_shape=jax.ShapeDtypeStruct((rows, D_MODEL), F32),
        compiler_params=_params("arbitrary"),
        name="out_proj",
    )(a, r, x, mod, nw, w_out)


def _ffn_kernel(*refs, tm, tps, final_norm):
    x_ref, xp_ref, xn_ref, mod_ref, wup_ref, cw_ref, cb_ref, wdn_ref = refs[:8]
    if final_norm:
        fnw_ref, o_ref, hext_ref, act_ref, ua_ref, ub_ref, slab_ref = refs[8:]
    else:
        o_ref, hext_ref, act_ref, ua_ref, ub_ref = refs[8:]
    assert (D_FF // FF_CHUNK) % 2 == 1
    jt = pl.program_id(0) % tps
    shift = mod_ref[:, 0:D_MODEL]
    scale = mod_ref[:, D_MODEL:2 * D_MODEL]
    gate = mod_ref[:, 2 * D_MODEL:3 * D_MODEL]

    def norm_mod(x):
        return _rms(x) * (1.0 + scale) + shift

    x = x_ref[...]
    hext_ref[pl.ds(BF16_ROWS, tm), :] = norm_mod(x).astype(BF16)
    hp = norm_mod(xp_ref[...])
    hp = jnp.where(jt == 0, _shift_down(hp), hp)
    hn = norm_mod(xn_ref[...])
    hn = jnp.where(jt == tps - 1, _shift_up(hn), hn)
    zeros = jnp.zeros_like(hp)
    hext_ref[pl.ds(0, BF16_ROWS), :] = jnp.concatenate([zeros, hp], axis=0).astype(BF16)
    hext_ref[pl.ds(BF16_ROWS + tm, BF16_ROWS), :] = jnp.concatenate([hn, zeros], axis=0).astype(BF16)

    def up(c, u_ref):
        for k in range(2):
            off = pl.multiple_of(k * D_FF + c * FF_CHUNK, FF_CHUNK)
            u_ref[k] = _dot(hext_ref[...], wup_ref[:, pl.ds(off, FF_CHUNK)])

    def conv(u_ref, k, off):
        sl = pl.ds(off, FF_CHUNK)
        return (cw_ref[0:1, sl] * u_ref[k, pl.ds(SUBLANES, tm), :]
                + cw_ref[1:2, sl] * u_ref[k, pl.ds(2 * SUBLANES, tm), :]
                + cw_ref[2:3, sl] * u_ref[k, pl.ds(3 * SUBLANES, tm), :]
                + cb_ref[:, sl])

    def activate(c, u_ref):
        og = pl.multiple_of(c * FF_CHUNK, FF_CHUNK)
        ov = pl.multiple_of(D_FF + c * FF_CHUNK, FF_CHUNK)
        yg = conv(u_ref, 0, og)
        yv = conv(u_ref, 1, ov)
        hg = 0.5 * yg
        act_ref[:, pl.ds(og, FF_CHUNK)] = ((hg + hg * jnp.tanh(hg)) * yv).astype(BF16)

    n_chunks = D_FF // FF_CHUNK
    up(0, ua_ref)

    def pair(j, _):
        up(2 * j + 1, ub_ref)
        activate(2 * j, ua_ref)
        up(2 * j + 2, ua_ref)
        activate(2 * j + 1, ub_ref)
        return 0

    for j in range((n_chunks - 1) // 2):
        pair(j, 0)
    activate(n_chunks - 1, ua_ref)
    out = x + gate * _dot(act_ref[...], wdn_ref[...])
    if not final_norm:
        o_ref[...] = out
        return
    out = _rms(out) * fnw_ref[...]
    for k in range(D_MODEL // LANES):
        slab_ref[k] = out[:, LANES * k:LANES * (k + 1)]
    for s in range(SUBLANES):
        for k in range(D_MODEL // LANES):
            o_ref[s, :, LANES * k:LANES * (k + 1)] = slab_ref[k, pl.ds(s, tm // SUBLANES, stride=SUBLANES), :]


def _ffn(x, mod, w_up, conv_w, conv_b, w_down, final_w, *, n_seq, seq_len, shared_mod, mod_row, layer):
    tm = min(1024, seq_len)
    tps = seq_len // tm
    rows = n_seq * seq_len
    gps = seq_len // SUBLANES
    gpt = tm // SUBLANES
    mod_idx = (lambda i: (mod_row, 0, 1)) if shared_mod else (lambda i: (mod_row + i // tps, 0, 1))

    def prev_idx(i):
        jt = i % tps
        return ((i // tps) * gps + jnp.where(jt == 0, gps - 1, jt * gpt - 1), 0)

    def next_idx(i):
        jt = i % tps
        return ((i // tps) * gps + jnp.where(jt == tps - 1, 0, (jt + 1) * gpt), 0)

    final_norm = final_w is not None
    in_specs = [pl.BlockSpec((tm, D_MODEL), lambda i: (i, 0)),
                pl.BlockSpec((SUBLANES, D_MODEL), prev_idx),
                pl.BlockSpec((SUBLANES, D_MODEL), next_idx),
                pl.BlockSpec((None, 1, 3 * D_MODEL), mod_idx),
                _resident((D_MODEL, 2 * D_FF), layer),
                _resident((3, 2 * D_FF), layer),
                _resident((1, 2 * D_FF), layer),
                _resident((D_FF, D_MODEL), layer)]
    args = [x, x, x, mod, w_up, conv_w, conv_b, w_down]
    if final_norm:
        in_specs.append(_resident((1, D_MODEL)))
        args.append(final_w)
    kern = functools.partial(_ffn_kernel, tm=tm, tps=tps, final_norm=final_norm)
    scratch = ([pltpu.VMEM((tm + 2 * BF16_ROWS, D_MODEL), BF16), pltpu.VMEM((tm, D_FF), BF16)]
               + [pltpu.VMEM((2, tm + 2 * BF16_ROWS, FF_CHUNK), F32)] * 2)
    if final_norm:
        out_spec = pl.BlockSpec((None, SUBLANES, tm // SUBLANES, D_MODEL),
                                lambda i: (i // tps, 0, i % tps, 0))
        out_shape = jax.ShapeDtypeStruct((n_seq, SUBLANES, gps, D_MODEL), F32)
        scratch.append(pltpu.VMEM((D_MODEL // LANES, tm, LANES), F32))
    else:
        out_spec = pl.BlockSpec((tm, D_MODEL), lambda i: (i, 0))
        out_shape = jax.ShapeDtypeStruct((rows, D_MODEL), F32)
    return pl.pallas_call(
        kern,
        grid=(rows // tm,),
        in_specs=in_specs,
        out_specs=out_spec,
        out_shape=out_shape,
        scratch_shapes=scratch,
        compiler_params=_params("arbitrary"),
        name="conv_ffn_final" if final_norm else "conv_ffn",
    )(*args)


def _to_segments(x):
    b, t, c = x.shape
    return x.reshape(b, SUBLANES, t // SUBLANES, c).transpose(0, 2, 1, 3).reshape(b * t, c)


def _rope_tables(seq_len):
    r = np.arange(seq_len)
    t = (r % SUBLANES) * (seq_len // SUBLANES) + r // SUBLANES
    row = (t // GRID_W).astype(np.float64)
    col = (t % GRID_W).astype(np.float64)
    pairs = HEAD_DIM // 4
    inv = ROPE_THETA ** (-np.arange(pairs, dtype=np.float64) / pairs)
    ang = np.concatenate([row[:, None] * inv, col[:, None] * inv], axis=-1)
    cos, sin = np.cos(ang), np.sin(ang)
    return (jnp.asarray(np.concatenate([cos, cos, cos, cos], axis=-1), F32),
            jnp.asarray(np.concatenate([-sin, sin, -sin, sin], axis=-1), F32))


def kernel(x, c, ctx, c_ctx, w_ada, b_ada, w_in, q_norm_w, k_norm_w, lru_conv_w, lru_conv_b, lru_wa,
           lru_ba, lru_wx, lru_bx, lru_lambda, attn_out_norm_w, lru_out_norm_w, w_out, ffn_w_up,
           ffn_conv_w, ffn_conv_b, ffn_w_down, final_norm_w):
    batch, seq, _ = x.shape
    ctx_len = ctx.shape[1]
    depth = w_in.shape[0]

    mod_rows = -(-(batch + 1) // SUBLANES) * SUBLANES
    cvec = jnp.zeros((mod_rows, D_MODEL), F32).at[:batch].set(c).at[batch].set(c_ctx)
    mods = _modulation(cvec, w_ada, b_ada)

    cos, sin = _rope_tables(seq)
    ones = jnp.asarray(np.ones((ctx_len, LANES)), F32)
    bd = jnp.asarray(np.kron(np.eye(QK_W // HEAD_DIM), np.ones((HEAD_DIM, HEAD_DIM))), BF16)

    x_lat = _to_segments(x)
    x_ctx = _to_segments(ctx)
    mods = mods.reshape(depth * mod_rows, 1, 6 * D_MODEL)
    w_up_b = ffn_w_up.astype(BF16)
    w_dn_b = ffn_w_down.astype(BF16)
    conv_b = ffn_conv_b.reshape(depth, 1, 2 * D_FF)

    for l in range(depth):
        ctx_out = l < depth - 1
        last = l == depth - 1
        lat = dict(n_seq=batch, seq_len=seq, shared_mod=False, mod_row=l * mod_rows, layer=l)
        cx = dict(n_seq=batch, seq_len=ctx_len, shared_mod=True, mod_row=l * mod_rows + batch, layer=l)
        nw_qk = jnp.concatenate([jnp.tile(q_norm_w[l], D_ATTN // HEAD_DIM),
                                 jnp.tile(k_norm_w[l], KV_W // HEAD_DIM)]).reshape(1, QK_W)
        qt_l, k_l, vt_l, xr_l, g_l = _inproj(x_lat, mods, w_in, bd, nw_qk, cos, sin,
                                             use_rope=True, **lat)
        qt_c, k_c, vt_c, xr_c, g_c = _inproj(x_ctx, mods, w_in, bd, nw_qk, ones, ones,
                                             use_rope=False, **cx)

        a_l = _attention(qt_l, k_c, vt_c, k_l, vt_l, n_seq=batch, q_len=seq, ctx_len=ctx_len,
                         lat_len=seq)
        r = _lru(xr_c, g_c, xr_l, g_l, lru_conv_w, lru_conv_b.reshape(depth, 1, D_LRU), lru_wa, lru_wx,
                 lru_ba, lru_bx, lru_lambda, n_seq=batch, ctx_len=ctx_len, lat_len=seq,
                 ctx_out=ctx_out, layer=l)

        nw_out = jnp.concatenate([attn_out_norm_w[l], lru_out_norm_w[l]]).reshape(1, D_MODEL)
        x_lat = _outproj(a_l, r[0], x_lat, mods, nw_out, w_out, **lat)
        x_lat = _ffn(x_lat, mods, w_up_b, ffn_conv_w, conv_b, w_dn_b,
                     final_norm_w.reshape(1, D_MODEL) if last else None, **lat)
        if ctx_out:
            a_c = _attention(qt_c, k_c, vt_c, None, None, n_seq=batch, q_len=ctx_len,
                             ctx_len=ctx_len, lat_len=0)
            x_ctx = _outproj(a_c, r[1], x_ctx, mods, nw_out, w_out, **cx)
            x_ctx = _ffn(x_ctx, mods, w_up_b, ffn_conv_w, conv_b, w_dn_b, None, **cx)
    return x_lat.reshape(batch, seq, D_MODEL)
```

```python
import functools

import jax
import jax.numpy as jnp
import numpy as np
from jax import lax
from jax.experimental import pallas as pl
from jax.experimental.pallas import tpu as pltpu

F32 = jnp.float32
BF16 = jnp.bfloat16

D_MODEL = 1024
D_ATTN = 512
D_LRU = 512
KV_W = 128
HEAD_DIM = 64
N_HEADS = 8
N_KV = 2
KV_GROUP = 4
LOG2E = 1.4426950408889634
LN2 = 0.6931471805599453
GELU_C0 = 0.7978845608028654
GELU_C1 = GELU_C0 * 0.044715
QK_W = D_ATTN + KV_W
D_IN = D_ATTN + 2 * KV_W + 2 * D_LRU
D_FF = 2816
LRU_C = 8.0
GRID_W = 64
ROPE_THETA = 10000.0
EPS = 1e-6
SUBLANES = 8
LANES = 128
BF16_ROWS = 16
V_ROWS = HEAD_DIM + BF16_ROWS
K_COLS = LANES
MAX_SHIFTED_WEIGHT = 2.0 ** 60
FF_CHUNK = 256
SUB_ROWS = 256
LRU_SLAB = 256
LRU_CHUNK = 1024
SCAN_BLOCK = 8
VMEM_LIMIT = 58 * 1024 * 1024


def _dot(a, b):
    return jnp.dot(a, b, preferred_element_type=F32)


def _sigmoid(z):
    return 0.5 * jnp.tanh(0.5 * z) + 0.5


def _rms(x):
    return x * lax.rsqrt(jnp.mean(x * x, axis=-1, keepdims=True) + EPS)


def _shift_down(g):
    rows = lax.broadcasted_iota(jnp.int32, g.shape, 0)
    return jnp.where(rows == 0, 0.0, pltpu.roll(g, 1, 0))


def _shift_up(g):
    rows = lax.broadcasted_iota(jnp.int32, g.shape, 0)
    return jnp.where(rows == SUBLANES - 1, 0.0, pltpu.roll(g, SUBLANES - 1, 0))


def _n_sub(tile_rows):
    return max(2, tile_rows // SUB_ROWS)


def _params(*sem):
    return pltpu.CompilerParams(dimension_semantics=sem, vmem_limit_bytes=VMEM_LIMIT)


def _resident(shape, layer=None):
    nd = len(shape)
    if layer is None:
        return pl.BlockSpec(shape, lambda *_: (0,) * nd, pipeline_mode=pl.Buffered(1))
    return pl.BlockSpec((None,) + tuple(shape), lambda *_: (layer,) + (0,) * nd,
                        pipeline_mode=pl.Buffered(1))


def _mod_kernel(c_ref, w_ref, b_ref, o_ref):
    c = c_ref[...]
    s = (c * _sigmoid(c)).astype(BF16)
    o_ref[...] = _dot(s, w_ref[...].astype(BF16)) + b_ref[...]


def _modulation(cvec, w_ada, b_ada, tn=1536):
    depth, d, n = w_ada.shape
    rows = cvec.shape[0]
    return pl.pallas_call(
        _mod_kernel,
        grid=(depth, n // tn),
        in_specs=[pl.BlockSpec((rows, d), lambda l, j: (0, 0)),
                  pl.BlockSpec((None, d, tn), lambda l, j: (l, 0, j)),
                  pl.BlockSpec((None, 1, tn), lambda l, j: (l, 0, j))],
        out_specs=pl.BlockSpec((None, rows, tn), lambda l, j: (l, 0, j)),
        out_shape=jax.ShapeDtypeStruct((depth, rows, n), F32),
        compiler_params=_params("arbitrary", "arbitrary"),
        name="adaln_mod",
    )(cvec, w_ada, b_ada.reshape(depth, 1, n))


def _inproj_kernel(x_ref, mod_ref, w_ref, bd_ref, nw_ref, cos_ref, sin_ref,
                   qt_ref, k_ref, vt_ref, xl_ref, g_ref, *, use_rope):
    shift = mod_ref[:, 0:D_MODEL]
    scale = mod_ref[:, D_MODEL:2 * D_MODEL]
    n_sub = _n_sub(x_ref.shape[0])
    sub = x_ref.shape[0] // n_sub
    w = w_ref[...].astype(BF16)

    def project(r0):
        h = _rms(x_ref[pl.ds(r0, sub), :]) * (1.0 + scale) + shift
        y = _dot(h.astype(BF16), w)
        qk = y[:, 0:QK_W]
        return y, _dot((qk * qk).astype(BF16), bd_ref[...])

    def finish(r0, y_ss):
        y, ss = y_ss
        rows = pl.ds(r0, sub)
        qk = y[:, 0:QK_W]
        qk = qk * lax.rsqrt(ss * (1.0 / HEAD_DIM) + EPS) * nw_ref[...]
        if use_rope:
            lane = lax.broadcasted_iota(jnp.int32, (sub, LANES), 1)
            first_half = (lane % HEAD_DIM) < (HEAD_DIM // 2)
            cos = cos_ref[rows, :]
            sin = sin_ref[rows, :]
            cols = []
            for k in range(QK_W // LANES):
                xc = qk[:, LANES * k:LANES * (k + 1)]
                other = jnp.where(first_half, pltpu.roll(xc, LANES - HEAD_DIM // 2, 1),
                                  pltpu.roll(xc, HEAD_DIM // 2, 1))
                cols.append(xc * cos + other * sin)
            qk = jnp.concatenate(cols, axis=1)
        qt = (qk[:, 0:D_ATTN] * (LOG2E * HEAD_DIM ** -0.5)).T
        for h in range(N_HEADS):
            qt_ref[h, :, rows] = qt[HEAD_DIM * h:HEAD_DIM * (h + 1)].astype(BF16)
        kk = qk[:, D_ATTN:QK_W]
        lane = lax.broadcasted_iota(jnp.int32, kk.shape, 1)
        one_hot = jnp.where(lane == HEAD_DIM, 1.0, 0.0)
        for g in range(N_KV):
            kg = kk if g == 0 else pltpu.roll(kk, HEAD_DIM, 1)
            k_ref[g, rows, :] = jnp.where(lane < HEAD_DIM, kg, one_hot).astype(BF16)
        vt = y[:, QK_W:QK_W + KV_W].T.astype(BF16)
        ones = jnp.ones((V_ROWS - HEAD_DIM, sub), BF16)
        for g in range(N_KV):
            vt_ref[g, :, rows] = jnp.concatenate([vt[HEAD_DIM * g:HEAD_DIM * (g + 1)], ones], axis=0)
        xl_ref[rows, :] = y[:, QK_W + KV_W:QK_W + KV_W + D_LRU]
        gt = y[:, QK_W + KV_W + D_LRU:D_IN]
        hg = 0.5 * gt
        g_ref[rows, :] = hg + hg * jnp.tanh(gt * (GELU_C0 + GELU_C1 * (gt * gt)))

    y = project(0)
    for s in range(n_sub):
        y_next = project((s + 1) * sub) if s + 1 < n_sub else None
        finish(s * sub, y)
        y = y_next


def _inproj(x, mod, w_in, bd, nw, cos, sin, *, n_seq, seq_len, shared_mod, mod_row, layer,
            use_rope):
    tm = min(1024, seq_len)
    tps = seq_len // tm
    rows = n_seq * seq_len
    mod_idx = (lambda i: (mod_row, 0, 0)) if shared_mod else (lambda i: (mod_row + i // tps, 0, 0))
    kern = functools.partial(_inproj_kernel, use_rope=use_rope)
    return pl.pallas_call(
        kern,
        grid=(rows // tm,),
        in_specs=[pl.BlockSpec((tm, D_MODEL), lambda i: (i, 0)),
                  pl.BlockSpec((None, 1, 2 * D_MODEL), mod_idx),
                  _resident((D_MODEL, D_IN), layer),
                  _resident((QK_W, QK_W)),
                  _resident((1, QK_W)),
                  pl.BlockSpec((tm, LANES), lambda i: (i % tps, 0)),
                  pl.BlockSpec((tm, LANES), lambda i: (i % tps, 0))],
        out_specs=[pl.BlockSpec((None, N_HEADS, HEAD_DIM, tm), lambda i: (i // tps, 0, 0, i % tps)),
                   pl.BlockSpec((None, N_KV, tm, K_COLS), lambda i: (i // tps, 0, i % tps, 0)),
                   pl.BlockSpec((None, N_KV, V_ROWS, tm), lambda i: (i // tps, 0, 0, i % tps)),
                   pl.BlockSpec((tm, D_LRU), lambda i: (i, 0)),
                   pl.BlockSpec((tm, D_LRU), lambda i: (i, 0))],
        out_shape=[jax.ShapeDtypeStruct((n_seq, N_HEADS, HEAD_DIM, seq_len), BF16),
                   jax.ShapeDtypeStruct((n_seq, N_KV, seq_len, K_COLS), BF16),
                   jax.ShapeDtypeStruct((n_seq, N_KV, V_ROWS, seq_len), BF16),
                   jax.ShapeDtypeStruct((rows, D_LRU), F32),
                   jax.ShapeDtypeStruct((rows, D_LRU), F32)],
        compiler_params=_params("arbitrary"),
        name="in_proj_rope" if use_rope else "in_proj",
    )(x, mod, w_in, bd, nw, cos, sin)


def _attn_kernel(*refs, tq, tk, n_lat_chunks):
    if n_lat_chunks:
        qt_ref, kc_ref, vtc_ref, kl_ref, vtl_ref, o_ref = refs
    else:
        qt_ref, kc_ref, vtc_ref, o_ref = refs
    n = KV_GROUP * tq

    def queries(g):
        return jnp.concatenate([qt_ref[KV_GROUP * g + h] for h in range(KV_GROUP)], axis=1)

    def ctx_scores(g, qt):
        return _dot(kc_ref[g, :, 0:HEAD_DIM], qt)

    def finish(g, acc):
        ot = acc[0:HEAD_DIM] * (1.0 / acc[HEAD_DIM:HEAD_DIM + 1])
        for k in range(KV_GROUP):
            h = KV_GROUP * g + k
            o_ref[HEAD_DIM * h:HEAD_DIM * (h + 1), :] = ot[:, k * tq:(k + 1) * tq].astype(o_ref.dtype)

    def exact(g):
        qt = queries(g)
        s = ctx_scores(g, qt)
        m = jnp.max(s, axis=0, keepdims=True)
        acc = _dot(vtc_ref[g], jnp.exp2(s - m).astype(BF16))

        def body(c, carry):
            m, acc = carry
            off = pl.multiple_of(c * tk, tk)
            s = _dot(kl_ref[g, pl.ds(off, tk), 0:HEAD_DIM], qt)
            m_new = jnp.maximum(m, jnp.max(s, axis=0, keepdims=True))
            p = jnp.exp2(s - m_new).astype(BF16)
            return m_new, jnp.exp2(m - m_new) * acc + _dot(vtl_ref[g, :, pl.ds(off, tk)], p)

        if n_lat_chunks:
            m, acc = lax.fori_loop(0, n_lat_chunks, body, (m, acc))
        return acc

    if not n_lat_chunks:
        for g in range(N_KV):
            finish(g, exact(g))
        return

    row = lax.broadcasted_iota(jnp.int32, (BF16_ROWS, n), 0)
    pad = jnp.zeros((K_COLS - HEAD_DIM - BF16_ROWS, n), BF16)

    def start(g):
        qt = queries(g)
        s = ctx_scores(g, qt)
        running = jnp.max(s, axis=0, keepdims=True)
        shift = running.astype(BF16).astype(F32)
        return dict(qt=qt, running=running, shift=shift, shifts=[shift, shift], ctx_scores=s,
                    largest=jnp.ones((1, n), F32))

    def absorb_context(g, st):
        p = jnp.exp2(st.pop("ctx_scores") - st["shift"]).astype(BF16)
        st["acc"] = _dot(vtc_ref[g], p)

    def issue(g, c, st):
        new_shift = st["shifts"].pop(0)
        shift_rows = jnp.where(row == 0, -new_shift, 0.0).astype(BF16)
        qa = jnp.concatenate([st["qt"], shift_rows, pad], axis=0)
        st["issued"] = new_shift, _dot(kl_ref[g, pl.ds(c * tk, tk), :], qa)

    def consume(g, c, st):
        new_shift, s = st.pop("issued")
        p = jnp.exp2(s)
        pv = _dot(vtl_ref[g, :, pl.ds(c * tk, tk)], p.astype(BF16))
        st["acc"] = st["acc"] * jnp.exp2(st["shift"] - new_shift) + pv
        st["shift"] = new_shift
        pmax = jnp.max(p, axis=0, keepdims=True)
        st["largest"] = jnp.maximum(st["largest"], pmax)
        st["running"] = jnp.maximum(st["running"], new_shift + jnp.log2(pmax))
        st["shifts"].append(st["running"].astype(BF16).astype(F32))

    states = [start(g) for g in range(N_KV)]
    items = [(g, c) for c in range(n_lat_chunks) for g in range(N_KV)]
    issue(*items[0], states[items[0][0]])
    for g in range(N_KV):
        absorb_context(g, states[g])
    for k, (g, c) in enumerate(items):
        if k + 1 < len(items):
            gn, cn = items[k + 1]
            issue(gn, cn, states[gn])
        consume(g, c, states[g])
    for g in range(N_KV):
        finish(g, states[g]["acc"])
    overflow = jnp.max(jnp.maximum(states[0]["largest"], states[1]["largest"])) > MAX_SHIFTED_WEIGHT

    @pl.when(overflow)
    def _():
        for g in range(N_KV):
            finish(g, exact(g))


def _attention(qt, k_ctx, vt_ctx, k_lat, vt_lat, *, n_seq, q_len, ctx_len, lat_len):
    tq = min(256, q_len)
    tk = min(512, lat_len // 2) if lat_len else 0
    n_lat_chunks = lat_len // tk if lat_len else 0
    nq = q_len // tq
    kern = functools.partial(_attn_kernel, tq=tq, tk=tk, n_lat_chunks=n_lat_chunks)
    in_specs = [pl.BlockSpec((None, N_HEADS, HEAD_DIM, tq), lambda b, i: (b, 0, 0, i)),
                pl.BlockSpec((None, N_KV, ctx_len, K_COLS), lambda b, i: (b, 0, 0, 0)),
                pl.BlockSpec((None, N_KV, V_ROWS, ctx_len), lambda b, i: (b, 0, 0, 0))]
    args = [qt, k_ctx, vt_ctx]
    if lat_len:
        in_specs += [pl.BlockSpec((None, N_KV, lat_len, K_COLS), lambda b, i: (b, 0, 0, 0)),
                     pl.BlockSpec((None, N_KV, V_ROWS, lat_len), lambda b, i: (b, 0, 0, 0))]
        args += [k_lat, vt_lat]
    return pl.pallas_call(
        kern,
        grid=(n_seq, nq),
        in_specs=in_specs,
        out_specs=pl.BlockSpec((None, D_ATTN, tq), lambda b, i: (b, 0, i)),
        out_shape=jax.ShapeDtypeStruct((n_seq, D_ATTN, q_len), BF16),
        compiler_params=_params("arbitrary", "arbitrary"),
        name="attn_latent" if lat_len else "attn_context",
    )(*args)


def _lru_sequence(x_ref, g_ref, o_ref, h0, consts, scr, *, n_groups):
    cw_ref, cb_ref, wg_ref, bg_ref, hnsp = consts
    xe_ref, af_ref, bf_ref, ar_ref, br_ref = scr
    rows = SUBLANES * n_groups
    width = x_ref.shape[1]
    chunk = min(LRU_CHUNK, rows)
    n_chunks = rows // chunk

    xe_ref[pl.ds(2 * SUBLANES, rows), :] = x_ref[...]
    xe_ref[pl.ds(0, SUBLANES), :] = _shift_down(x_ref[pl.ds(rows - 2 * SUBLANES, SUBLANES), :])
    xe_ref[pl.ds(SUBLANES, SUBLANES), :] = _shift_down(x_ref[pl.ds(rows - SUBLANES, SUBLANES), :])
    xe_ref[pl.ds(rows + 2 * SUBLANES, SUBLANES), :] = _shift_up(x_ref[pl.ds(0, SUBLANES), :])

    def gates(c, _):
        r0 = pl.multiple_of(c * chunk, chunk)
        xc = cb_ref[...]
        for k in range(4):
            xc = xc + cw_ref[k:k + 1, :] * xe_ref[pl.ds(r0 + SUBLANES * k, chunk), :]
        t = jnp.tanh(_dot(xc.astype(BF16), wg_ref[...]) + bg_ref[...])
        hx = 0.5 * xc
        for d, (a_ref, b_ref) in enumerate(((af_ref, bf_ref), (ar_ref, br_ref))):
            t_r = t[:, (2 * d) * width:(2 * d + 1) * width]
            t_i = t[:, (2 * d + 1) * width:(2 * d + 2) * width]
            log2_a = hnsp[d] * t_r + hnsp[d]
            a = jnp.exp2(log2_a)
            y = jnp.tanh(LN2 * log2_a) * (-1.0 - a * a)
            b = jnp.where(y > 0.0, y * lax.rsqrt(y), 0.0) * (hx * t_i + hx)
            a_ref[pl.ds(r0, chunk), :] = a
            b_ref[pl.ds(r0, chunk), :] = b
        return 0

    lax.fori_loop(0, n_chunks, gates, 0)

    blk = min(SCAN_BLOCK, n_groups)
    blk_rows = blk * SUBLANES
    n_blocks = n_groups // blk

    def scan_block(a_ref, b_ref, r0, h, p, order):
        a_blk = a_ref[pl.ds(r0, blk_rows), :]
        b_blk = b_ref[pl.ds(r0, blk_rows), :]
        hs, ps = [None] * blk, [None] * blk
        for k in order:
            a = a_blk[k * SUBLANES:(k + 1) * SUBLANES]
            h = a * h + b_blk[k * SUBLANES:(k + 1) * SUBLANES]
            p = a * p
            hs[k], ps[k] = h, p
        b_ref[pl.ds(r0, blk_rows), :] = jnp.concatenate(hs, axis=0)
        a_ref[pl.ds(r0, blk_rows), :] = jnp.concatenate(ps, axis=0)
        return h, p

    def scan(i, carry):
        hf, pf, hr, pr = carry
        rf = pl.multiple_of(i * blk_rows, blk_rows)
        rr = pl.multiple_of((n_blocks - 1 - i) * blk_rows, blk_rows)
        hf, pf = scan_block(af_ref, bf_ref, rf, hf, pf, range(blk))
        hr, pr = scan_block(ar_ref, br_ref, rr, hr, pr, range(blk - 1, -1, -1))
        return hf, pf, hr, pr

    zero = jnp.zeros((SUBLANES, width), F32)
    one = jnp.ones((SUBLANES, width), F32)
    hf, pf, hr, pr = lax.fori_loop(0, n_blocks, scan, (zero, one, zero, one))

    e = h0[0]
    init_f = []
    for s in range(SUBLANES):
        init_f.append(e)
        e = hf[s:s + 1, :] + pf[s:s + 1, :] * e
    fin_f = e
    e = h0[1]
    init_r = [None] * SUBLANES
    for s in range(SUBLANES - 1, -1, -1):
        init_r[s] = e
        e = hr[s:s + 1, :] + pr[s:s + 1, :] * e
    fin_r = e

    if o_ref is not None:
        init_f = jnp.concatenate(init_f, axis=0)[None]
        init_r = jnp.concatenate(init_r, axis=0)[None]

        def combine(c, _):
            r0 = pl.multiple_of(c * chunk, chunk)
            sl = pl.ds(r0, chunk)
            shp = (chunk // SUBLANES, SUBLANES, width)
            h = (bf_ref[sl, :].reshape(shp) + af_ref[sl, :].reshape(shp) * init_f
                 + br_ref[sl, :].reshape(shp) + ar_ref[sl, :].reshape(shp) * init_r)
            o_ref[sl, :] = (h.reshape(chunk, width) * g_ref[sl, :]).astype(o_ref.dtype)
            return 0

        lax.fori_loop(0, n_chunks, combine, 0)
    return fin_f, fin_r


def _lru_kernel(*refs, ctx_groups, lat_groups, ctx_out):
    xc_ref, gc_ref, xl_ref, gl_ref, cw_ref, cb_ref, wa_ref, wx_ref, ba_ref, bx_ref, lam_ref = refs[:11]
    n_out = 2 if ctx_out else 1
    outs = refs[11:11 + n_out]
    scr = refs[11 + n_out:-2]
    wg_ref, bg_ref = refs[-2:]
    width = xc_ref.shape[1]
    blk = wa_ref.shape[-1]
    wg_ref[...] = jnp.zeros(wg_ref.shape, BF16)
    for t, (w_ref, b_ref, d) in enumerate(((wa_ref, ba_ref, 0), (wx_ref, bx_ref, 0),
                                           (wa_ref, ba_ref, 1), (wx_ref, bx_ref, 1))):
        for p in range(width // blk):
            wg_ref[blk * p:blk * (p + 1), t * width + blk * p:t * width + blk * (p + 1)] = (
                (0.5 * w_ref[d, p]).astype(BF16))
        bg_ref[:, t * width:(t + 1) * width] = 0.5 * b_ref[d:d + 1, :]
    z = -lam_ref[...]
    softplus = jnp.maximum(z, 0.0) + jnp.log1p(jnp.exp(-jnp.abs(z)))
    hnsp = [(-0.5 * LRU_C * LOG2E) * softplus[d:d + 1, :] for d in range(2)]
    consts = (cw_ref, cb_ref, wg_ref, bg_ref, hnsp)
    zero = jnp.zeros((1, xc_ref.shape[1]), F32)
    fin = _lru_sequence(xc_ref, gc_ref, outs[1] if ctx_out else None, (zero, zero), consts, scr,
                        n_groups=ctx_groups)
    _lru_sequence(xl_ref, gl_ref, outs[0], fin, consts, scr, n_groups=lat_groups)


def _lru(x_ctx, g_ctx, x_lat, g_lat, conv_w, conv_b, wa, wx, ba, bx, lam, *, n_seq, ctx_len, lat_len,
         ctx_out, layer):
    w = LRU_SLAB
    n_slab = D_LRU // w
    blk = wa.shape[-1]
    per = w // blk

    def vec(rows):
        return pl.BlockSpec((None, rows, w), lambda b, c: (layer, 0, c))

    gate_w = pl.BlockSpec((None, 2, per, blk, blk), lambda b, c: (layer, 0, c, 0, 0))
    kern = functools.partial(_lru_kernel, ctx_groups=ctx_len // SUBLANES,
                             lat_groups=lat_len // SUBLANES, ctx_out=ctx_out)
    ctx_spec = pl.BlockSpec((ctx_len, w), lambda b, c: (b, c))
    lat_spec = pl.BlockSpec((lat_len, w), lambda b, c: (b, c))
    out_specs = [lat_spec]
    out_shape = [jax.ShapeDtypeStruct((n_seq * lat_len, D_LRU), BF16)]
    if ctx_out:
        out_specs.append(ctx_spec)
        out_shape.append(jax.ShapeDtypeStruct((n_seq * ctx_len, D_LRU), BF16))
    max_rows = max(ctx_len, lat_len)
    return pl.pallas_call(
        kern,
        grid=(n_seq, n_slab),
        in_specs=[ctx_spec, ctx_spec, lat_spec, lat_spec,
                  vec(conv_w.shape[1]), vec(1), gate_w, gate_w, vec(2), vec(2), vec(2)],
        out_specs=out_specs,
        out_shape=out_shape,
        scratch_shapes=[pltpu.VMEM((max_rows + 3 * SUBLANES, w), F32)]
                       + [pltpu.VMEM((max_rows, w), F32)] * 4
                       + [pltpu.VMEM((w, 4 * w), BF16), pltpu.VMEM((1, 4 * w), F32)],
        compiler_params=_params("arbitrary", "arbitrary"),
        name="rglru_bidir",
    )(x_ctx, g_ctx, x_lat, g_lat, conv_w, conv_b, wa, wx, ba, bx, lam)


def _outproj_kernel(at_ref, r_ref, x_ref, gate_ref, nw_ref, w_ref, o_ref):
    n_sub = _n_sub(x_ref.shape[0])
    sub = x_ref.shape[0] // n_sub
    w = w_ref[...].astype(BF16)

    def project(r0):
        rows = pl.ds(r0, sub)
        a = at_ref[:, rows].astype(F32).T
        cat = jnp.concatenate([_rms(a), _rms(r_ref[rows, :].astype(F32))], axis=1) * nw_ref[...]
        return _dot(cat.astype(BF16), w)

    y = project(0)
    for s in range(n_sub):
        y_next = project((s + 1) * sub) if s + 1 < n_sub else None
        rows = pl.ds(s * sub, sub)
        o_ref[rows, :] = x_ref[rows, :] + gate_ref[...] * y
        y = y_next


def _outproj(a, r, x, mod, nw, w_out, *, n_seq, seq_len, shared_mod, mod_row, layer):
    tm = min(1024, seq_len)
    tps = seq_len // tm
    rows = n_seq * seq_len
    gate_idx = (lambda i: (mod_row, 0, 2)) if shared_mod else (lambda i: (mod_row + i // tps, 0, 2))
    return pl.pallas_call(
        _outproj_kernel,
        grid=(rows // tm,),
        in_specs=[pl.BlockSpec((None, D_ATTN, tm), lambda i: (i // tps, 0, i % tps)),
                  pl.BlockSpec((tm, D_LRU), lambda i: (i, 0)),
                  pl.BlockSpec((tm, D_MODEL), lambda i: (i, 0)),
                  pl.BlockSpec((None, 1, D_MODEL), gate_idx),
                  _resident((1, D_MODEL)),
                  _resident((D_MODEL, D_MODEL), layer)],
        out_specs=pl.BlockSpec((tm, D_MODEL), lambda i: (i, 0)),
        out_shape=jax.ShapeDtypeStruct((rows, D_MODEL), F32),
        compiler_params=_params("arbitrary"),
        name="out_proj",
    )(a, r, x, mod, nw, w_out)


def _ffn_kernel(*refs, tm, tps, final_norm):
    x_ref, xp_ref, xn_ref, mod_ref, wup_ref, cw_ref, cb_ref, wdn_ref = refs[:8]
    if final_norm:
        fnw_ref, o_ref, hext_ref, act_ref, ua_ref, ub_ref, slab_ref = refs[8:]
    else:
        o_ref, hext_ref, act_ref, ua_ref, ub_ref = refs[8:]
    assert (D_FF // FF_CHUNK) % 2 == 1
    jt = pl.program_id(0) % tps
    shift = mod_ref[:, 0:D_MODEL]
    scale = mod_ref[:, D_MODEL:2 * D_MODEL]
    gate = mod_ref[:, 2 * D_MODEL:3 * D_MODEL]

    def norm_mod(x):
        return _rms(x) * (1.0 + scale) + shift

    x = x_ref[...]
    hext_ref[pl.ds(BF16_ROWS, tm), :] = norm_mod(x).astype(BF16)
    hp = norm_mod(xp_ref[...])
    hp = jnp.where(jt == 0, _shift_down(hp), hp)
    hn = norm_mod(xn_ref[...])
    hn = jnp.where(jt == tps - 1, _shift_up(hn), hn)
    zeros = jnp.zeros_like(hp)
    hext_ref[pl.ds(0, BF16_ROWS), :] = jnp.concatenate([zeros, hp], axis=0).astype(BF16)
    hext_ref[pl.ds(BF16_ROWS + tm, BF16_ROWS), :] = jnp.concatenate([hn, zeros], axis=0).astype(BF16)

    def up(c, u_ref):
        for k in range(2):
            off = pl.multiple_of(k * D_FF + c * FF_CHUNK, FF_CHUNK)
            u_ref[k] = _dot(hext_ref[...], wup_ref[:, pl.ds(off, FF_CHUNK)])

    def conv(u_ref, k, off):
        sl = pl.ds(off, FF_CHUNK)
        return (cw_ref[0:1, sl] * u_ref[k, pl.ds(SUBLANES, tm), :]
                + cw_ref[1:2, sl] * u_ref[k, pl.ds(2 * SUBLANES, tm), :]
                + cw_ref[2:3, sl] * u_ref[k, pl.ds(3 * SUBLANES, tm), :]
                + cb_ref[:, sl])

    def activate(c, u_ref):
        og = pl.multiple_of(c * FF_CHUNK, FF_CHUNK)
        ov = pl.multiple_of(D_FF + c * FF_CHUNK, FF_CHUNK)
        yg = conv(u_ref, 0, og)
        yv = conv(u_ref, 1, ov)
        hg = 0.5 * yg
        act_ref[:, pl.ds(og, FF_CHUNK)] = ((hg + hg * jnp.tanh(hg)) * yv).astype(BF16)

    n_chunks = D_FF // FF_CHUNK
    up(0, ua_ref)

    def pair(j, _):
        up(2 * j + 1, ub_ref)
        activate(2 * j, ua_ref)
        up(2 * j + 2, ua_ref)
        activate(2 * j + 1, ub_ref)
        return 0

    for j in range((n_chunks - 1) // 2):
        pair(j, 0)
    activate(n_chunks - 1, ua_ref)
    out = x + gate * _dot(act_ref[...], wdn_ref[...])
    if not final_norm:
        o_ref[...] = out
        return
    out = _rms(out) * fnw_ref[...]
    for k in range(D_MODEL // LANES):
        slab_ref[k] = out[:, LANES * k:LANES * (k + 1)]
    for s in range(SUBLANES):
        for k in range(D_MODEL // LANES):
            o_ref[s, :, LANES * k:LANES * (k + 1)] = slab_ref[k, pl.ds(s, tm // SUBLANES, stride=SUBLANES), :]


def _ffn(x, mod, w_up, conv_w, conv_b, w_down, final_w, *, n_seq, seq_len, shared_mod, mod_row, layer):
    tm = min(1024, seq_len)
    tps = seq_len // tm
    rows = n_seq * seq_len
    gps = seq_len // SUBLANES
    gpt = tm // SUBLANES
    mod_idx = (lambda i: (mod_row, 0, 1)) if shared_mod else (lambda i: (mod_row + i // tps, 0, 1))

    def prev_idx(i):
        jt = i % tps
        return ((i // tps) * gps + jnp.where(jt == 0, gps - 1, jt * gpt - 1), 0)

    def next_idx(i):
        jt = i % tps
        return ((i // tps) * gps + jnp.where(jt == tps - 1, 0, (jt + 1) * gpt), 0)

    final_norm = final_w is not None
    in_specs = [pl.BlockSpec((tm, D_MODEL), lambda i: (i, 0)),
                pl.BlockSpec((SUBLANES, D_MODEL), prev_idx),
                pl.BlockSpec((SUBLANES, D_MODEL), next_idx),
                pl.BlockSpec((None, 1, 3 * D_MODEL), mod_idx),
                _resident((D_MODEL, 2 * D_FF), layer),
                _resident((3, 2 * D_FF), layer),
                _resident((1, 2 * D_FF), layer),
                _resident((D_FF, D_MODEL), layer)]
    args = [x, x, x, mod, w_up, conv_w, conv_b, w_down]
    if final_norm:
        in_specs.append(_resident((1, D_MODEL)))
        args.append(final_w)
    kern = functools.partial(_ffn_kernel, tm=tm, tps=tps, final_norm=final_norm)
    scratch = ([pltpu.VMEM((tm + 2 * BF16_ROWS, D_MODEL), BF16), pltpu.VMEM((tm, D_FF), BF16)]
               + [pltpu.VMEM((2, tm + 2 * BF16_ROWS, FF_CHUNK), F32)] * 2)
    if final_norm:
        out_spec = pl.BlockSpec((None, SUBLANES, tm // SUBLANES, D_MODEL),
                                lambda i: (i // tps, 0, i % tps, 0))
        out_shape = jax.ShapeDtypeStruct((n_seq, SUBLANES, gps, D_MODEL), F32)
        scratch.append(pltpu.VMEM((D_MODEL // LANES, tm, LANES), F32))
    else:
        out_spec = pl.BlockSpec((tm, D_MODEL), lambda i: (i, 0))
        out_shape = jax.ShapeDtypeStruct((rows, D_MODEL), F32)
    return pl.pallas_call(
        kern,
        grid=(rows // tm,),
        in_specs=in_specs,
        out_specs=out_spec,
        out_shape=out_shape,
        scratch_shapes=scratch,
        compiler_params=_params("arbitrary"),
        name="conv_ffn_final" if final_norm else "conv_ffn",
    )(*args)


def _to_segments(x):
    b, t, c = x.shape
    return x.reshape(b, SUBLANES, t // SUBLANES, c).transpose(0, 2, 1, 3).reshape(b * t, c)


def _rope_tables(seq_len):
    r = np.arange(seq_len)
    t = (r % SUBLANES) * (seq_len // SUBLANES) + r // SUBLANES
    row = (t // GRID_W).astype(np.float64)
    col = (t % GRID_W).astype(np.float64)
    pairs = HEAD_DIM // 4
    inv = ROPE_THETA ** (-np.arange(pairs, dtype=np.float64) / pairs)
    ang = np.concatenate([row[:, None] * inv, col[:, None] * inv], axis=-1)
    cos, sin = np.cos(ang), np.sin(ang)
    return (jnp.asarray(np.concatenate([cos, cos, cos, cos], axis=-1), F32),
            jnp.asarray(np.concatenate([-sin, sin, -sin, sin], axis=-1), F32))


def kernel(x, c, ctx, c_ctx, w_ada, b_ada, w_in, q_norm_w, k_norm_w, lru_conv_w, lru_conv_b, lru_wa,
           lru_ba, lru_wx, lru_bx, lru_lambda, attn_out_norm_w, lru_out_norm_w, w_out, ffn_w_up,
           ffn_conv_w, ffn_conv_b, ffn_w_down, final_norm_w):
    batch, seq, _ = x.shape
    ctx_len = ctx.shape[1]
    depth = w_in.shape[0]

    mod_rows = -(-(batch + 1) // SUBLANES) * SUBLANES
    cvec = jnp.zeros((mod_rows, D_MODEL), F32).at[:batch].set(c).at[batch].set(c_ctx)
    mods = _modulation(cvec, w_ada, b_ada)

    cos, sin = _rope_tables(seq)
    ones = jnp.asarray(np.ones((ctx_len, LANES)), F32)
    bd = jnp.asarray(np.kron(np.eye(QK_W // HEAD_DIM), np.ones((HEAD_DIM, HEAD_DIM))), BF16)

    x_lat = _to_segments(x)
    x_ctx = _to_segments(ctx)
    mods = mods.reshape(depth * mod_rows, 1, 6 * D_MODEL)
    w_up_b = ffn_w_up.astype(BF16)
    w_dn_b = ffn_w_down.astype(BF16)
    conv_b = ffn_conv_b.reshape(depth, 1, 2 * D_FF)

    for l in range(depth):
        ctx_out = l < depth - 1
        last = l == depth - 1
        lat = dict(n_seq=batch, seq_len=seq, shared_mod=False, mod_row=l * mod_rows, layer=l)
        cx = dict(n_seq=batch, seq_len=ctx_len, shared_mod=True, mod_row=l * mod_rows + batch, layer=l)
        nw_qk = jnp.concatenate([jnp.tile(q_norm_w[l], D_ATTN // HEAD_DIM),
                                 jnp.tile(k_norm_w[l], KV_W // HEAD_DIM)]).reshape(1, QK_W)
        qt_l, k_l, vt_l, xr_l, g_l = _inproj(x_lat, mods, w_in, bd, nw_qk, cos, sin,
                                             use_rope=True, **lat)
        qt_c, k_c, vt_c, xr_c, g_c = _inproj(x_ctx, mods, w_in, bd, nw_qk, ones, ones,
                                             use_rope=False, **cx)

        a_l = _attention(qt_l, k_c, vt_c, k_l, vt_l, n_seq=batch, q_len=seq, ctx_len=ctx_len,
                         lat_len=seq)
        r = _lru(xr_c, g_c, xr_l, g_l, lru_conv_w, lru_conv_b.reshape(depth, 1, D_LRU), lru_wa, lru_wx,
                 lru_ba, lru_bx, lru_lambda, n_seq=batch, ctx_len=ctx_len, lat_len=seq,
                 ctx_out=ctx_out, layer=l)

        nw_out = jnp.concatenate([attn_out_norm_w[l], lru_out_norm_w[l]]).reshape(1, D_MODEL)
        x_lat = _outproj(a_l, r[0], x_lat, mods, nw_out, w_out, **lat)
        x_lat = _ffn(x_lat, mods, w_up_b, ffn_conv_w, conv_b, w_dn_b,
                     final_norm_w.reshape(1, D_MODEL) if last else None, **lat)
        if ctx_out:
            a_c = _attention(qt_c, k_c, vt_c, None, None, n_seq=batch, q_len=ctx_len,
                             ctx_len=ctx_len, lat_len=0)
            x_ctx = _outproj(a_c, r[1], x_ctx, mods, nw_out, w_out, **cx)
            x_ctx = _ffn(x_ctx, mods, w_up_b, ffn_conv_w, conv_b, w_dn_b, None, **cx)
    return x_lat.reshape(batch, seq, D_MODEL)
```

```python
import functools

import jax
import jax.numpy as jnp
import numpy as np
from jax import lax
from jax.experimental import pallas as pl
from jax.experimental.pallas import tpu as pltpu

F32 = jnp.float32
BF16 = jnp.bfloat16

D_MODEL = 1024
D_ATTN = 512
D_LRU = 512
KV_W = 128
HEAD_DIM = 64
N_HEADS = 8
N_KV = 2
KV_GROUP = 4
LOG2E = 1.4426950408889634
LN2 = 0.6931471805599453
GELU_C0 = 0.7978845608028654
GELU_C1 = GELU_C0 * 0.044715
QK_W = D_ATTN + KV_W
D_IN = D_ATTN + 2 * KV_W + 2 * D_LRU
D_FF = 2816
LRU_C = 8.0
GRID_W = 64
ROPE_THETA = 10000.0
EPS = 1e-6
SUBLANES = 8
LANES = 128
BF16_ROWS = 16
V_ROWS = HEAD_DIM + BF16_ROWS
K_COLS = LANES
MAX_SHIFTED_WEIGHT = 2.0 ** 60
FF_CHUNK = 256
SUB_ROWS = 256
LRU_SLAB = 256
LRU_CHUNK = 1024
SCAN_BLOCK = 8
VMEM_LIMIT = 58 * 1024 * 1024


def _dot(a, b):
    return jnp.dot(a, b, preferred_element_type=F32)


def _sigmoid(z):
    return 0.5 * jnp.tanh(0.5 * z) + 0.5


def _rms(x):
    return x * lax.rsqrt(jnp.mean(x * x, axis=-1, keepdims=True) + EPS)


def _shift_down(g):
    rows = lax.broadcasted_iota(jnp.int32, g.shape, 0)
    return jnp.where(rows == 0, 0.0, pltpu.roll(g, 1, 0))


def _shift_up(g):
    rows = lax.broadcasted_iota(jnp.int32, g.shape, 0)
    return jnp.where(rows == SUBLANES - 1, 0.0, pltpu.roll(g, SUBLANES - 1, 0))


def _n_sub(tile_rows):
    return max(2, tile_rows // SUB_ROWS)


def _params(*sem):
    return pltpu.CompilerParams(dimension_semantics=sem, vmem_limit_bytes=VMEM_LIMIT)


def _resident(shape, layer=None):
    nd = len(shape)
    if layer is None:
        return pl.BlockSpec(shape, lambda *_: (0,) * nd, pipeline_mode=pl.Buffered(1))
    return pl.BlockSpec((None,) + tuple(shape), lambda *_: (layer,) + (0,) * nd,
                        pipeline_mode=pl.Buffered(1))


def _mod_kernel(c_ref, w_ref, b_ref, o_ref):
    c = c_ref[...]
    s = (c * _sigmoid(c)).astype(BF16)
    o_ref[...] = _dot(s, w_ref[...].astype(BF16)) + b_ref[...]


def _modulation(cvec, w_ada, b_ada, tn=1536):
    depth, d, n = w_ada.shape
    rows = cvec.shape[0]
    return pl.pallas_call(
        _mod_kernel,
        grid=(depth, n // tn),
        in_specs=[pl.BlockSpec((rows, d), lambda l, j: (0, 0)),
                  pl.BlockSpec((None, d, tn), lambda l, j: (l, 0, j)),
                  pl.BlockSpec((None, 1, tn), lambda l, j: (l, 0, j))],
        out_specs=pl.BlockSpec((None, rows, tn), lambda l, j: (l, 0, j)),
        out_shape=jax.ShapeDtypeStruct((depth, rows, n), F32),
        compiler_params=_params("arbitrary", "arbitrary"),
        name="adaln_mod",
    )(cvec, w_ada, b_ada.reshape(depth, 1, n))


def _inproj_kernel(x_ref, mod_ref, w_ref, bd_ref, nw_ref, cos_ref, sin_ref,
                   qt_ref, k_ref, vt_ref, xl_ref, g_ref, *, use_rope):
    shift = mod_ref[:, 0:D_MODEL]
    scale = mod_ref[:, D_MODEL:2 * D_MODEL]
    n_sub = _n_sub(x_ref.shape[0])
    sub = x_ref.shape[0] // n_sub
    w = w_ref[...].astype(BF16)

    def project(r0):
        h = _rms(x_ref[pl.ds(r0, sub), :]) * (1.0 + scale) + shift
        y = _dot(h.astype(BF16), w)
        qk = y[:, 0:QK_W]
        return y, _dot((qk * qk).astype(BF16), bd_ref[...])

    def finish(r0, y_ss):
        y, ss = y_ss
        rows = pl.ds(r0, sub)
        qk = y[:, 0:QK_W]
        qk = qk * lax.rsqrt(ss * (1.0 / HEAD_DIM) + EPS) * nw_ref[...]
        if use_rope:
            lane = lax.broadcasted_iota(jnp.int32, (sub, LANES), 1)
            first_half = (lane % HEAD_DIM) < (HEAD_DIM // 2)
            cos = cos_ref[rows, :]
            sin = sin_ref[rows, :]
            cols = []
            for k in range(QK_W // LANES):
                xc = qk[:, LANES * k:LANES * (k + 1)]
                other = jnp.where(first_half, pltpu.roll(xc, LANES - HEAD_DIM // 2, 1),
                                  pltpu.roll(xc, HEAD_DIM // 2, 1))
                cols.append(xc * cos + other * sin)
            qk = jnp.concatenate(cols, axis=1)
        qt = (qk[:, 0:D_ATTN] * (LOG2E * HEAD_DIM ** -0.5)).T
        for h in range(N_HEADS):
            qt_ref[h, :, rows] = qt[HEAD_DIM * h:HEAD_DIM * (h + 1)].astype(BF16)
        kk = qk[:, D_ATTN:QK_W]
        lane = lax.broadcasted_iota(jnp.int32, kk.shape, 1)
        one_hot = jnp.where(lane == HEAD_DIM, 1.0, 0.0)
        for g in range(N_KV):
            kg = kk if g == 0 else pltpu.roll(kk, HEAD_DIM, 1)
            k_ref[g, rows, :] = jnp.where(lane < HEAD_DIM, kg, one_hot).astype(BF16)
        vt = y[:, QK_W:QK_W + KV_W].T.astype(BF16)
        ones = jnp.ones((V_ROWS - HEAD_DIM, sub), BF16)
        for g in range(N_KV):
            vt_ref[g, :, rows] = jnp.concatenate([vt[HEAD_DIM * g:HEAD_DIM * (g + 1)], ones], axis=0)
        xl_ref[rows, :] = y[:, QK_W + KV_W:QK_W + KV_W + D_LRU]
        gt = y[:, QK_W + KV_W + D_LRU:D_IN]
        hg = 0.5 * gt
        g_ref[rows, :] = hg + hg * jnp.tanh(gt * (GELU_C0 + GELU_C1 * (gt * gt)))

    y = project(0)
    for s in range(n_sub):
        y_next = project((s + 1) * sub) if s + 1 < n_sub else None
        finish(s * sub, y)
        y = y_next


def _inproj(x, mod, w_in, bd, nw, cos, sin, *, n_seq, seq_len, shared_mod, mod_row, layer,
            use_rope):
    tm = min(1024, seq_len)
    tps = seq_len // tm
    rows = n_seq * seq_len
    mod_idx = (lambda i: (mod_row, 0, 0)) if shared_mod else (lambda i: (mod_row + i // tps, 0, 0))
    kern = functools.partial(_inproj_kernel, use_rope=use_rope)
    return pl.pallas_call(
        kern,
        grid=(rows // tm,),
        in_specs=[pl.BlockSpec((tm, D_MODEL), lambda i: (i, 0)),
                  pl.BlockSpec((None, 1, 2 * D_MODEL), mod_idx),
                  _resident((D_MODEL, D_IN), layer),
                  _resident((QK_W, QK_W)),
                  _resident((1, QK_W)),
                  pl.BlockSpec((tm, LANES), lambda i: (i % tps, 0)),
                  pl.BlockSpec((tm, LANES), lambda i: (i % tps, 0))],
        out_specs=[pl.BlockSpec((None, N_HEADS, HEAD_DIM, tm), lambda i: (i // tps, 0, 0, i % tps)),
                   pl.BlockSpec((None, N_KV, tm, K_COLS), lambda i: (i // tps, 0, i % tps, 0)),
                   pl.BlockSpec((None, N_KV, V_ROWS, tm), lambda i: (i // tps, 0, 0, i % tps)),
                   pl.BlockSpec((tm, D_LRU), lambda i: (i, 0)),
                   pl.BlockSpec((tm, D_LRU), lambda i: (i, 0))],
        out_shape=[jax.ShapeDtypeStruct((n_seq, N_HEADS, HEAD_DIM, seq_len), BF16),
                   jax.ShapeDtypeStruct((n_seq, N_KV, seq_len, K_COLS), BF16),
                   jax.ShapeDtypeStruct((n_seq, N_KV, V_ROWS, seq_len), BF16),
                   jax.ShapeDtypeStruct((rows, D_LRU), F32),
                   jax.ShapeDtypeStruct((rows, D_LRU), F32)],
        compiler_params=_params("arbitrary"),
        name="in_proj_rope" if use_rope else "in_proj",
    )(x, mod, w_in, bd, nw, cos, sin)


def _attn_kernel(*refs, tq, tk, n_lat_chunks):
    if n_lat_chunks:
        qt_ref, kc_ref, vtc_ref, kl_ref, vtl_ref, o_ref = refs
    else:
        qt_ref, kc_ref, vtc_ref, o_ref = refs
    n = KV_GROUP * tq

    def queries(g):
        return jnp.concatenate([qt_ref[KV_GROUP * g + h] for h in range(KV_GROUP)], axis=1)

    def ctx_scores(g, qt):
        return _dot(kc_ref[g, :, 0:HEAD_DIM], qt)

    def finish(g, acc):
        ot = acc[0:HEAD_DIM] * (1.0 / acc[HEAD_DIM:HEAD_DIM + 1])
        for k in range(KV_GROUP):
            h = KV_GROUP * g + k
            o_ref[HEAD_DIM * h:HEAD_DIM * (h + 1), :] = ot[:, k * tq:(k + 1) * tq].astype(o_ref.dtype)

    def exact(g):
        qt = queries(g)
        s = ctx_scores(g, qt)
        m = jnp.max(s, axis=0, keepdims=True)
        acc = _dot(vtc_ref[g], jnp.exp2(s - m).astype(BF16))

        def body(c, carry):
            m, acc = carry
            off = pl.multiple_of(c * tk, tk)
            s = _dot(kl_ref[g, pl.ds(off, tk), 0:HEAD_DIM], qt)
            m_new = jnp.maximum(m, jnp.max(s, axis=0, keepdims=True))
            p = jnp.exp2(s - m_new).astype(BF16)
            return m_new, jnp.exp2(m - m_new) * acc + _dot(vtl_ref[g, :, pl.ds(off, tk)], p)

        if n_lat_chunks:
            m, acc = lax.fori_loop(0, n_lat_chunks, body, (m, acc))
        return acc

    if not n_lat_chunks:
        for g in range(N_KV):
            finish(g, exact(g))
        return

    row = lax.broadcasted_iota(jnp.int32, (BF16_ROWS, n), 0)
    pad = jnp.zeros((K_COLS - HEAD_DIM - BF16_ROWS, n), BF16)

    def start(g):
        qt = queries(g)
        s = ctx_scores(g, qt)
        running = jnp.max(s, axis=0, keepdims=True)
        shift = running.astype(BF16).astype(F32)
        acc = _dot(vtc_ref[g], jnp.exp2(s - shift).astype(BF16))
        return dict(qt=qt, running=running, shift=shift, shifts=[shift, shift], acc=acc,
                    largest=jnp.ones((1, n), F32))

    def issue(g, c, st):
        new_shift = st["shifts"].pop(0)
        shift_rows = jnp.where(row == 0, -new_shift, 0.0).astype(BF16)
        qa = jnp.concatenate([st["qt"], shift_rows, pad], axis=0)
        st["issued"] = new_shift, _dot(kl_ref[g, pl.ds(c * tk, tk), :], qa)

    def consume(g, c, st):
        new_shift, s = st.pop("issued")
        p = jnp.exp2(s)
        pv = _dot(vtl_ref[g, :, pl.ds(c * tk, tk)], p.astype(BF16))
        st["acc"] = st["acc"] * jnp.exp2(st["shift"] - new_shift) + pv
        st["shift"] = new_shift
        pmax = jnp.max(p, axis=0, keepdims=True)
        st["largest"] = jnp.maximum(st["largest"], pmax)
        st["running"] = jnp.maximum(st["running"], new_shift + jnp.log2(pmax))
        st["shifts"].append(st["running"].astype(BF16).astype(F32))

    states = [start(g) for g in range(N_KV)]
    for c in range(n_lat_chunks):
        for g in range(N_KV):
            issue(g, c, states[g])
            consume(g, c, states[g])
    for g in range(N_KV):
        finish(g, states[g]["acc"])
    overflow = jnp.max(jnp.maximum(states[0]["largest"], states[1]["largest"])) > MAX_SHIFTED_WEIGHT

    @pl.when(overflow)
    def _():
        for g in range(N_KV):
            finish(g, exact(g))


def _attention(qt, k_ctx, vt_ctx, k_lat, vt_lat, *, n_seq, q_len, ctx_len, lat_len):
    tq = min(256, q_len)
    tk = min(512, lat_len // 2) if lat_len else 0
    n_lat_chunks = lat_len // tk if lat_len else 0
    nq = q_len // tq
    kern = functools.partial(_attn_kernel, tq=tq, tk=tk, n_lat_chunks=n_lat_chunks)
    in_specs = [pl.BlockSpec((None, N_HEADS, HEAD_DIM, tq), lambda b, i: (b, 0, 0, i)),
                pl.BlockSpec((None, N_KV, ctx_len, K_COLS), lambda b, i: (b, 0, 0, 0)),
                pl.BlockSpec((None, N_KV, V_ROWS, ctx_len), lambda b, i: (b, 0, 0, 0))]
    args = [qt, k_ctx, vt_ctx]
    if lat_len:
        in_specs += [pl.BlockSpec((None, N_KV, lat_len, K_COLS), lambda b, i: (b, 0, 0, 0)),
                     pl.BlockSpec((None, N_KV, V_ROWS, lat_len), lambda b, i: (b, 0, 0, 0))]
        args += [k_lat, vt_lat]
    return pl.pallas_call(
        kern,
        grid=(n_seq, nq),
        in_specs=in_specs,
        out_specs=pl.BlockSpec((None, D_ATTN, tq), lambda b, i: (b, 0, i)),
        out_shape=jax.ShapeDtypeStruct((n_seq, D_ATTN, q_len), BF16),
        compiler_params=_params("arbitrary", "arbitrary"),
        name="attn_latent" if lat_len else "attn_context",
    )(*args)


def _lru_sequence(x_ref, g_ref, o_ref, h0, consts, scr, *, n_groups):
    cw_ref, cb_ref, wg_ref, bg_ref, hnsp = consts
    xe_ref, af_ref, bf_ref, ar_ref, br_ref = scr
    rows = SUBLANES * n_groups
    width = x_ref.shape[1]
    chunk = min(LRU_CHUNK, rows)
    n_chunks = rows // chunk

    xe_ref[pl.ds(2 * SUBLANES, rows), :] = x_ref[...]
    xe_ref[pl.ds(0, SUBLANES), :] = _shift_down(x_ref[pl.ds(rows - 2 * SUBLANES, SUBLANES), :])
    xe_ref[pl.ds(SUBLANES, SUBLANES), :] = _shift_down(x_ref[pl.ds(rows - SUBLANES, SUBLANES), :])
    xe_ref[pl.ds(rows + 2 * SUBLANES, SUBLANES), :] = _shift_up(x_ref[pl.ds(0, SUBLANES), :])

    def gates(c, _):
        r0 = pl.multiple_of(c * chunk, chunk)
        xc = cb_ref[...]
        for k in range(4):
            xc = xc + cw_ref[k:k + 1, :] * xe_ref[pl.ds(r0 + SUBLANES * k, chunk), :]
        t = jnp.tanh(_dot(xc.astype(BF16), wg_ref[...]) + bg_ref[...])
        hx = 0.5 * xc
        for d, (a_ref, b_ref) in enumerate(((af_ref, bf_ref), (ar_ref, br_ref))):
            t_r = t[:, (2 * d) * width:(2 * d + 1) * width]
            t_i = t[:, (2 * d + 1) * width:(2 * d + 2) * width]
            log2_a = hnsp[d] * t_r + hnsp[d]
            a = jnp.exp2(log2_a)
            y = jnp.tanh(LN2 * log2_a) * (-1.0 - a * a)
            b = jnp.where(y > 0.0, y * lax.rsqrt(y), 0.0) * (hx * t_i + hx)
            a_ref[pl.ds(r0, chunk), :] = a
            b_ref[pl.ds(r0, chunk), :] = b
        return 0

    lax.fori_loop(0, n_chunks, gates, 0)

    blk = min(SCAN_BLOCK, n_groups)
    blk_rows = blk * SUBLANES
    n_blocks = n_groups // blk

    def scan_block(a_ref, b_ref, r0, h, p, order):
        a_blk = a_ref[pl.ds(r0, blk_rows), :]
        b_blk = b_ref[pl.ds(r0, blk_rows), :]
        hs, ps = [None] * blk, [None] * blk
        for k in order:
            a = a_blk[k * SUBLANES:(k + 1) * SUBLANES]
            h = a * h + b_blk[k * SUBLANES:(k + 1) * SUBLANES]
            p = a * p
            hs[k], ps[k] = h, p
        b_ref[pl.ds(r0, blk_rows), :] = jnp.concatenate(hs, axis=0)
        a_ref[pl.ds(r0, blk_rows), :] = jnp.concatenate(ps, axis=0)
        return h, p

    def scan(i, carry):
        hf, pf, hr, pr = carry
        rf = pl.multiple_of(i * blk_rows, blk_rows)
        rr = pl.multiple_of((n_blocks - 1 - i) * blk_rows, blk_rows)
        hf, pf = scan_block(af_ref, bf_ref, rf, hf, pf, range(blk))
        hr, pr = scan_block(ar_ref, br_ref, rr, hr, pr, range(blk - 1, -1, -1))
        return hf, pf, hr, pr

    zero = jnp.zeros((SUBLANES, width), F32)
    one = jnp.ones((SUBLANES, width), F32)
    hf, pf, hr, pr = lax.fori_loop(0, n_blocks, scan, (zero, one, zero, one))

    e = h0[0]
    init_f = []
    for s in range(SUBLANES):
        init_f.append(e)
        e = hf[s:s + 1, :] + pf[s:s + 1, :] * e
    fin_f = e
    e = h0[1]
    init_r = [None] * SUBLANES
    for s in range(SUBLANES - 1, -1, -1):
        init_r[s] = e
        e = hr[s:s + 1, :] + pr[s:s + 1, :] * e
    fin_r = e

    if o_ref is not None:
        init_f = jnp.concatenate(init_f, axis=0)[None]
        init_r = jnp.concatenate(init_r, axis=0)[None]

        def combine(c, _):
            r0 = pl.multiple_of(c * chunk, chunk)
            sl = pl.ds(r0, chunk)
            shp = (chunk // SUBLANES, SUBLANES, width)
            h = (bf_ref[sl, :].reshape(shp) + af_ref[sl, :].reshape(shp) * init_f
                 + br_ref[sl, :].reshape(shp) + ar_ref[sl, :].reshape(shp) * init_r)
            o_ref[sl, :] = (h.reshape(chunk, width) * g_ref[sl, :]).astype(o_ref.dtype)
            return 0

        lax.fori_loop(0, n_chunks, combine, 0)
    return fin_f, fin_r


def _lru_kernel(*refs, ctx_groups, lat_groups, ctx_out):
    xc_ref, gc_ref, xl_ref, gl_ref, cw_ref, cb_ref, wa_ref, wx_ref, ba_ref, bx_ref, lam_ref = refs[:11]
    n_out = 2 if ctx_out else 1
    outs = refs[11:11 + n_out]
    scr = refs[11 + n_out:-2]
    wg_ref, bg_ref = refs[-2:]
    width = xc_ref.shape[1]
    blk = wa_ref.shape[-1]
    wg_ref[...] = jnp.zeros(wg_ref.shape, BF16)
    for t, (w_ref, b_ref, d) in enumerate(((wa_ref, ba_ref, 0), (wx_ref, bx_ref, 0),
                                           (wa_ref, ba_ref, 1), (wx_ref, bx_ref, 1))):
        for p in range(width // blk):
            wg_ref[blk * p:blk * (p + 1), t * width + blk * p:t * width + blk * (p + 1)] = (
                (0.5 * w_ref[d, p]).astype(BF16))
        bg_ref[:, t * width:(t + 1) * width] = 0.5 * b_ref[d:d + 1, :]
    z = -lam_ref[...]
    softplus = jnp.maximum(z, 0.0) + jnp.log1p(jnp.exp(-jnp.abs(z)))
    hnsp = [(-0.5 * LRU_C * LOG2E) * softplus[d:d + 1, :] for d in range(2)]
    consts = (cw_ref, cb_ref, wg_ref, bg_ref, hnsp)
    zero = jnp.zeros((1, xc_ref.shape[1]), F32)
    fin = _lru_sequence(xc_ref, gc_ref, outs[1] if ctx_out else None, (zero, zero), consts, scr,
                        n_groups=ctx_groups)
    _lru_sequence(xl_ref, gl_ref, outs[0], fin, consts, scr, n_groups=lat_groups)


def _lru(x_ctx, g_ctx, x_lat, g_lat, conv_w, conv_b, wa, wx, ba, bx, lam, *, n_seq, ctx_len, lat_len,
         ctx_out, layer):
    w = LRU_SLAB
    n_slab = D_LRU // w
    blk = wa.shape[-1]
    per = w // blk

    def vec(rows):
        return pl.BlockSpec((None, rows, w), lambda b, c: (layer, 0, c))

    gate_w = pl.BlockSpec((None, 2, per, blk, blk), lambda b, c: (layer, 0, c, 0, 0))
    kern = functools.partial(_lru_kernel, ctx_groups=ctx_len // SUBLANES,
                             lat_groups=lat_len // SUBLANES, ctx_out=ctx_out)
    ctx_spec = pl.BlockSpec((ctx_len, w), lambda b, c: (b, c))
    lat_spec = pl.BlockSpec((lat_len, w), lambda b, c: (b, c))
    out_specs = [lat_spec]
    out_shape = [jax.ShapeDtypeStruct((n_seq * lat_len, D_LRU), BF16)]
    if ctx_out:
        out_specs.append(ctx_spec)
        out_shape.append(jax.ShapeDtypeStruct((n_seq * ctx_len, D_LRU), BF16))
    max_rows = max(ctx_len, lat_len)
    return pl.pallas_call(
        kern,
        grid=(n_seq, n_slab),
        in_specs=[ctx_spec, ctx_spec, lat_spec, lat_spec,
                  vec(conv_w.shape[1]), vec(1), gate_w, gate_w, vec(2), vec(2), vec(2)],
        out_specs=out_specs,
        out_shape=out_shape,
        scratch_shapes=[pltpu.VMEM((max_rows + 3 * SUBLANES, w), F32)]
                       + [pltpu.VMEM((max_rows, w), F32)] * 4
                       + [pltpu.VMEM((w, 4 * w), BF16), pltpu.VMEM((1, 4 * w), F32)],
        compiler_params=_params("arbitrary", "arbitrary"),
        name="rglru_bidir",
    )(x_ctx, g_ctx, x_lat, g_lat, conv_w, conv_b, wa, wx, ba, bx, lam)


def _outproj_kernel(at_ref, r_ref, x_ref, gate_ref, nw_ref, w_ref, o_ref):
    n_sub = _n_sub(x_ref.shape[0])
    sub = x_ref.shape[0] // n_sub
    w = w_ref[...].astype(BF16)

    def project(r0):
        rows = pl.ds(r0, sub)
        a = at_ref[:, rows].astype(F32).T
        cat = jnp.concatenate([_rms(a), _rms(r_ref[rows, :].astype(F32))], axis=1) * nw_ref[...]
        return _dot(cat.astype(BF16), w)

    y = project(0)
    for s in range(n_sub):
        y_next = project((s + 1) * sub) if s + 1 < n_sub else None
        rows = pl.ds(s * sub, sub)
        o_ref[rows, :] = x_ref[rows, :] + gate_ref[...] * y
        y = y_next


def _outproj(a, r, x, mod, nw, w_out, *, n_seq, seq_len, shared_mod, mod_row, layer):
    tm = min(1024, seq_len)
    tps = seq_len // tm
    rows = n_seq * seq_len
    gate_idx = (lambda i: (mod_row, 0, 2)) if shared_mod else (lambda i: (mod_row + i // tps, 0, 2))
    return pl.pallas_call(
        _outproj_kernel,
        grid=(rows // tm,),
        in_specs=[pl.BlockSpec((None, D_ATTN, tm), lambda i: (i // tps, 0, i % tps)),
                  pl.BlockSpec((tm, D_LRU), lambda i: (i, 0)),
                  pl.BlockSpec((tm, D_MODEL), lambda i: (i, 0)),
                  pl.BlockSpec((None, 1, D_MODEL), gate_idx),
                  _resident((1, D_MODEL)),
                  _resident((D_MODEL, D_MODEL), layer)],
        out_specs=pl.BlockSpec((tm, D_MODEL), lambda i: (i, 0)),
        out_shape=jax.ShapeDtypeStruct((rows, D_MODEL), F32),
        compiler_params=_params("arbitrary"),
        name="out_proj",
    )(a, r, x, mod, nw, w_out)


def _ffn_kernel(*refs, tm, tps, final_norm):
    x_ref, xp_ref, xn_ref, mod_ref, wup_ref, cw_ref, cb_ref, wdn_ref = refs[:8]
    if final_norm:
        fnw_ref, o_ref, hext_ref, act_ref, ua_ref, ub_ref, slab_ref = refs[8:]
    else:
        o_ref, hext_ref, act_ref, ua_ref, ub_ref = refs[8:]
    assert (D_FF // FF_CHUNK) % 2 == 1
    jt = pl.program_id(0) % tps
    shift = mod_ref[:, 0:D_MODEL]
    scale = mod_ref[:, D_MODEL:2 * D_MODEL]
    gate = mod_ref[:, 2 * D_MODEL:3 * D_MODEL]

    def norm_mod(x):
        return _rms(x) * (1.0 + scale) + shift

    x = x_ref[...]
    hext_ref[pl.ds(BF16_ROWS, tm), :] = norm_mod(x).astype(BF16)
    hp = norm_mod(xp_ref[...])
    hp = jnp.where(jt == 0, _shift_down(hp), hp)
    hn = norm_mod(xn_ref[...])
    hn = jnp.where(jt == tps - 1, _shift_up(hn), hn)
    zeros = jnp.zeros_like(hp)
    hext_ref[pl.ds(0, BF16_ROWS), :] = jnp.concatenate([zeros, hp], axis=0).astype(BF16)
    hext_ref[pl.ds(BF16_ROWS + tm, BF16_ROWS), :] = jnp.concatenate([hn, zeros], axis=0).astype(BF16)

    def up(c, u_ref):
        for k in range(2):
            off = pl.multiple_of(k * D_FF + c * FF_CHUNK, FF_CHUNK)
            u_ref[k] = _dot(hext_ref[...], wup_ref[:, pl.ds(off, FF_CHUNK)])

    def conv(u_ref, k, off):
        sl = pl.ds(off, FF_CHUNK)
        return (cw_ref[0:1, sl] * u_ref[k, pl.ds(SUBLANES, tm), :]
                + cw_ref[1:2, sl] * u_ref[k, pl.ds(2 * SUBLANES, tm), :]
                + cw_ref[2:3, sl] * u_ref[k, pl.ds(3 * SUBLANES, tm), :]
                + cb_ref[:, sl])

    def activate(c, u_ref):
        og = pl.multiple_of(c * FF_CHUNK, FF_CHUNK)
        ov = pl.multiple_of(D_FF + c * FF_CHUNK, FF_CHUNK)
        yg = conv(u_ref, 0, og)
        yv = conv(u_ref, 1, ov)
        hg = 0.5 * yg
        act_ref[:, pl.ds(og, FF_CHUNK)] = ((hg + hg * jnp.tanh(hg)) * yv).astype(BF16)

    n_chunks = D_FF // FF_CHUNK
    up(0, ua_ref)

    def pair(j, _):
        up(2 * j + 1, ub_ref)
        activate(2 * j, ua_ref)
        up(2 * j + 2, ua_ref)
        activate(2 * j + 1, ub_ref)
        return 0

    for j in range((n_chunks - 1) // 2):
        pair(j, 0)
    activate(n_chunks - 1, ua_ref)
    out = x + gate * _dot(act_ref[...], wdn_ref[...])
    if not final_norm:
        o_ref[...] = out
        return
    out = _rms(out) * fnw_ref[...]
    for k in range(D_MODEL // LANES):
        slab_ref[k] = out[:, LANES * k:LANES * (k + 1)]
    for s in range(SUBLANES):
        for k in range(D_MODEL // LANES):
            o_ref[s, :, LANES * k:LANES * (k + 1)] = slab_ref[k, pl.ds(s, tm // SUBLANES, stride=SUBLANES), :]


def _ffn(x, mod, w_up, conv_w, conv_b, w_down, final_w, *, n_seq, seq_len, shared_mod, mod_row, layer):
    tm = min(1024, seq_len)
    tps = seq_len // tm
    rows = n_seq * seq_len
    gps = seq_len // SUBLANES
    gpt = tm // SUBLANES
    mod_idx = (lambda i: (mod_row, 0, 1)) if shared_mod else (lambda i: (mod_row + i // tps, 0, 1))

    def prev_idx(i):
        jt = i % tps
        return ((i // tps) * gps + jnp.where(jt == 0, gps - 1, jt * gpt - 1), 0)

    def next_idx(i):
        jt = i % tps
        return ((i // tps) * gps + jnp.where(jt == tps - 1, 0, (jt + 1) * gpt), 0)

    final_norm = final_w is not None
    in_specs = [pl.BlockSpec((tm, D_MODEL), lambda i: (i, 0)),
                pl.BlockSpec((SUBLANES, D_MODEL), prev_idx),
                pl.BlockSpec((SUBLANES, D_MODEL), next_idx),
                pl.BlockSpec((None, 1, 3 * D_MODEL), mod_idx),
                _resident((D_MODEL, 2 * D_FF), layer),
                _resident((3, 2 * D_FF), layer),
                _resident((1, 2 * D_FF), layer),
                _resident((D_FF, D_MODEL), layer)]
    args = [x, x, x, mod, w_up, conv_w, conv_b, w_down]
    if final_norm:
        in_specs.append(_resident((1, D_MODEL)))
        args.append(final_w)
    kern = functools.partial(_ffn_kernel, tm=tm, tps=tps, final_norm=final_norm)
    scratch = ([pltpu.VMEM((tm + 2 * BF16_ROWS, D_MODEL), BF16), pltpu.VMEM((tm, D_FF), BF16)]
               + [pltpu.VMEM((2, tm + 2 * BF16_ROWS, FF_CHUNK), F32)] * 2)
    if final_norm:
        out_spec = pl.BlockSpec((None, SUBLANES, tm // SUBLANES, D_MODEL),
                                lambda i: (i // tps, 0, i % tps, 0))
        out_shape = jax.ShapeDtypeStruct((n_seq, SUBLANES, gps, D_MODEL), F32)
        scratch.append(pltpu.VMEM((D_MODEL // LANES, tm, LANES), F32))
    else:
        out_spec = pl.BlockSpec((tm, D_MODEL), lambda i: (i, 0))
        out_shape = jax.ShapeDtypeStruct((rows, D_MODEL), F32)
    return pl.pallas_call(
        kern,
        grid=(rows // tm,),
        in_specs=in_specs,
        out_specs=out_spec,
        out_shape=out_shape,
        scratch_shapes=scratch,
        compiler_params=_params("arbitrary"),
        name="conv_ffn_final" if final_norm else "conv_ffn",
    )(*args)


def _to_segments(x):
    b, t, c = x.shape
    return x.reshape(b, SUBLANES, t // SUBLANES, c).transpose(0, 2, 1, 3).reshape(b * t, c)


def _rope_tables(seq_len):
    r = np.arange(seq_len)
    t = (r % SUBLANES) * (seq_len // SUBLANES) + r // SUBLANES
    row = (t // GRID_W).astype(np.float64)
    col = (t % GRID_W).astype(np.float64)
    pairs = HEAD_DIM // 4
    inv = ROPE_THETA ** (-np.arange(pairs, dtype=np.float64) / pairs)
    ang = np.concatenate([row[:, None] * inv, col[:, None] * inv], axis=-1)
    cos, sin = np.cos(ang), np.sin(ang)
    return (jnp.asarray(np.concatenate([cos, cos, cos, cos], axis=-1), F32),
            jnp.asarray(np.concatenate([-sin, sin, -sin, sin], axis=-1), F32))


def kernel(x, c, ctx, c_ctx, w_ada, b_ada, w_in, q_norm_w, k_norm_w, lru_conv_w, lru_conv_b, lru_wa,
           lru_ba, lru_wx, lru_bx, lru_lambda, attn_out_norm_w, lru_out_norm_w, w_out, ffn_w_up,
           ffn_conv_w, ffn_conv_b, ffn_w_down, final_norm_w):
    batch, seq, _ = x.shape
    ctx_len = ctx.shape[1]
    depth = w_in.shape[0]

    mod_rows = -(-(batch + 1) // SUBLANES) * SUBLANES
    cvec = jnp.zeros((mod_rows, D_MODEL), F32).at[:batch].set(c).at[batch].set(c_ctx)
    mods = _modulation(cvec, w_ada, b_ada)

    cos, sin = _rope_tables(seq)
    ones = jnp.asarray(np.ones((ctx_len, LANES)), F32)
    bd = jnp.asarray(np.kron(np.eye(QK_W // HEAD_DIM), np.ones((HEAD_DIM, HEAD_DIM))), BF16)

    x_lat = _to_segments(x)
    x_ctx = _to_segments(ctx)
    mods = mods.reshape(depth * mod_rows, 1, 6 * D_MODEL)
    w_up_b = ffn_w_up.astype(BF16)
    w_dn_b = ffn_w_down.astype(BF16)
    conv_b = ffn_conv_b.reshape(depth, 1, 2 * D_FF)

    for l in range(depth):
        ctx_out = l < depth - 1
        last = l == depth - 1
        lat = dict(n_seq=batch, seq_len=seq, shared_mod=False, mod_row=l * mod_rows, layer=l)
        cx = dict(n_seq=batch, seq_len=ctx_len, shared_mod=True, mod_row=l * mod_rows + batch, layer=l)
        nw_qk = jnp.concatenate([jnp.tile(q_norm_w[l], D_ATTN // HEAD_DIM),
                                 jnp.tile(k_norm_w[l], KV_W // HEAD_DIM)]).reshape(1, QK_W)
        qt_l, k_l, vt_l, xr_l, g_l = _inproj(x_lat, mods, w_in, bd, nw_qk, cos, sin,
                                             use_rope=True, **lat)
        qt_c, k_c, vt_c, xr_c, g_c = _inproj(x_ctx, mods, w_in, bd, nw_qk, ones, ones,
                                             use_rope=False, **cx)

        a_l = _attention(qt_l, k_c, vt_c, k_l, vt_l, n_seq=batch, q_len=seq, ctx_len=ctx_len,
                         lat_len=seq)
        r = _lru(xr_c, g_c, xr_l, g_l, lru_conv_w, lru_conv_b.reshape(depth, 1, D_LRU), lru_wa, lru_wx,
                 lru_ba, lru_bx, lru_lambda, n_seq=batch, ctx_len=ctx_len, lat_len=seq,
                 ctx_out=ctx_out, layer=l)

        nw_out = jnp.concatenate([attn_out_norm_w[l], lru_out_norm_w[l]]).reshape(1, D_MODEL)
        x_lat = _outproj(a_l, r[0], x_lat, mods, nw_out, w_out, **lat)
        x_lat = _ffn(x_lat, mods, w_up_b, ffn_conv_w, conv_b, w_dn_b,
                     final_norm_w.reshape(1, D_MODEL) if last else None, **lat)
        if ctx_out:
            a_c = _attention(qt_c, k_c, vt_c, None, None, n_seq=batch, q_len=ctx_len,
                             ctx_len=ctx_len, lat_len=0)
            x_ctx = _outproj(a_c, r[1], x_ctx, mods, nw_out, w_out, **cx)
            x_ctx = _ffn(x_ctx, mods, w_up_b, ffn_conv_w, conv_b, w_dn_b, None, **cx)
    return x_lat.reshape(batch, seq, D_MODEL)
```

```python
import functools

import jax
import jax.numpy as jnp
import numpy as np
from jax import lax
from jax.experimental import pallas as pl
from jax.experimental.pallas import tpu as pltpu

F32 = jnp.float32
BF16 = jnp.bfloat16

D_MODEL = 1024
D_ATTN = 512
D_LRU = 512
KV_W = 128
HEAD_DIM = 64
N_HEADS = 8
N_KV = 2
KV_GROUP = 4
LOG2E = 1.4426950408889634
LN2 = 0.6931471805599453
GELU_C0 = 0.7978845608028654
GELU_C1 = GELU_C0 * 0.044715
QK_W = D_ATTN + KV_W
D_IN = D_ATTN + 2 * KV_W + 2 * D_LRU
D_FF = 2816
LRU_C = 8.0
GRID_W = 64
ROPE_THETA = 10000.0
EPS = 1e-6
SUBLANES = 8
LANES = 128
BF16_ROWS = 16
V_ROWS = HEAD_DIM
K_COLS = LANES
MAX_SHIFTED_WEIGHT = 2.0 ** 60
FF_CHUNK = 256
SUB_ROWS = 256
LRU_SLAB = 256
LRU_CHUNK = 1024
SCAN_BLOCK = 8
VMEM_LIMIT = 58 * 1024 * 1024


def _dot(a, b):
    return jnp.dot(a, b, preferred_element_type=F32)


def _sigmoid(z):
    return 0.5 * jnp.tanh(0.5 * z) + 0.5


def _rms(x):
    return x * lax.rsqrt(jnp.mean(x * x, axis=-1, keepdims=True) + EPS)


def _shift_down(g):
    rows = lax.broadcasted_iota(jnp.int32, g.shape, 0)
    return jnp.where(rows == 0, 0.0, pltpu.roll(g, 1, 0))


def _shift_up(g):
    rows = lax.broadcasted_iota(jnp.int32, g.shape, 0)
    return jnp.where(rows == SUBLANES - 1, 0.0, pltpu.roll(g, SUBLANES - 1, 0))


def _n_sub(tile_rows):
    return max(2, tile_rows // SUB_ROWS)


def _params(*sem):
    return pltpu.CompilerParams(dimension_semantics=sem, vmem_limit_bytes=VMEM_LIMIT)


def _resident(shape, layer=None):
    nd = len(shape)
    if layer is None:
        return pl.BlockSpec(shape, lambda *_: (0,) * nd, pipeline_mode=pl.Buffered(1))
    return pl.BlockSpec((None,) + tuple(shape), lambda *_: (layer,) + (0,) * nd,
                        pipeline_mode=pl.Buffered(1))


def _mod_kernel(c_ref, w_ref, b_ref, o_ref):
    c = c_ref[...]
    s = (c * _sigmoid(c)).astype(BF16)
    o_ref[...] = _dot(s, w_ref[...].astype(BF16)) + b_ref[...]


def _modulation(cvec, w_ada, b_ada, tn=1536):
    depth, d, n = w_ada.shape
    rows = cvec.shape[0]
    return pl.pallas_call(
        _mod_kernel,
        grid=(depth, n // tn),
        in_specs=[pl.BlockSpec((rows, d), lambda l, j: (0, 0)),
                  pl.BlockSpec((None, d, tn), lambda l, j: (l, 0, j)),
                  pl.BlockSpec((None, 1, tn), lambda l, j: (l, 0, j))],
        out_specs=pl.BlockSpec((None, rows, tn), lambda l, j: (l, 0, j)),
        out_shape=jax.ShapeDtypeStruct((depth, rows, n), F32),
        compiler_params=_params("arbitrary", "arbitrary"),
        name="adaln_mod",
    )(cvec, w_ada, b_ada.reshape(depth, 1, n))


def _inproj_kernel(x_ref, mod_ref, w_ref, bd_ref, nw_ref, cos_ref, sin_ref,
                   qt_ref, k_ref, vt_ref, xl_ref, g_ref, *, use_rope):
    shift = mod_ref[:, 0:D_MODEL]
    scale = mod_ref[:, D_MODEL:2 * D_MODEL]
    n_sub = _n_sub(x_ref.shape[0])
    sub = x_ref.shape[0] // n_sub
    w = w_ref[...].astype(BF16)

    def project(r0):
        h = _rms(x_ref[pl.ds(r0, sub), :]) * (1.0 + scale) + shift
        y = _dot(h.astype(BF16), w)
        qk = y[:, 0:QK_W]
        return y, _dot((qk * qk).astype(BF16), bd_ref[...])

    def finish(r0, y_ss):
        y, ss = y_ss
        rows = pl.ds(r0, sub)
        qk = y[:, 0:QK_W]
        qk = qk * lax.rsqrt(ss * (1.0 / HEAD_DIM) + EPS) * nw_ref[...]
        if use_rope:
            lane = lax.broadcasted_iota(jnp.int32, (sub, LANES), 1)
            first_half = (lane % HEAD_DIM) < (HEAD_DIM // 2)
            cos = cos_ref[rows, :]
            sin = sin_ref[rows, :]
            cols = []
            for k in range(QK_W // LANES):
                xc = qk[:, LANES * k:LANES * (k + 1)]
                other = jnp.where(first_half, pltpu.roll(xc, LANES - HEAD_DIM // 2, 1),
                                  pltpu.roll(xc, HEAD_DIM // 2, 1))
                cols.append(xc * cos + other * sin)
            qk = jnp.concatenate(cols, axis=1)
        qt = (qk[:, 0:D_ATTN] * (LOG2E * HEAD_DIM ** -0.5)).T
        for h in range(N_HEADS):
            qt_ref[h, :, rows] = qt[HEAD_DIM * h:HEAD_DIM * (h + 1)].astype(BF16)
        kk = qk[:, D_ATTN:QK_W]
        lane = lax.broadcasted_iota(jnp.int32, kk.shape, 1)
        one_hot = jnp.where(lane == HEAD_DIM, 1.0, 0.0)
        for g in range(N_KV):
            kg = kk if g == 0 else pltpu.roll(kk, HEAD_DIM, 1)
            k_ref[g, rows, :] = jnp.where(lane < HEAD_DIM, kg, one_hot).astype(BF16)
        vt = y[:, QK_W:QK_W + KV_W].T.astype(BF16)
        for g in range(N_KV):
            vt_ref[g, :, rows] = vt[HEAD_DIM * g:HEAD_DIM * (g + 1)]
        xl_ref[rows, :] = y[:, QK_W + KV_W:QK_W + KV_W + D_LRU]
        gt = y[:, QK_W + KV_W + D_LRU:D_IN]
        hg = 0.5 * gt
        g_ref[rows, :] = hg + hg * jnp.tanh(gt * (GELU_C0 + GELU_C1 * (gt * gt)))

    y = project(0)
    for s in range(n_sub):
        y_next = project((s + 1) * sub) if s + 1 < n_sub else None
        finish(s * sub, y)
        y = y_next


def _inproj(x, mod, w_in, bd, nw, cos, sin, *, n_seq, seq_len, shared_mod, mod_row, layer,
            use_rope):
    tm = min(1024, seq_len)
    tps = seq_len // tm
    rows = n_seq * seq_len
    mod_idx = (lambda i: (mod_row, 0, 0)) if shared_mod else (lambda i: (mod_row + i // tps, 0, 0))
    kern = functools.partial(_inproj_kernel, use_rope=use_rope)
    return pl.pallas_call(
        kern,
        grid=(rows // tm,),
        in_specs=[pl.BlockSpec((tm, D_MODEL), lambda i: (i, 0)),
                  pl.BlockSpec((None, 1, 2 * D_MODEL), mod_idx),
                  _resident((D_MODEL, D_IN), layer),
                  _resident((QK_W, QK_W)),
                  _resident((1, QK_W)),
                  pl.BlockSpec((tm, LANES), lambda i: (i % tps, 0)),
                  pl.BlockSpec((tm, LANES), lambda i: (i % tps, 0))],
        out_specs=[pl.BlockSpec((None, N_HEADS, HEAD_DIM, tm), lambda i: (i // tps, 0, 0, i % tps)),
                   pl.BlockSpec((None, N_KV, tm, K_COLS), lambda i: (i // tps, 0, i % tps, 0)),
                   pl.BlockSpec((None, N_KV, V_ROWS, tm), lambda i: (i // tps, 0, 0, i % tps)),
                   pl.BlockSpec((tm, D_LRU), lambda i: (i, 0)),
                   pl.BlockSpec((tm, D_LRU), lambda i: (i, 0))],
        out_shape=[jax.ShapeDtypeStruct((n_seq, N_HEADS, HEAD_DIM, seq_len), BF16),
                   jax.ShapeDtypeStruct((n_seq, N_KV, seq_len, K_COLS), BF16),
                   jax.ShapeDtypeStruct((n_seq, N_KV, V_ROWS, seq_len), BF16),
                   jax.ShapeDtypeStruct((rows, D_LRU), F32),
                   jax.ShapeDtypeStruct((rows, D_LRU), F32)],
        compiler_params=_params("arbitrary"),
        name="in_proj_rope" if use_rope else "in_proj",
    )(x, mod, w_in, bd, nw, cos, sin)


def _attn_kernel(*refs, tq, tk, n_lat_chunks):
    if n_lat_chunks:
        qt_ref, kc_ref, vtc_ref, kl_ref, vtl_ref, o_ref = refs
    else:
        qt_ref, kc_ref, vtc_ref, o_ref = refs
    n = KV_GROUP * tq

    def queries(g):
        return jnp.concatenate([qt_ref[KV_GROUP * g + h] for h in range(KV_GROUP)], axis=1)

    def ctx_scores(g, qt):
        return _dot(kc_ref[g, :, 0:HEAD_DIM], qt)

    def weigh(vt, p):
        psum = jnp.broadcast_to(jnp.sum(p, axis=0, keepdims=True), (SUBLANES, p.shape[1]))
        return jnp.concatenate([_dot(vt, p.astype(BF16)), psum], axis=0)

    def finish(g, acc):
        ot = acc[0:HEAD_DIM] * (1.0 / acc[HEAD_DIM:HEAD_DIM + 1])
        for k in range(KV_GROUP):
            h = KV_GROUP * g + k
            o_ref[HEAD_DIM * h:HEAD_DIM * (h + 1), :] = ot[:, k * tq:(k + 1) * tq].astype(o_ref.dtype)

    def exact(g):
        qt = queries(g)
        s = ctx_scores(g, qt)
        m = jnp.max(s, axis=0, keepdims=True)
        acc = weigh(vtc_ref[g], jnp.exp2(s - m))

        def body(c, carry):
            m, acc = carry
            off = pl.multiple_of(c * tk, tk)
            s = _dot(kl_ref[g, pl.ds(off, tk), 0:HEAD_DIM], qt)
            m_new = jnp.maximum(m, jnp.max(s, axis=0, keepdims=True))
            p = jnp.exp2(s - m_new)
            return m_new, jnp.exp2(m - m_new) * acc + weigh(vtl_ref[g, :, pl.ds(off, tk)], p)

        if n_lat_chunks:
            m, acc = lax.fori_loop(0, n_lat_chunks, body, (m, acc))
        return acc

    if not n_lat_chunks:
        for g in range(N_KV):
            finish(g, exact(g))
        return

    row = lax.broadcasted_iota(jnp.int32, (BF16_ROWS, n), 0)
    pad = jnp.zeros((K_COLS - HEAD_DIM - BF16_ROWS, n), BF16)

    def start(g):
        qt = queries(g)
        s = ctx_scores(g, qt)
        running = jnp.max(s, axis=0, keepdims=True)
        shift = running.astype(BF16).astype(F32)
        acc = weigh(vtc_ref[g], jnp.exp2(s - shift))
        return dict(qt=qt, running=running, shift=shift, shifts=[shift, shift], acc=acc,
                    largest=jnp.ones((1, n), F32))

    def issue(g, c, st):
        new_shift = st["shifts"].pop(0)
        shift_rows = jnp.where(row == 0, -new_shift, 0.0).astype(BF16)
        qa = jnp.concatenate([st["qt"], shift_rows, pad], axis=0)
        st["issued"] = new_shift, _dot(kl_ref[g, pl.ds(c * tk, tk), :], qa)

    def consume(g, c, st):
        new_shift, s = st.pop("issued")
        p = jnp.exp2(s)
        pv = weigh(vtl_ref[g, :, pl.ds(c * tk, tk)], p)
        st["acc"] = st["acc"] * jnp.exp2(st["shift"] - new_shift) + pv
        st["shift"] = new_shift
        pmax = jnp.max(p, axis=0, keepdims=True)
        st["largest"] = jnp.maximum(st["largest"], pmax)
        st["running"] = jnp.maximum(st["running"], new_shift + jnp.log2(pmax))
        st["shifts"].append(st["running"].astype(BF16).astype(F32))

    states = [start(g) for g in range(N_KV)]
    items = [(g, c) for c in range(n_lat_chunks) for g in range(N_KV)]
    issue(*items[0], states[items[0][0]])
    for k, (g, c) in enumerate(items):
        if k + 1 < len(items):
            gn, cn = items[k + 1]
            issue(gn, cn, states[gn])
        consume(g, c, states[g])
    for g in range(N_KV):
        finish(g, states[g]["acc"])
    overflow = jnp.max(jnp.maximum(states[0]["largest"], states[1]["largest"])) > MAX_SHIFTED_WEIGHT

    @pl.when(overflow)
    def _():
        for g in range(N_KV):
            finish(g, exact(g))


def _attention(qt, k_ctx, vt_ctx, k_lat, vt_lat, *, n_seq, q_len, ctx_len, lat_len):
    tq = min(256, q_len)
    tk = min(512, lat_len // 2) if lat_len else 0
    n_lat_chunks = lat_len // tk if lat_len else 0
    nq = q_len // tq
    kern = functools.partial(_attn_kernel, tq=tq, tk=tk, n_lat_chunks=n_lat_chunks)
    in_specs = [pl.BlockSpec((None, N_HEADS, HEAD_DIM, tq), lambda b, i: (b, 0, 0, i)),
                pl.BlockSpec((None, N_KV, ctx_len, K_COLS), lambda b, i: (b, 0, 0, 0)),
                pl.BlockSpec((None, N_KV, V_ROWS, ctx_len), lambda b, i: (b, 0, 0, 0))]
    args = [qt, k_ctx, vt_ctx]
    if lat_len:
        in_specs += [pl.BlockSpec((None, N_KV, lat_len, K_COLS), lambda b, i: (b, 0, 0, 0)),
                     pl.BlockSpec((None, N_KV, V_ROWS, lat_len), lambda b, i: (b, 0, 0, 0))]
        args += [k_lat, vt_lat]
    return pl.pallas_call(
        kern,
        grid=(n_seq, nq),
        in_specs=in_specs,
        out_specs=pl.BlockSpec((None, D_ATTN, tq), lambda b, i: (b, 0, i)),
        out_shape=jax.ShapeDtypeStruct((n_seq, D_ATTN, q_len), BF16),
        compiler_params=_params("arbitrary", "arbitrary"),
        name="attn_latent" if lat_len else "attn_context",
    )(*args)


def _lru_sequence(x_ref, g_ref, o_ref, h0, consts, scr, *, n_groups):
    cw_ref, cb_ref, wg_ref, bg_ref, hnsp = consts
    xe_ref, af_ref, bf_ref, ar_ref, br_ref = scr
    rows = SUBLANES * n_groups
    width = x_ref.shape[1]
    chunk = min(LRU_CHUNK, rows)
    n_chunks = rows // chunk

    xe_ref[pl.ds(2 * SUBLANES, rows), :] = x_ref[...]
    xe_ref[pl.ds(0, SUBLANES), :] = _shift_down(x_ref[pl.ds(rows - 2 * SUBLANES, SUBLANES), :])
    xe_ref[pl.ds(SUBLANES, SUBLANES), :] = _shift_down(x_ref[pl.ds(rows - SUBLANES, SUBLANES), :])
    xe_ref[pl.ds(rows + 2 * SUBLANES, SUBLANES), :] = _shift_up(x_ref[pl.ds(0, SUBLANES), :])

    def gates(c, _):
        r0 = pl.multiple_of(c * chunk, chunk)
        xc = cb_ref[...]
        for k in range(4):
            xc = xc + cw_ref[k:k + 1, :] * xe_ref[pl.ds(r0 + SUBLANES * k, chunk), :]
        t = jnp.tanh(_dot(xc.astype(BF16), wg_ref[...]) + bg_ref[...])
        hx = 0.5 * xc
        for d, (a_ref, b_ref) in enumerate(((af_ref, bf_ref), (ar_ref, br_ref))):
            t_r = t[:, (2 * d) * width:(2 * d + 1) * width]
            t_i = t[:, (2 * d + 1) * width:(2 * d + 2) * width]
            log2_a = hnsp[d] * t_r + hnsp[d]
            a = jnp.exp2(log2_a)
            y = jnp.tanh(LN2 * log2_a) * (-1.0 - a * a)
            b = jnp.where(y > 0.0, y * lax.rsqrt(y), 0.0) * (hx * t_i + hx)
            a_ref[pl.ds(r0, chunk), :] = a
            b_ref[pl.ds(r0, chunk), :] = b
        return 0

    lax.fori_loop(0, n_chunks, gates, 0)

    blk = min(SCAN_BLOCK, n_groups)
    blk_rows = blk * SUBLANES
    n_blocks = n_groups // blk

    def scan_block(a_ref, b_ref, r0, h, p, order):
        a_blk = a_ref[pl.ds(r0, blk_rows), :]
        b_blk = b_ref[pl.ds(r0, blk_rows), :]
        hs, ps = [None] * blk, [None] * blk
        for k in order:
            a = a_blk[k * SUBLANES:(k + 1) * SUBLANES]
            h = a * h + b_blk[k * SUBLANES:(k + 1) * SUBLANES]
            p = a * p
            hs[k], ps[k] = h, p
        b_ref[pl.ds(r0, blk_rows), :] = jnp.concatenate(hs, axis=0)
        a_ref[pl.ds(r0, blk_rows), :] = jnp.concatenate(ps, axis=0)
        return h, p

    def scan(i, carry):
        hf, pf, hr, pr = carry
        rf = pl.multiple_of(i * blk_rows, blk_rows)
        rr = pl.multiple_of((n_blocks - 1 - i) * blk_rows, blk_rows)
        hf, pf = scan_block(af_ref, bf_ref, rf, hf, pf, range(blk))
        hr, pr = scan_block(ar_ref, br_ref, rr, hr, pr, range(blk - 1, -1, -1))
        return hf, pf, hr, pr

    zero = jnp.zeros((SUBLANES, width), F32)
    one = jnp.ones((SUBLANES, width), F32)
    hf, pf, hr, pr = lax.fori_loop(0, n_blocks, scan, (zero, one, zero, one))

    e = h0[0]
    init_f = []
    for s in range(SUBLANES):
        init_f.append(e)
        e = hf[s:s + 1, :] + pf[s:s + 1, :] * e
    fin_f = e
    e = h0[1]
    init_r = [None] * SUBLANES
    for s in range(SUBLANES - 1, -1, -1):
        init_r[s] = e
        e = hr[s:s + 1, :] + pr[s:s + 1, :] * e
    fin_r = e

    if o_ref is not None:
        init_f = jnp.concatenate(init_f, axis=0)[None]
        init_r = jnp.concatenate(init_r, axis=0)[None]

        def combine(c, _):
            r0 = pl.multiple_of(c * chunk, chunk)
            sl = pl.ds(r0, chunk)
            shp = (chunk // SUBLANES, SUBLANES, width)
            h = (bf_ref[sl, :].reshape(shp) + af_ref[sl, :].reshape(shp) * init_f
                 + br_ref[sl, :].reshape(shp) + ar_ref[sl, :].reshape(shp) * init_r)
            o_ref[sl, :] = (h.reshape(chunk, width) * g_ref[sl, :]).astype(o_ref.dtype)
            return 0

        lax.fori_loop(0, n_chunks, combine, 0)
    return fin_f, fin_r


def _lru_kernel(*refs, ctx_groups, lat_groups, ctx_out):
    xc_ref, gc_ref, xl_ref, gl_ref, cw_ref, cb_ref, wa_ref, wx_ref, ba_ref, bx_ref, lam_ref = refs[:11]
    n_out = 2 if ctx_out else 1
    outs = refs[11:11 + n_out]
    scr = refs[11 + n_out:-2]
    wg_ref, bg_ref = refs[-2:]
    width = xc_ref.shape[1]
    blk = wa_ref.shape[-1]
    wg_ref[...] = jnp.zeros(wg_ref.shape, BF16)
    for t, (w_ref, b_ref, d) in enumerate(((wa_ref, ba_ref, 0), (wx_ref, bx_ref, 0),
                                           (wa_ref, ba_ref, 1), (wx_ref, bx_ref, 1))):
        for p in range(width // blk):
            wg_ref[blk * p:blk * (p + 1), t * width + blk * p:t * width + blk * (p + 1)] = (
                (0.5 * w_ref[d, p]).astype(BF16))
        bg_ref[:, t * width:(t + 1) * width] = 0.5 * b_ref[d:d + 1, :]
    z = -lam_ref[...]
    softplus = jnp.maximum(z, 0.0) + jnp.log1p(jnp.exp(-jnp.abs(z)))
    hnsp = [(-0.5 * LRU_C * LOG2E) * softplus[d:d + 1, :] for d in range(2)]
    consts = (cw_ref, cb_ref, wg_ref, bg_ref, hnsp)
    zero = jnp.zeros((1, xc_ref.shape[1]), F32)
    fin = _lru_sequence(xc_ref, gc_ref, outs[1] if ctx_out else None, (zero, zero), consts, scr,
                        n_groups=ctx_groups)
    _lru_sequence(xl_ref, gl_ref, outs[0], fin, consts, scr, n_groups=lat_groups)


def _lru(x_ctx, g_ctx, x_lat, g_lat, conv_w, conv_b, wa, wx, ba, bx, lam, *, n_seq, ctx_len, lat_len,
         ctx_out, layer):
    w = LRU_SLAB
    n_slab = D_LRU // w
    blk = wa.shape[-1]
    per = w // blk

    def vec(rows):
        return pl.BlockSpec((None, rows, w), lambda b, c: (layer, 0, c))

    gate_w = pl.BlockSpec((None, 2, per, blk, blk), lambda b, c: (layer, 0, c, 0, 0))
    kern = functools.partial(_lru_kernel, ctx_groups=ctx_len // SUBLANES,
                             lat_groups=lat_len // SUBLANES, ctx_out=ctx_out)
    ctx_spec = pl.BlockSpec((ctx_len, w), lambda b, c: (b, c))
    lat_spec = pl.BlockSpec((lat_len, w), lambda b, c: (b, c))
    out_specs = [lat_spec]
    out_shape = [jax.ShapeDtypeStruct((n_seq * lat_len, D_LRU), BF16)]
    if ctx_out:
        out_specs.append(ctx_spec)
        out_shape.append(jax.ShapeDtypeStruct((n_seq * ctx_len, D_LRU), BF16))
    max_rows = max(ctx_len, lat_len)
    return pl.pallas_call(
        kern,
        grid=(n_seq, n_slab),
        in_specs=[ctx_spec, ctx_spec, lat_spec, lat_spec,
                  vec(conv_w.shape[1]), vec(1), gate_w, gate_w, vec(2), vec(2), vec(2)],
        out_specs=out_specs,
        out_shape=out_shape,
        scratch_shapes=[pltpu.VMEM((max_rows + 3 * SUBLANES, w), F32)]
                       + [pltpu.VMEM((max_rows, w), F32)] * 4
                       + [pltpu.VMEM((w, 4 * w), BF16), pltpu.VMEM((1, 4 * w), F32)],
        compiler_params=_params("arbitrary", "arbitrary"),
        name="rglru_bidir",
    )(x_ctx, g_ctx, x_lat, g_lat, conv_w, conv_b, wa, wx, ba, bx, lam)


def _outproj_kernel(at_ref, r_ref, x_ref, gate_ref, nw_ref, w_ref, o_ref):
    n_sub = _n_sub(x_ref.shape[0])
    sub = x_ref.shape[0] // n_sub
    w = w_ref[...].astype(BF16)

    def project(r0):
        rows = pl.ds(r0, sub)
        a = at_ref[:, rows].astype(F32).T
        cat = jnp.concatenate([_rms(a), _rms(r_ref[rows, :].astype(F32))], axis=1) * nw_ref[...]
        return _dot(cat.astype(BF16), w)

    y = project(0)
    for s in range(n_sub):
        y_next = project((s + 1) * sub) if s + 1 < n_sub else None
        rows = pl.ds(s * sub, sub)
        o_ref[rows, :] = x_ref[rows, :] + gate_ref[...] * y
        y = y_next


def _outproj(a, r, x, mod, nw, w_out, *, n_seq, seq_len, shared_mod, mod_row, layer):
    tm = min(1024, seq_len)
    tps = seq_len // tm
    rows = n_seq * seq_len
    gate_idx = (lambda i: (mod_row, 0, 2)) if shared_mod else (lambda i: (mod_row + i // tps, 0, 2))
    return pl.pallas_call(
        _outproj_kernel,
        grid=(rows // tm,),
        in_specs=[pl.BlockSpec((None, D_ATTN, tm), lambda i: (i // tps, 0, i % tps)),
                  pl.BlockSpec((tm, D_LRU), lambda i: (i, 0)),
                  pl.BlockSpec((tm, D_MODEL), lambda i: (i, 0)),
                  pl.BlockSpec((None, 1, D_MODEL), gate_idx),
                  _resident((1, D_MODEL)),
                  _resident((D_MODEL, D_MODEL), layer)],
        out_specs=pl.BlockSpec((tm, D_MODEL), lambda i: (i, 0)),
        out_shape=jax.ShapeDtypeStruct((rows, D_MODEL), F32),
        compiler_params=_params("arbitrary"),
        name="out_proj",
    )(a, r, x, mod, nw, w_out)


def _ffn_kernel(*refs, tm, tps, final_norm):
    x_ref, xp_ref, xn_ref, mod_ref, wup_ref, cw_ref, cb_ref, wdn_ref = refs[:8]
    if final_norm:
        fnw_ref, o_ref, hext_ref, act_ref, ua_ref, ub_ref, slab_ref = refs[8:]
    else:
        o_ref, hext_ref, act_ref, ua_ref, ub_ref = refs[8:]
    assert (D_FF // FF_CHUNK) % 2 == 1
    jt = pl.program_id(0) % tps
    shift = mod_ref[:, 0:D_MODEL]
    scale = mod_ref[:, D_MODEL:2 * D_MODEL]
    gate = mod_ref[:, 2 * D_MODEL:3 * D_MODEL]

    def norm_mod(x):
        return _rms(x) * (1.0 + scale) + shift

    x = x_ref[...]
    hext_ref[pl.ds(BF16_ROWS, tm), :] = norm_mod(x).astype(BF16)
    hp = norm_mod(xp_ref[...])
    hp = jnp.where(jt == 0, _shift_down(hp), hp)
    hn = norm_mod(xn_ref[...])
    hn = jnp.where(jt == tps - 1, _shift_up(hn), hn)
    zeros = jnp.zeros_like(hp)
    hext_ref[pl.ds(0, BF16_ROWS), :] = jnp.concatenate([zeros, hp], axis=0).astype(BF16)
    hext_ref[pl.ds(BF16_ROWS + tm, BF16_ROWS), :] = jnp.concatenate([hn, zeros], axis=0).astype(BF16)

    def up(c, u_ref):
        for k in range(2):
            off = pl.multiple_of(k * D_FF + c * FF_CHUNK, FF_CHUNK)
            u_ref[k] = _dot(hext_ref[...], wup_ref[:, pl.ds(off, FF_CHUNK)])

    def conv(u_ref, k, off):
        sl = pl.ds(off, FF_CHUNK)
        return (cw_ref[0:1, sl] * u_ref[k, pl.ds(SUBLANES, tm), :]
                + cw_ref[1:2, sl] * u_ref[k, pl.ds(2 * SUBLANES, tm), :]
                + cw_ref[2:3, sl] * u_ref[k, pl.ds(3 * SUBLANES, tm), :]
                + cb_ref[:, sl])

    def activate(c, u_ref):
        og = pl.multiple_of(c * FF_CHUNK, FF_CHUNK)
        ov = pl.multiple_of(D_FF + c * FF_CHUNK, FF_CHUNK)
        yg = conv(u_ref, 0, og)
        yv = conv(u_ref, 1, ov)
        hg = 0.5 * yg
        act_ref[:, pl.ds(og, FF_CHUNK)] = ((hg + hg * jnp.tanh(hg)) * yv).astype(BF16)

    n_chunks = D_FF // FF_CHUNK
    up(0, ua_ref)

    def pair(j, _):
        up(2 * j + 1, ub_ref)
        activate(2 * j, ua_ref)
        up(2 * j + 2, ua_ref)
        activate(2 * j + 1, ub_ref)
        return 0

    for j in range((n_chunks - 1) // 2):
        pair(j, 0)
    activate(n_chunks - 1, ua_ref)
    out = x + gate * _dot(act_ref[...], wdn_ref[...])
    if not final_norm:
        o_ref[...] = out
        return
    out = _rms(out) * fnw_ref[...]
    for k in range(D_MODEL // LANES):
        slab_ref[k] = out[:, LANES * k:LANES * (k + 1)]
    for s in range(SUBLANES):
        for k in range(D_MODEL // LANES):
            o_ref[s, :, LANES * k:LANES * (k + 1)] = slab_ref[k, pl.ds(s, tm // SUBLANES, stride=SUBLANES), :]


def _ffn(x, mod, w_up, conv_w, conv_b, w_down, final_w, *, n_seq, seq_len, shared_mod, mod_row, layer):
    tm = min(1024, seq_len)
    tps = seq_len // tm
    rows = n_seq * seq_len
    gps = seq_len // SUBLANES
    gpt = tm // SUBLANES
    mod_idx = (lambda i: (mod_row, 0, 1)) if shared_mod else (lambda i: (mod_row + i // tps, 0, 1))

    def prev_idx(i):
        jt = i % tps
        return ((i // tps) * gps + jnp.where(jt == 0, gps - 1, jt * gpt - 1), 0)

    def next_idx(i):
        jt = i % tps
        return ((i // tps) * gps + jnp.where(jt == tps - 1, 0, (jt + 1) * gpt), 0)

    final_norm = final_w is not None
    in_specs = [pl.BlockSpec((tm, D_MODEL), lambda i: (i, 0)),
                pl.BlockSpec((SUBLANES, D_MODEL), prev_idx),
                pl.BlockSpec((SUBLANES, D_MODEL), next_idx),
                pl.BlockSpec((None, 1, 3 * D_MODEL), mod_idx),
                _resident((D_MODEL, 2 * D_FF), layer),
                _resident((3, 2 * D_FF), layer),
                _resident((1, 2 * D_FF), layer),
                _resident((D_FF, D_MODEL), layer)]
    args = [x, x, x, mod, w_up, conv_w, conv_b, w_down]
    if final_norm:
        in_specs.append(_resident((1, D_MODEL)))
        args.append(final_w)
    kern = functools.partial(_ffn_kernel, tm=tm, tps=tps, final_norm=final_norm)
    scratch = ([pltpu.VMEM((tm + 2 * BF16_ROWS, D_MODEL), BF16), pltpu.VMEM((tm, D_FF), BF16)]
               + [pltpu.VMEM((2, tm + 2 * BF16_ROWS, FF_CHUNK), F32)] * 2)
    if final_norm:
        out_spec = pl.BlockSpec((None, SUBLANES, tm // SUBLANES, D_MODEL),
                                lambda i: (i // tps, 0, i % tps, 0))
        out_shape = jax.ShapeDtypeStruct((n_seq, SUBLANES, gps, D_MODEL), F32)
        scratch.append(pltpu.VMEM((D_MODEL // LANES, tm, LANES), F32))
    else:
        out_spec = pl.BlockSpec((tm, D_MODEL), lambda i: (i, 0))
        out_shape = jax.ShapeDtypeStruct((rows, D_MODEL), F32)
    return pl.pallas_call(
        kern,
        grid=(rows // tm,),
        in_specs=in_specs,
        out_specs=out_spec,
        out_shape=out_shape,
        scratch_shapes=scratch,
        compiler_params=_params("arbitrary"),
        name="conv_ffn_final" if final_norm else "conv_ffn",
    )(*args)


def _to_segments(x):
    b, t, c = x.shape
    return x.reshape(b, SUBLANES, t // SUBLANES, c).transpose(0, 2, 1, 3).reshape(b * t, c)


def _rope_tables(seq_len):
    r = np.arange(seq_len)
    t = (r % SUBLANES) * (seq_len // SUBLANES) + r // SUBLANES
    row = (t // GRID_W).astype(np.float64)
    col = (t % GRID_W).astype(np.float64)
    pairs = HEAD_DIM // 4
    inv = ROPE_THETA ** (-np.arange(pairs, dtype=np.float64) / pairs)
    ang = np.concatenate([row[:, None] * inv, col[:, None] * inv], axis=-1)
    cos, sin = np.cos(ang), np.sin(ang)
    return (jnp.asarray(np.concatenate([cos, cos, cos, cos], axis=-1), F32),
            jnp.asarray(np.concatenate([-sin, sin, -sin, sin], axis=-1), F32))


def kernel(x, c, ctx, c_ctx, w_ada, b_ada, w_in, q_norm_w, k_norm_w, lru_conv_w, lru_conv_b, lru_wa,
           lru_ba, lru_wx, lru_bx, lru_lambda, attn_out_norm_w, lru_out_norm_w, w_out, ffn_w_up,
           ffn_conv_w, ffn_conv_b, ffn_w_down, final_norm_w):
    batch, seq, _ = x.shape
    ctx_len = ctx.shape[1]
    depth = w_in.shape[0]

    mod_rows = -(-(batch + 1) // SUBLANES) * SUBLANES
    cvec = jnp.zeros((mod_rows, D_MODEL), F32).at[:batch].set(c).at[batch].set(c_ctx)
    mods = _modulation(cvec, w_ada, b_ada)

    cos, sin = _rope_tables(seq)
    ones = jnp.asarray(np.ones((ctx_len, LANES)), F32)
    bd = jnp.asarray(np.kron(np.eye(QK_W // HEAD_DIM), np.ones((HEAD_DIM, HEAD_DIM))), BF16)

    x_lat = _to_segments(x)
    x_ctx = _to_segments(ctx)
    mods = mods.reshape(depth * mod_rows, 1, 6 * D_MODEL)
    w_up_b = ffn_w_up.astype(BF16)
    w_dn_b = ffn_w_down.astype(BF16)
    conv_b = ffn_conv_b.reshape(depth, 1, 2 * D_FF)

    for l in range(depth):
        ctx_out = l < depth - 1
        last = l == depth - 1
        lat = dict(n_seq=batch, seq_len=seq, shared_mod=False, mod_row=l * mod_rows, layer=l)
        cx = dict(n_seq=batch, seq_len=ctx_len, shared_mod=True, mod_row=l * mod_rows + batch, layer=l)
        nw_qk = jnp.concatenate([jnp.tile(q_norm_w[l], D_ATTN // HEAD_DIM),
                                 jnp.tile(k_norm_w[l], KV_W // HEAD_DIM)]).reshape(1, QK_W)
        qt_l, k_l, vt_l, xr_l, g_l = _inproj(x_lat, mods, w_in, bd, nw_qk, cos, sin,
                                             use_rope=True, **lat)
        qt_c, k_c, vt_c, xr_c, g_c = _inproj(x_ctx, mods, w_in, bd, nw_qk, ones, ones,
                                             use_rope=False, **cx)

        a_l = _attention(qt_l, k_c, vt_c, k_l, vt_l, n_seq=batch, q_len=seq, ctx_len=ctx_len,
                         lat_len=seq)
        r = _lru(xr_c, g_c, xr_l, g_l, lru_conv_w, lru_conv_b.reshape(depth, 1, D_LRU), lru_wa, lru_wx,
                 lru_ba, lru_bx, lru_lambda, n_seq=batch, ctx_len=ctx_len, lat_len=seq,
                 ctx_out=ctx_out, layer=l)

        nw_out = jnp.concatenate([attn_out_norm_w[l], lru_out_norm_w[l]]).reshape(1, D_MODEL)
        x_lat = _outproj(a_l, r[0], x_lat, mods, nw_out, w_out, **lat)
        x_lat = _ffn(x_lat, mods, w_up_b, ffn_conv_w, conv_b, w_dn_b,
                     final_norm_w.reshape(1, D_MODEL) if last else None, **lat)
        if ctx_out:
            a_c = _attention(qt_c, k_c, vt_c, None, None, n_seq=batch, q_len=ctx_len,
                             ctx_len=ctx_len, lat_len=0)
            x_ctx = _outproj(a_c, r[1], x_ctx, mods, nw_out, w_out, **cx)
            x_ctx = _ffn(x_ctx, mods, w_up_b, ffn_conv_w, conv_b, w_dn_b, None, **cx)
    return x_lat.reshape(batch, seq, D_MODEL)
```
